```python
import math
import jax
import jax.numpy as jnp
from jax import lax
import numpy as np

D_MODEL = 1024
BATCH = 8
SEQ = 8192
DEPTH = 1
DEC_BATCH = 8
DEC_SEQ = 32
PAST_LEN = 2048

CHUNK = 64
D_MIX = D_MODEL
SSD_WIDTH = D_MIX // 2
SSD_HEAD_DIM = 64
SSD_HEADS = SSD_WIDTH // SSD_HEAD_DIM
SSD_GROUPS = 2
SSD_HEADS_PER_GROUP = SSD_HEADS // SSD_GROUPS
SSD_STATE = 128
SSD_CONV = 4
SSD_CONV_DIM = SSD_WIDTH + 2 * SSD_GROUPS * SSD_STATE
SSD_CHUNK = CHUNK
S5_WIDTH = D_MIX - SSD_WIDTH
S5_GROUP_CH = 16
S5_GROUPS = S5_WIDTH // S5_GROUP_CH
S5_STATE = 64
D_FF = 2816
FFN_CONV = 3
D_IN = SSD_WIDTH + SSD_CONV_DIM + SSD_HEADS + S5_WIDTH
EPS = 1e-6

kernel_name = "hybrid_ssd_s5_streaming_step"


def rmsnorm(x, w):
    xf = x.astype(jnp.float32)
    xf = xf * lax.rsqrt(jnp.mean(xf * xf, axis=-1, keepdims=True) + EPS)
    return (xf * w.astype(jnp.float32)).astype(x.dtype)


def causal_dwconv(u, hist, w, b):
    k = w.shape[0]
    length = u.shape[1]
    full = jnp.concatenate([hist.astype(u.dtype), u], axis=1)
    out = b + full[:, 0:length] * w[0]
    for i in range(1, k):
        out = out + full[:, i:i + length] * w[i]
    return out, full[:, length:]


def ssd_scan(xs, dt, a, bm, cm, h0):
    bsz, length, nh, hd = xs.shape
    q = SSD_CHUNK if length % SSD_CHUNK == 0 else length
    nc = length // q
    x_c = xs.reshape(bsz, nc, q, nh, hd)
    dt_c = dt.reshape(bsz, nc, q, nh)
    b_c = bm.reshape(bsz, nc, q, nh, SSD_STATE)
    c_c = cm.reshape(bsz, nc, q, nh, SSD_STATE)
    cs = jnp.cumsum(dt_c * a, axis=2)
    causal = jnp.tril(jnp.ones((q, q), dtype=bool))[None, None, :, :, None]
    seg = cs[:, :, :, None, :] - cs[:, :, None, :, :]
    decay = jnp.exp(jnp.where(causal, seg, -jnp.inf))
    scores = jnp.einsum("bclhn,bcshn->bclsh", c_c, b_c)
    y_diag = jnp.einsum("bclsh,bcshp->bclhp", scores * decay * dt_c[:, :, None, :, :], x_c)
    w_end = jnp.exp(cs[:, :, -1:, :] - cs) * dt_c
    states = jnp.einsum("bcshn,bcsh,bcshp->bchpn", b_c, w_end, x_c)
    chunk_decay = jnp.exp(cs[:, :, -1, :])

    def step(h, inp):
        st, dcy = inp
        return h * dcy[:, :, None, None] + st, h

    h_final, h_prev = lax.scan(step, h0.astype(states.dtype),
                               (jnp.moveaxis(states, 1, 0), jnp.moveaxis(chunk_decay, 1, 0)))
    h_prev = jnp.moveaxis(h_prev, 0, 1)
    y_off = jnp.einsum("bclhn,bchpn,bclh->bclhp", c_c, h_prev, jnp.exp(cs))
    return (y_diag + y_off).reshape(bsz, length, nh, hd), h_final


def ssd_mixer(z, xbc, dt_raw, conv_hist, h0, conv_w, conv_b, dt_bias, a_log, d, norm_w):
    bsz, length, _ = z.shape
    xbc, new_conv = causal_dwconv(xbc, conv_hist, conv_w, conv_b)
    xbc = jax.nn.silu(xbc)
    xs, bm, cm = jnp.split(xbc, [SSD_WIDTH, SSD_WIDTH + SSD_GROUPS * SSD_STATE], axis=-1)
    xs = xs.reshape(bsz, length, SSD_HEADS, SSD_HEAD_DIM)
    bm = jnp.repeat(bm.reshape(bsz, length, SSD_GROUPS, SSD_STATE), SSD_HEADS_PER_GROUP, axis=2)
    cm = jnp.repeat(cm.reshape(bsz, length, SSD_GROUPS, SSD_STATE), SSD_HEADS_PER_GROUP, axis=2)
    dt = jax.nn.softplus(dt_raw + dt_bias)
    a = -jnp.exp(a_log)
    y, h_new = ssd_scan(xs, dt, a, bm, cm, h0)
    y = y + d[:, None] * xs
    g = (y.reshape(bsz, length, SSD_WIDTH) * jax.nn.silu(z)).reshape(bsz, length, SSD_GROUPS, -1)
    g = rmsnorm(g, norm_w.reshape(SSD_GROUPS, -1)).reshape(bsz, length, SSD_WIDTH)
    return g, new_conv, h_new


def complex_affine_combine(e1, e2):
    a1r, a1i, b1r, b1i = e1
    a2r, a2i, b2r, b2i = e2
    return (a2r * a1r - a2i * a1i,
            a2r * a1i + a2i * a1r,
            a2r * b1r - a2i * b1i + b2r,
            a2r * b1i + a2i * b1r + b2i)


def s5_mixer(u, h0_re, h0_im, lam_re, lam_im, log_dt, b_re, b_im, c_re, c_im, d, glu_w, glu_b):
    bsz, length, _ = u.shape
    ug = u.reshape(bsz, length, S5_GROUPS, S5_GROUP_CH)
    dt = jnp.exp(log_dt)[:, None]
    mag = jnp.exp(lam_re * dt)
    ang = lam_im * dt
    lb_re = mag * jnp.cos(ang)
    lb_im = mag * jnp.sin(ang)
    den = lam_re * lam_re + lam_im * lam_im
    q_re = ((lb_re - 1) * lam_re + lb_im * lam_im) / den
    q_im = (lb_im * lam_re - (lb_re - 1) * lam_im) / den
    bb_re = q_re[..., None] * b_re - q_im[..., None] * b_im
    bb_im = q_re[..., None] * b_im + q_im[..., None] * b_re
    bu_re = jnp.einsum("blgc,gpc->blgp", ug, bb_re)
    bu_im = jnp.einsum("blgc,gpc->blgp", ug, bb_im)
    first_re = bu_re[:, 0] + lb_re * h0_re - lb_im * h0_im
    first_im = bu_im[:, 0] + lb_re * h0_im + lb_im * h0_re
    bu_re = bu_re.at[:, 0].set(first_re.astype(bu_re.dtype))
    bu_im = bu_im.at[:, 0].set(first_im.astype(bu_im.dtype))
    a_re = jnp.broadcast_to(lb_re, (1, length, S5_GROUPS, S5_STATE))
    a_im = jnp.broadcast_to(lb_im, (1, length, S5_GROUPS, S5_STATE))
    _, _, h_re, h_im = lax.associative_scan(complex_affine_combine, (a_re, a_im, bu_re, bu_im), axis=1)
    y = (jnp.einsum("blgp,gcp->blgc", h_re, c_re) - jnp.einsum("blgp,gcp->blgc", h_im, c_im)
         + d * ug)
    g = jnp.einsum("blgc,gck->blgk", jax.nn.gelu(y, approximate=True), glu_w) + glu_b
    out = g[..., :S5_GROUP_CH] * jax.nn.sigmoid(g[..., S5_GROUP_CH:])
    return out.reshape(bsz, length, S5_WIDTH), h_re[:, -1], h_im[:, -1]


def hybrid_layer(x, conv_hist, ssd_h0, s5_h0_re, s5_h0_im, ffn_hist,
                 pre_mix_norm_w, w_in, ssd_conv_w, ssd_conv_b, ssd_dt_bias, ssd_a_log, ssd_d, ssd_norm_w,
                 s5_lambda_re, s5_lambda_im, s5_log_dt, s5_b_re, s5_b_im, s5_c_re, s5_c_im, s5_d,
                 s5_glu_w, s5_glu_b, w_out, post_mix_norm_w, pre_ffn_norm_w, w_up, ffn_conv_w, ffn_conv_b,
                 w_down, post_ffn_norm_w):
    xn = rmsnorm(x, pre_mix_norm_w)
    proj = xn @ w_in
    z, xbc, dt_raw, u_s5 = jnp.split(
        proj, [SSD_WIDTH, SSD_WIDTH + SSD_CONV_DIM, SSD_WIDTH + SSD_CONV_DIM + SSD_HEADS], axis=-1)
    y_ssd, new_conv, new_ssd = ssd_mixer(z, xbc, dt_raw, conv_hist, ssd_h0, ssd_conv_w, ssd_conv_b,
                                         ssd_dt_bias, ssd_a_log, ssd_d, ssd_norm_w)
    y_s5, new_re, new_im = s5_mixer(u_s5, s5_h0_re, s5_h0_im, s5_lambda_re, s5_lambda_im, s5_log_dt,
                                    s5_b_re, s5_b_im, s5_c_re, s5_c_im, s5_d, s5_glu_w, s5_glu_b)
    mix = jnp.concatenate([y_ssd, y_s5], axis=-1) @ w_out
    h = x + rmsnorm(mix, post_mix_norm_w)
    up, new_ffn = causal_dwconv(rmsnorm(h, pre_ffn_norm_w) @ w_up, ffn_hist, ffn_conv_w, ffn_conv_b)
    gate, val = jnp.split(up, 2, axis=-1)
    ffn = (jax.nn.gelu(gate, approximate=True) * val) @ w_down
    y = h + rmsnorm(ffn, post_ffn_norm_w)
    return y, new_conv, new_ssd, new_re, new_im, new_ffn


def setup_inputs(seed: int = 0) -> dict:
    key = jax.random.key(seed)
    ks = iter(jax.random.split(key, 48))

    def nrm(shape, scale):
        return jax.random.normal(next(ks), shape, jnp.float32) * scale

    def gain(shape):
        return 1.0 + nrm(shape, 0.05)

    dt0 = jnp.exp(jax.random.uniform(next(ks), (DEPTH, SSD_HEADS), jnp.float32,
                                     minval=math.log(1e-3), maxval=math.log(1e-1)))
    n_idx = jnp.arange(S5_STATE, dtype=jnp.float32)
    return {
        "x_prompt": nrm((BATCH, SEQ, D_MODEL), 1.0),
        "x_sample": nrm((DEC_BATCH, DEC_SEQ, D_MODEL), 1.0),
        "cache_ssd_conv": nrm((DEPTH, DEC_BATCH, SSD_CONV - 1, SSD_CONV_DIM), 1.0),
        "state_ssd": nrm((DEPTH, DEC_BATCH, SSD_HEADS, SSD_HEAD_DIM, SSD_STATE), 0.1),
        "state_s5_re": nrm((DEPTH, DEC_BATCH, S5_GROUPS, S5_STATE), 0.5),
        "state_s5_im": nrm((DEPTH, DEC_BATCH, S5_GROUPS, S5_STATE), 0.5),
        "cache_ffn_conv": nrm((DEPTH, DEC_BATCH, FFN_CONV - 1, 2 * D_FF), 1.0),
        "pre_mix_norm_w": gain((DEPTH, D_MODEL)),
        "w_in": nrm((DEPTH, D_MODEL, D_IN), D_MODEL ** -0.5),
        "ssd_conv_w": nrm((DEPTH, SSD_CONV, SSD_CONV_DIM), 0.3),
        "ssd_conv_b": nrm((DEPTH, SSD_CONV_DIM), 0.02),
        "ssd_dt_bias": dt0 + jnp.log(-jnp.expm1(-dt0)),
        "ssd_a_log": jnp.log(jax.random.uniform(next(ks), (DEPTH, SSD_HEADS), jnp.float32, minval=1.0, maxval=16.0)),
        "ssd_d": 1.0 + nrm((DEPTH, SSD_HEADS), 0.1),
        "ssd_norm_w": gain((DEPTH, SSD_WIDTH)),
        "s5_lambda_re": -0.5 + nrm((DEPTH, S5_GROUPS, S5_STATE), 0.01),
        "s5_lambda_im": math.pi * n_idx + nrm((DEPTH, S5_GROUPS, S5_STATE), 0.01),
        "s5_log_dt": jax.random.uniform(next(ks), (DEPTH, S5_GROUPS), jnp.float32,
                                        minval=math.log(1e-3), maxval=math.log(1e-1)),
        "s5_b_re": nrm((DEPTH, S5_GROUPS, S5_STATE, S5_GROUP_CH), (2 * S5_GROUP_CH) ** -0.5),
        "s5_b_im": nrm((DEPTH, S5_GROUPS, S5_STATE, S5_GROUP_CH), (2 * S5_GROUP_CH) ** -0.5),
        "s5_c_re": nrm((DEPTH, S5_GROUPS, S5_GROUP_CH, S5_STATE), (2 * S5_STATE) ** -0.5),
        "s5_c_im": nrm((DEPTH, S5_GROUPS, S5_GROUP_CH, S5_STATE), (2 * S5_STATE) ** -0.5),
        "s5_d": nrm((DEPTH, S5_GROUPS, S5_GROUP_CH), 0.5),
        "s5_glu_w": nrm((DEPTH, S5_GROUPS, S5_GROUP_CH, 2 * S5_GROUP_CH), S5_GROUP_CH ** -0.5),
        "s5_glu_b": nrm((DEPTH, S5_GROUPS, 2 * S5_GROUP_CH), 0.02),
        "w_out": nrm((DEPTH, D_MIX, D_MODEL), D_MIX ** -0.5),
        "post_mix_norm_w": gain((DEPTH, D_MODEL)),
        "pre_ffn_norm_w": gain((DEPTH, D_MODEL)),
        "w_up": nrm((DEPTH, D_MODEL, 2 * D_FF), D_MODEL ** -0.5),
        "ffn_conv_w": nrm((DEPTH, FFN_CONV, 2 * D_FF), 0.5),
        "ffn_conv_b": nrm((DEPTH, 2 * D_FF), 0.02),
        "w_down": nrm((DEPTH, D_FF, D_MODEL), D_FF ** -0.5),
        "post_ffn_norm_w": gain((DEPTH, D_MODEL)),
    }


def reference(x_prompt, x_sample, cache_ssd_conv, state_ssd, state_s5_re, state_s5_im, cache_ffn_conv,
              pre_mix_norm_w, w_in, ssd_conv_w, ssd_conv_b, ssd_dt_bias, ssd_a_log, ssd_d, ssd_norm_w,
              s5_lambda_re, s5_lambda_im, s5_log_dt, s5_b_re, s5_b_im, s5_c_re, s5_c_im, s5_d,
              s5_glu_w, s5_glu_b, w_out, post_mix_norm_w, pre_ffn_norm_w, w_up, ffn_conv_w, ffn_conv_b,
              w_down, post_ffn_norm_w):
    bsz = x_prompt.shape[0]
    dtp = x_prompt.dtype
    y_prompt, y_sample = x_prompt, x_sample
    p_conv, p_ssd, p_re, p_im, p_ffn = [], [], [], [], []
    s_conv, s_ssd, s_re, s_im, s_ffn = [], [], [], [], []
    for l in range(DEPTH):
        lw = (pre_mix_norm_w[l], w_in[l], ssd_conv_w[l], ssd_conv_b[l], ssd_dt_bias[l], ssd_a_log[l],
              ssd_d[l], ssd_norm_w[l], s5_lambda_re[l], s5_lambda_im[l], s5_log_dt[l], s5_b_re[l],
              s5_b_im[l], s5_c_re[l], s5_c_im[l], s5_d[l], s5_glu_w[l], s5_glu_b[l], w_out[l],
              post_mix_norm_w[l], pre_ffn_norm_w[l], w_up[l], ffn_conv_w[l], ffn_conv_b[l], w_down[l],
              post_ffn_norm_w[l])
        y_prompt, pc, ph, pr, pi_, pf = hybrid_layer(
            y_prompt,
            jnp.zeros((bsz, SSD_CONV - 1, SSD_CONV_DIM), dtp),
            jnp.zeros((bsz, SSD_HEADS, SSD_HEAD_DIM, SSD_STATE), dtp),
            jnp.zeros((bsz, S5_GROUPS, S5_STATE), dtp),
            jnp.zeros((bsz, S5_GROUPS, S5_STATE), dtp),
            jnp.zeros((bsz, FFN_CONV - 1, 2 * D_FF), dtp),
            *lw)
        y_sample, sc, sh, sr, si, sf = hybrid_layer(
            y_sample, cache_ssd_conv[l], state_ssd[l], state_s5_re[l], state_s5_im[l], cache_ffn_conv[l], *lw)
        p_conv.append(pc); p_ssd.append(ph); p_re.append(pr); p_im.append(pi_); p_ffn.append(pf)
        s_conv.append(sc); s_ssd.append(sh); s_re.append(sr); s_im.append(si); s_ffn.append(sf)
    new_ssd_conv_prompt = jnp.stack(p_conv)
    new_ssd_state_prompt = jnp.stack(p_ssd)
    new_s5_re_prompt = jnp.stack(p_re)
    new_s5_im_prompt = jnp.stack(p_im)
    new_ffn_conv_prompt = jnp.stack(p_ffn)
    new_ssd_conv_sample = jnp.stack(s_conv)
    new_ssd_state_sample = jnp.stack(s_ssd)
    new_s5_re_sample = jnp.stack(s_re)
    new_s5_im_sample = jnp.stack(s_im)
    new_ffn_conv_sample = jnp.stack(s_ffn)
    return (y_prompt, y_sample,
            new_ssd_conv_prompt, new_ssd_state_prompt, new_s5_re_prompt, new_s5_im_prompt, new_ffn_conv_prompt,
            new_ssd_conv_sample, new_ssd_state_sample, new_s5_re_sample, new_s5_im_sample, new_ffn_conv_sample)
```

```python
import functools
import math

import jax
import jax.numpy as jnp
from jax import lax
from jax.experimental import pallas as pl
from jax.experimental.pallas import tpu as pltpu

F32 = jnp.float32
BF16 = jnp.bfloat16

D_MODEL = 1024
SSD_WIDTH = 512
SSD_HEAD_DIM = 64
SSD_HEADS = 8
SSD_GROUPS = 2
SSD_HPG = SSD_HEADS // SSD_GROUPS
SSD_GROUP_W = SSD_WIDTH // SSD_GROUPS
SSD_STATE = 128
SSD_CONV = 4
SSD_CONV_DIM = SSD_WIDTH + 2 * SSD_GROUPS * SSD_STATE
S5_WIDTH = 512
S5_GROUP_CH = 16
S5_GROUPS = 32
S5_STATE = 64
S5_BLOCKS = 4
S5_GPB = S5_GROUPS // S5_BLOCKS
S5_BLOCK_STATE = S5_GPB * S5_STATE
D_FF = 2816
FFN_CONV = 3
EPS = 1e-6

LANES = 128
SUBLANES = 8
HIST_ROWS = SUBLANES
VMEM_LIMIT = 56 * 1024 * 1024

FFN_CHUNK = 256


def _rms(x, w):
    ms = jnp.mean(x * x, axis=-1, keepdims=True)
    return x * lax.rsqrt(ms + EPS) * w


def _gelu_tanh(x):
    c = math.sqrt(2.0 / math.pi)
    return 0.5 * x * (1.0 + jnp.tanh(c * (x + 0.044715 * (x * x * x))))


def _softplus(x):
    return jnp.maximum(x, 0.0) + jnp.log1p(jnp.exp(-jnp.abs(x)))


def _const_spec(shape):
    nd = len(shape)
    return pl.BlockSpec(shape, lambda *_: (0,) * nd, pipeline_mode=pl.Buffered(1))


def _params(n_axes):
    return pltpu.CompilerParams(dimension_semantics=("arbitrary",) * n_axes, vmem_limit_bytes=VMEM_LIMIT)


def _inproj_kernel(x_ref, nw_ref, wa_ref, wdt_ref, wu_ref, z_ref, xbc_ref, dt_ref, u_ref):
    x = x_ref[0]
    xn = _rms(x, nw_ref[...]).astype(BF16)
    a = jnp.dot(xn, wa_ref[...], preferred_element_type=F32)
    z_ref[0] = a[:, :SSD_WIDTH].astype(BF16)
    xbc_ref[0] = a[:, SSD_WIDTH:].astype(BF16)
    dt_ref[0] = jnp.dot(xn, wdt_ref[...], preferred_element_type=F32)
    u_ref[0] = jnp.dot(xn, wu_ref[...], preferred_element_type=F32).astype(BF16)


def _inproj(x, nw, wa, wdt, wu, tm):
    bsz, length, _ = x.shape
    grid = (bsz, length // tm)
    tok = lambda w, dt: (jax.ShapeDtypeStruct((bsz, length, w), dt), pl.BlockSpec((1, tm, w), lambda b, i: (b, i, 0)))
    outs = [tok(SSD_WIDTH, BF16), tok(SSD_CONV_DIM, BF16), tok(LANES, F32), tok(S5_WIDTH, BF16)]
    return pl.pallas_call(
        _inproj_kernel,
        grid=grid,
        in_specs=[pl.BlockSpec((1, tm, D_MODEL), lambda b, i: (b, i, 0)),
                  _const_spec(nw.shape), _const_spec(wa.shape), _const_spec(wdt.shape), _const_spec(wu.shape)],
        out_specs=[o[1] for o in outs],
        out_shape=[o[0] for o in outs],
        compiler_params=_params(2),
        name="inproj",
    )(x, nw, wa, wdt, wu)


def _split3(v):
    hi = v.astype(BF16)
    r1 = v - hi.astype(F32)
    mid = r1.astype(BF16)
    lo = (r1 - mid.astype(F32)).astype(BF16)
    return hi, mid, lo


def _head_cols(a):
    q = a.shape[0]
    lane = lax.broadcasted_iota(jnp.int32, (q, LANES), 1)
    tiles = []
    for j in range(SSD_HEADS * SSD_HEAD_DIM // LANES):
        even = jnp.broadcast_to(a[:, 2 * j:2 * j + 1], (q, LANES))
        odd = jnp.broadcast_to(a[:, 2 * j + 1:2 * j + 2], (q, LANES))
        tiles.append(jnp.where(lane < SSD_HEAD_DIM, even, odd))
    return jnp.concatenate(tiles, axis=-1)


def _ssd_kernel(z_ref, xbc_ref, dt_ref, hist_ref, h0_ref, cw_ref, cb_ref, dtb_ref, alog_ref, d_ref, nw_ref,
                y_ref, st_out_ref, ext_scr, st_scr, *, q, valid):
    t = pl.program_id(1)

    @pl.when(t == 0)
    def _():
        ext_scr[0:HIST_ROWS, :] = hist_ref[0]
        st_scr[...] = h0_ref[0]

    xraw = xbc_ref[0].astype(F32)
    ext_scr[HIST_ROWS:HIST_ROWS + q, :] = xraw
    conv = cb_ref[...] + xraw * cw_ref[SSD_CONV - 1:SSD_CONV, :]
    for k in range(1, SSD_CONV):
        conv = conv + ext_scr[HIST_ROWS - k:HIST_ROWS - k + q, :] * cw_ref[SSD_CONV - 1 - k:SSD_CONV - k, :]
    ext_scr[0:HIST_ROWS, :] = ext_scr[q:q + HIST_ROWS, :]
    xact = conv * jax.nn.sigmoid(conv)
    xs = xact[:, :SSD_WIDTH]
    xs_bf = xs.astype(BF16)

    dt = _softplus(dt_ref[0] + dtb_ref[...])
    row = lax.broadcasted_iota(jnp.int32, (q, q), 0)
    col = lax.broadcasted_iota(jnp.int32, (q, q), 1)
    causal = row >= col
    if valid < q:
        dt = jnp.where(lax.broadcasted_iota(jnp.int32, dt.shape, 0) < valid, dt, 0.0)
    a = -jnp.exp(alog_ref[...])
    tril = jnp.where(causal, 1.0, 0.0).astype(BF16)
    cs = sum(jnp.dot(tril, part, preferred_element_type=F32) for part in _split3(dt * a))
    cs_t = cs.T
    dt_t = dt.T
    cs_w = _head_cols(cs)
    dt_w = _head_cols(dt)
    cs_last_w = cs_w[q - 1:q, :]
    decay_in = jnp.exp(cs_w)
    w_end = jnp.exp(cs_last_w - cs_w) * dt_w
    chunk_decay = jnp.exp(cs_last_w)

    lane = lax.broadcasted_iota(jnp.int32, (1, SSD_GROUP_W), 1)
    ys = []
    for g in range(SSD_GROUPS):
        b_g = xact[:, SSD_WIDTH + g * SSD_STATE:SSD_WIDTH + (g + 1) * SSD_STATE].astype(BF16)
        c_off = SSD_WIDTH + SSD_GROUPS * SSD_STATE
        c_g = xact[:, c_off + g * SSD_STATE:c_off + (g + 1) * SSD_STATE].astype(BF16)
        gs = slice(g * SSD_GROUP_W, (g + 1) * SSD_GROUP_W)
        x_g = xs_bf[:, gs]
        scores = lax.dot_general(c_g, b_g, (((1,), (1,)), ((), ())), preferred_element_type=F32)
        st_g = st_scr[:, gs]
        y_g = jnp.dot(c_g, st_g.astype(BF16), preferred_element_type=F32) * decay_in[:, gs]
        for j in range(SSD_HPG):
            h = g * SSD_HPG + j
            seg = cs[:, h:h + 1] - cs_t[h:h + 1, :]
            lmat = jnp.where(causal, jnp.exp(seg), 0.0) * dt_t[h:h + 1, :]
            m = (scores * lmat).astype(BF16)
            in_head = (lane >= j * SSD_HEAD_DIM) & (lane < (j + 1) * SSD_HEAD_DIM)
            x_h = jnp.where(in_head, x_g, jnp.zeros_like(x_g))
            y_g = y_g + jnp.dot(m, x_h, preferred_element_type=F32)
        ys.append(y_g)
        wx = (xs[:, gs] * w_end[:, gs]).astype(BF16)
        new = lax.dot_general(b_g, wx, (((0,), (0,)), ((), ())), preferred_element_type=F32)
        st_scr[:, gs] = st_g * chunk_decay[:, gs] + new
    y = jnp.concatenate(ys, axis=-1) + d_ref[...] * xs

    zf = z_ref[0].astype(F32)
    gz = y * (zf * jax.nn.sigmoid(zf))
    nw = nw_ref[...]
    outs = []
    for g in range(SSD_GROUPS):
        gs = slice(g * SSD_GROUP_W, (g + 1) * SSD_GROUP_W)
        outs.append(_rms(gz[:, gs], nw[:, gs]))
    y_ref[0] = jnp.concatenate(outs, axis=-1).astype(BF16)

    @pl.when(t == pl.num_programs(1) - 1)
    def _():
        st_out_ref[0] = st_scr[...]


def _ssd(z, xbc, dt, hist, h0, cw, cb, dtb, alog, dvec, nw, q, valid):
    bsz, length, _ = z.shape
    grid = (bsz, length // q)
    tok = lambda w: pl.BlockSpec((1, q, w), lambda b, i: (b, i, 0))
    per_b = lambda r, w: pl.BlockSpec((1, r, w), lambda b, i: (b, 0, 0))
    return pl.pallas_call(
        functools.partial(_ssd_kernel, q=q, valid=valid),
        grid=grid,
        in_specs=[tok(SSD_WIDTH), tok(SSD_CONV_DIM), tok(LANES), per_b(HIST_ROWS, SSD_CONV_DIM),
                  per_b(SSD_STATE, SSD_WIDTH), _const_spec(cw.shape), _const_spec(cb.shape), _const_spec(dtb.shape),
                  _const_spec(alog.shape), _const_spec(dvec.shape), _const_spec(nw.shape)],
        out_specs=[tok(SSD_WIDTH), per_b(SSD_STATE, SSD_WIDTH)],
        out_shape=[jax.ShapeDtypeStruct((bsz, length, SSD_WIDTH), BF16),
                   jax.ShapeDtypeStruct((bsz, SSD_STATE, SSD_WIDTH), F32)],
        scratch_shapes=[pltpu.VMEM((q + HIST_ROWS, SSD_CONV_DIM), F32), pltpu.VMEM((SSD_STATE, SSD_WIDTH), F32)],
        compiler_params=_params(2),
        name="ssd",
    )(z, xbc, dt, hist, h0, cw, cb, dtb, alog, dvec, nw)


def _s5_kernel(u_ref, h0re_ref, h0im_ref, lr_ref, li_ref, bb_ref, cc_ref, d_ref, gw_ref, gb_ref,
               y_ref, hre_out_ref, him_out_ref, hs_scr, hre_scr, him_scr, *, tt, bsz):
    i = pl.program_id(0)

    @pl.when(i == 0)
    def _():
        hre_scr[...] = h0re_ref[...]
        him_scr[...] = h0im_ref[...]

    rows = tt * bsz
    u = u_ref[...].reshape(rows, S5_WIDTH)
    u_bf = u.astype(BF16)
    half = S5_BLOCK_STATE
    for kb in range(S5_BLOCKS):
        c0 = kb * 2 * half
        hs_scr[:, c0:c0 + 2 * half] = jnp.dot(u_bf[:, kb * LANES:(kb + 1) * LANES], bb_ref[kb],
                                              preferred_element_type=F32)

    for kb in range(S5_BLOCKS):
        c0 = kb * 2 * half
        lr = jnp.broadcast_to(lr_ref[kb], (bsz, half))
        li = jnp.broadcast_to(li_ref[kb], (bsz, half))

        def body(t, carry, c0=c0, lr=lr, li=li):
            hr, hi = carry
            r0 = pl.multiple_of(t * bsz, bsz)
            bur = hs_scr[pl.ds(r0, bsz), c0:c0 + half]
            bui = hs_scr[pl.ds(r0, bsz), c0 + half:c0 + 2 * half]
            nr = lr * hr - li * hi + bur
            ni = lr * hi + li * hr + bui
            hs_scr[pl.ds(r0, bsz), c0:c0 + half] = nr
            hs_scr[pl.ds(r0, bsz), c0 + half:c0 + 2 * half] = ni
            return nr, ni

        hr, hi = lax.fori_loop(0, tt, body, (hre_scr[kb], him_scr[kb]), unroll=8)
        hre_scr[kb] = hr
        him_scr[kb] = hi

    ys = []
    for kb in range(S5_BLOCKS):
        c0 = kb * 2 * half
        ys.append(jnp.dot(hs_scr[:, c0:c0 + 2 * half].astype(BF16), cc_ref[kb], preferred_element_type=F32))
    y = jnp.concatenate(ys, axis=-1) + d_ref[...] * u
    ge = _gelu_tanh(y).astype(BF16)
    outs = []
    for kb in range(S5_BLOCKS):
        gl = jnp.dot(ge[:, kb * LANES:(kb + 1) * LANES], gw_ref[kb], preferred_element_type=F32) + gb_ref[kb]
        outs.append(gl[:, :LANES] * jax.nn.sigmoid(gl[:, LANES:]))
    y_ref[...] = jnp.concatenate(outs, axis=-1).reshape(tt, bsz, S5_WIDTH)

    @pl.when(i == pl.num_programs(0) - 1)
    def _():
        hre_out_ref[...] = hre_scr[...]
        him_out_ref[...] = him_scr[...]


def _s5(u3, h0re, h0im, lr, li, bb, cc, dvec, gw, gb, tt):
    length, bsz, _ = u3.shape
    grid = (length // tt,)
    st_shape = (S5_BLOCKS, bsz, S5_BLOCK_STATE)
    tok = pl.BlockSpec((tt, bsz, S5_WIDTH), lambda i: (i, 0, 0))
    return pl.pallas_call(
        functools.partial(_s5_kernel, tt=tt, bsz=bsz),
        grid=grid,
        in_specs=[tok, _const_spec(st_shape), _const_spec(st_shape), _const_spec(lr.shape), _const_spec(li.shape),
                  _const_spec(bb.shape), _const_spec(cc.shape), _const_spec(dvec.shape), _const_spec(gw.shape),
                  _const_spec(gb.shape)],
        out_specs=[tok, pl.BlockSpec(st_shape, lambda i: (0, 0, 0)), pl.BlockSpec(st_shape, lambda i: (0, 0, 0))],
        out_shape=[jax.ShapeDtypeStruct((length, bsz, S5_WIDTH), F32),
                   jax.ShapeDtypeStruct(st_shape, F32), jax.ShapeDtypeStruct(st_shape, F32)],
        scratch_shapes=[pltpu.VMEM((tt * bsz, S5_BLOCKS * 2 * S5_BLOCK_STATE), F32),
                        pltpu.VMEM(st_shape, F32), pltpu.VMEM(st_shape, F32)],
        compiler_params=_params(1),
        name="s5",
    )(u3, h0re, h0im, lr, li, bb, cc, dvec, gw, gb)


def _ffn_kernel(x_ref, ys_ref, y5_ref, hist_ref, wout_ref, n1_ref, n2_ref, wup_ref, cw_ref, cb_ref, wdn_ref, n3_ref,
                y_ref, carry_out_ref, extg_scr, extv_scr, carry_scr, act_scr, *, tm):
    t = pl.program_id(1)

    @pl.when(t == 0)
    def _():
        carry_scr[...] = hist_ref[0]

    mix = (jnp.dot(ys_ref[0], wout_ref[0:SSD_WIDTH, :], preferred_element_type=F32)
           + jnp.dot(y5_ref[0], wout_ref[SSD_WIDTH:, :], preferred_element_type=F32))
    h = x_ref[0] + _rms(mix, n1_ref[...])
    hn = _rms(h, n2_ref[...]).astype(BF16)

    def conv_chunk(ext_scr, c0):
        cols = slice(c0, c0 + FFN_CHUNK)
        up = jnp.dot(hn, wup_ref[:, cols], preferred_element_type=F32)
        ext_scr[0:HIST_ROWS, :] = carry_scr[:, cols]
        ext_scr[HIST_ROWS:HIST_ROWS + tm, :] = up
        carry_scr[:, cols] = ext_scr[tm:tm + HIST_ROWS, :]
        out = cb_ref[:, cols] + up * cw_ref[FFN_CONV - 1:FFN_CONV, cols]
        for k in range(1, FFN_CONV):
            out = out + ext_scr[HIST_ROWS - k:HIST_ROWS - k + tm, :] * cw_ref[FFN_CONV - 1 - k:FFN_CONV - k, cols]
        return out

    for c in range(D_FF // FFN_CHUNK):
        gate = conv_chunk(extg_scr, c * FFN_CHUNK)
        val = conv_chunk(extv_scr, D_FF + c * FFN_CHUNK)
        act_scr[:, c * FFN_CHUNK:(c + 1) * FFN_CHUNK] = (_gelu_tanh(gate) * val).astype(BF16)

    ffn = jnp.dot(act_scr[...], wdn_ref[...], preferred_element_type=F32)
    y_ref[0] = h + _rms(ffn, n3_ref[...])

    @pl.when(t == pl.num_programs(1) - 1)
    def _():
        carry_out_ref[0] = carry_scr[...]


def _ffn(x, ys, y5, hist, wout, n1, n2, wup, cw, cb, wdn, n3, tm):
    bsz, length, _ = x.shape
    grid = (bsz, length // tm)
    tok = lambda w: pl.BlockSpec((1, tm, w), lambda b, i: (b, i, 0))
    per_b = pl.BlockSpec((1, HIST_ROWS, 2 * D_FF), lambda b, i: (b, 0, 0))
    return pl.pallas_call(
        functools.partial(_ffn_kernel, tm=tm),
        grid=grid,
        in_specs=[tok(D_MODEL), tok(SSD_WIDTH), tok(S5_WIDTH), per_b,
                  _const_spec(wout.shape), _const_spec(n1.shape), _const_spec(n2.shape), _const_spec(wup.shape),
                  _const_spec(cw.shape), _const_spec(cb.shape), _const_spec(wdn.shape), _const_spec(n3.shape)],
        out_specs=[tok(D_MODEL), per_b],
        out_shape=[jax.ShapeDtypeStruct((bsz, length, D_MODEL), F32),
                   jax.ShapeDtypeStruct((bsz, HIST_ROWS, 2 * D_FF), F32)],
        scratch_shapes=[pltpu.VMEM((tm + HIST_ROWS, FFN_CHUNK), F32), pltpu.VMEM((tm + HIST_ROWS, FFN_CHUNK), F32),
                        pltpu.VMEM((HIST_ROWS, 2 * D_FF), F32), pltpu.VMEM((tm, D_FF), BF16)],
        compiler_params=_params(2),
        name="outffn",
    )(x, ys, y5, hist, wout, n1, n2, wup, cw, cb, wdn, n3)


def _block_diag(w):
    _, r, c = w.shape
    w4 = w.reshape(S5_BLOCKS, S5_GPB, r, c)
    eye = jnp.eye(S5_GPB, dtype=w.dtype)
    return jnp.einsum("kgrc,gh->kgrhc", w4, eye).reshape(S5_BLOCKS, S5_GPB * r, S5_GPB * c)


def _prep_weights(pre_mix_norm_w, w_in, ssd_conv_w, ssd_conv_b, ssd_dt_bias, ssd_a_log, ssd_d, ssd_norm_w,
                  s5_lambda_re, s5_lambda_im, s5_log_dt, s5_b_re, s5_b_im, s5_c_re, s5_c_im, s5_d,
                  s5_glu_w, s5_glu_b, w_out, post_mix_norm_w, pre_ffn_norm_w, w_up, ffn_conv_w, ffn_conv_b,
                  w_down, post_ffn_norm_w):
    o_dt = SSD_WIDTH + SSD_CONV_DIM
    o_u = o_dt + SSD_HEADS
    pad_heads = lambda v: jnp.pad(v, (0, LANES - SSD_HEADS)).reshape(1, LANES)
    w = dict(
        nw0=pre_mix_norm_w.reshape(1, D_MODEL),
        wa=w_in[:, :o_dt].astype(BF16),
        wdt=jnp.pad(w_in[:, o_dt:o_u], ((0, 0), (0, LANES - SSD_HEADS))).astype(BF16),
        wu=w_in[:, o_u:].astype(BF16),
        cw=ssd_conv_w, cb=ssd_conv_b.reshape(1, SSD_CONV_DIM),
        dtb=pad_heads(ssd_dt_bias), alog=pad_heads(ssd_a_log),
        dssd=jnp.repeat(ssd_d, SSD_HEAD_DIM).reshape(1, SSD_WIDTH),
        nssd=ssd_norm_w.reshape(1, SSD_WIDTH),
        wout=w_out.astype(BF16), n1=post_mix_norm_w.reshape(1, D_MODEL), n2=pre_ffn_norm_w.reshape(1, D_MODEL),
        wup=w_up.astype(BF16), fcw=ffn_conv_w, fcb=ffn_conv_b.reshape(1, 2 * D_FF),
        wdn=w_down.astype(BF16), n3=post_ffn_norm_w.reshape(1, D_MODEL),
    )
    dt = jnp.exp(s5_log_dt)[:, None]
    mag = jnp.exp(s5_lambda_re * dt)
    ang = s5_lambda_im * dt
    lb_re = mag * jnp.cos(ang)
    lb_im = mag * jnp.sin(ang)
    den = s5_lambda_re * s5_lambda_re + s5_lambda_im * s5_lambda_im
    q_re = ((lb_re - 1) * s5_lambda_re + lb_im * s5_lambda_im) / den
    q_im = (lb_im * s5_lambda_re - (lb_re - 1) * s5_lambda_im) / den
    bb_re = q_re[..., None] * s5_b_re - q_im[..., None] * s5_b_im
    bb_im = q_re[..., None] * s5_b_im + q_im[..., None] * s5_b_re
    to_in = lambda m: _block_diag(jnp.swapaxes(m, 1, 2))
    w["bb"] = jnp.concatenate([to_in(bb_re), to_in(bb_im)], axis=-1).astype(BF16)
    to_out = lambda m: _block_diag(jnp.swapaxes(m, 1, 2))
    w["cc"] = jnp.concatenate([to_out(s5_c_re), to_out(-s5_c_im)], axis=1).astype(BF16)
    w["lr"] = lb_re.reshape(S5_BLOCKS, 1, S5_BLOCK_STATE)
    w["li"] = lb_im.reshape(S5_BLOCKS, 1, S5_BLOCK_STATE)
    w["d5"] = s5_d.reshape(1, S5_WIDTH)
    w["gw"] = jnp.concatenate([_block_diag(s5_glu_w[..., :S5_GROUP_CH]), _block_diag(s5_glu_w[..., S5_GROUP_CH:])],
                              axis=-1).astype(BF16)
    gb = lambda v: v.reshape(S5_BLOCKS, 1, LANES)
    w["gb"] = jnp.concatenate([gb(s5_glu_b[:, :S5_GROUP_CH]), gb(s5_glu_b[:, S5_GROUP_CH:])], axis=-1)
    return w


def _hist_tile(hist):
    return jnp.pad(hist, ((0, 0), (HIST_ROWS - hist.shape[1], 0), (0, 0)))


def _layer(x, conv_hist, ssd_h0, s5_re, s5_im, ffn_hist, w, *, tm, q, tt):
    bsz, length, _ = x.shape
    z, xbc, dt, u = _inproj(x, w["nw0"], w["wa"], w["wdt"], w["wu"], tm)
    new_conv = xbc[:, length - (SSD_CONV - 1):, :].astype(F32)

    lpad = -length % q
    if lpad:
        assert length < q, "a padded sequence must fit one SSD chunk"
        padt = lambda a: jnp.pad(a, ((0, 0), (0, lpad), (0, 0)))
        z, xbc, dt = padt(z), padt(xbc), padt(dt)
    h0 = jnp.transpose(ssd_h0, (0, 3, 1, 2)).reshape(bsz, SSD_STATE, SSD_WIDTH)
    y_ssd, st = _ssd(z, xbc, dt, _hist_tile(conv_hist), h0, w["cw"], w["cb"], w["dtb"], w["alog"], w["dssd"],
                     w["nssd"], q, length if lpad else q)
    y_ssd = y_ssd[:, :length]
    new_ssd = jnp.transpose(st.reshape(bsz, SSD_STATE, SSD_HEADS, SSD_HEAD_DIM), (0, 2, 3, 1))

    to_blocks = lambda s: jnp.transpose(s.reshape(bsz, S5_BLOCKS, S5_BLOCK_STATE), (1, 0, 2))
    from_blocks = lambda s: jnp.transpose(s, (1, 0, 2)).reshape(bsz, S5_GROUPS, S5_STATE)
    u3 = jnp.transpose(u, (1, 0, 2)).astype(F32)
    y5, hre, him = _s5(u3, to_blocks(s5_re), to_blocks(s5_im), w["lr"], w["li"], w["bb"], w["cc"], w["d5"],
                       w["gw"], w["gb"], tt)
    y_s5 = jnp.transpose(y5, (1, 0, 2)).astype(BF16)

    y, carry = _ffn(x, y_ssd, y_s5, _hist_tile(ffn_hist), w["wout"], w["n1"], w["n2"], w["wup"], w["fcw"], w["fcb"],
                    w["wdn"], w["n3"], tm)
    new_ffn = carry[:, HIST_ROWS - (FFN_CONV - 1):, :]
    return y, new_conv, new_ssd, from_blocks(hre), from_blocks(him), new_ffn


def _tiles(length):
    tm = min(length, 512)
    q = 128
    tt = min(length, 64)
    assert length % tm == 0 and length % tt == 0 and length >= SSD_CONV - 1
    return dict(tm=tm, q=q, tt=tt)


def kernel(x_prompt, x_sample, cache_ssd_conv, state_ssd, state_s5_re, state_s5_im, cache_ffn_conv, pre_mix_norm_w, w_in, ssd_conv_w, ssd_conv_b, ssd_dt_bias, ssd_a_log, ssd_d, ssd_norm_w, s5_lambda_re, s5_lambda_im, s5_log_dt, s5_b_re, s5_b_im, s5_c_re, s5_c_im, s5_d, s5_glu_w, s5_glu_b, w_out, post_mix_norm_w, pre_ffn_norm_w, w_up, ffn_conv_w, ffn_conv_b, w_down, post_ffn_norm_w):
    depth = w_in.shape[0]
    bsz = x_prompt.shape[0]
    dtp = x_prompt.dtype
    layer_params = (pre_mix_norm_w, w_in, ssd_conv_w, ssd_conv_b, ssd_dt_bias, ssd_a_log, ssd_d, ssd_norm_w,
                    s5_lambda_re, s5_lambda_im, s5_log_dt, s5_b_re, s5_b_im, s5_c_re, s5_c_im, s5_d,
                    s5_glu_w, s5_glu_b, w_out, post_mix_norm_w, pre_ffn_norm_w, w_up, ffn_conv_w, ffn_conv_b,
                    w_down, post_ffn_norm_w)
    y_prompt, y_sample = x_prompt, x_sample
    prompt_states, sample_states = [], []
    for l in range(depth):
        w = _prep_weights(*(p[l] for p in layer_params))
        y_prompt, *ps = _layer(
            y_prompt,
            jnp.zeros((bsz, SSD_CONV - 1, SSD_CONV_DIM), dtp),
            jnp.zeros((bsz, SSD_HEADS, SSD_HEAD_DIM, SSD_STATE), dtp),
            jnp.zeros((bsz, S5_GROUPS, S5_STATE), dtp),
            jnp.zeros((bsz, S5_GROUPS, S5_STATE), dtp),
            jnp.zeros((bsz, FFN_CONV - 1, 2 * D_FF), dtp),
            w, **_tiles(y_prompt.shape[1]))
        y_sample, *ss = _layer(
            y_sample, cache_ssd_conv[l], state_ssd[l], state_s5_re[l], state_s5_im[l], cache_ffn_conv[l],
            w, **_tiles(y_sample.shape[1]))
        prompt_states.append(ps)
        sample_states.append(ss)
    stack = lambda states: tuple(jnp.stack([s[k] for s in states]) for k in range(5))
    return (y_prompt, y_sample) + stack(prompt_states) + stack(sample_states)
```

```python
import functools
import math

import jax
import jax.numpy as jnp
from jax import lax
from jax.experimental import pallas as pl
from jax.experimental.pallas import tpu as pltpu

F32 = jnp.float32
BF16 = jnp.bfloat16

D_MODEL = 1024
SSD_WIDTH = 512
SSD_HEAD_DIM = 64
SSD_HEADS = 8
SSD_GROUPS = 2
SSD_HPG = SSD_HEADS // SSD_GROUPS
SSD_GROUP_W = SSD_WIDTH // SSD_GROUPS
SSD_STATE = 128
SSD_CONV = 4
SSD_CONV_DIM = SSD_WIDTH + 2 * SSD_GROUPS * SSD_STATE
S5_WIDTH = 512
S5_GROUP_CH = 16
S5_GROUPS = 32
S5_STATE = 64
S5_BLOCKS = 4
S5_GPB = S5_GROUPS // S5_BLOCKS
S5_BLOCK_STATE = S5_GPB * S5_STATE
D_FF = 2816
FFN_CONV = 3
EPS = 1e-6

LANES = 128
SUBLANES = 8
HIST_ROWS = SUBLANES
VMEM_LIMIT = 56 * 1024 * 1024

FFN_CHUNK = 256


def _rms(x, w):
    ms = jnp.mean(x * x, axis=-1, keepdims=True)
    return x * lax.rsqrt(ms + EPS) * w


def _gelu_tanh(x):
    c = math.sqrt(2.0 / math.pi)
    return 0.5 * x * (1.0 + jnp.tanh(c * (x + 0.044715 * (x * x * x))))


def _softplus(x):
    return jnp.maximum(x, 0.0) + jnp.log1p(jnp.exp(-jnp.abs(x)))


def _causal_conv(ext, cur, w_ref, b_ref, c0, rows, taps):
    cols = slice(c0, c0 + LANES)
    ext[HIST_ROWS:HIST_ROWS + rows, :] = cur
    out = b_ref[:, cols] + cur * w_ref[taps - 1:taps, cols]
    for k in range(1, taps):
        out = out + ext[pl.ds(HIST_ROWS - k, rows), :] * w_ref[taps - 1 - k:taps - k, cols]
    ext[0:HIST_ROWS, :] = ext[rows:rows + HIST_ROWS, :]
    return out


def _const_spec(shape):
    nd = len(shape)
    return pl.BlockSpec(shape, lambda *_: (0,) * nd, pipeline_mode=pl.Buffered(1))


def _params(n_axes):
    return pltpu.CompilerParams(dimension_semantics=("arbitrary",) * n_axes, vmem_limit_bytes=VMEM_LIMIT)


def _inproj_kernel(x_ref, nw_ref, wa_ref, wdt_ref, wu_ref, z_ref, xbc_ref, dt_ref, u_ref, *, tm, bsz):
    b = pl.program_id(1)
    x = x_ref[0]
    xn = _rms(x, nw_ref[...]).astype(BF16)
    a = jnp.dot(xn, wa_ref[...], preferred_element_type=F32)
    z_ref[0] = a[:, :SSD_WIDTH].astype(BF16)
    xbc_ref[0] = a[:, SSD_WIDTH:].astype(BF16)
    dt_ref[0] = jnp.dot(xn, wdt_ref[...], preferred_element_type=F32)
    u = jnp.dot(xn, wu_ref[...], preferred_element_type=F32)
    for j in range(S5_BLOCKS):
        u_ref[j, pl.ds(b, tm, stride=bsz), :] = u[:, j * LANES:(j + 1) * LANES]


def _tb_spec(tm, bsz):
    return pl.BlockSpec((S5_BLOCKS, tm * bsz, LANES), lambda i, b: (0, i, 0))


def _inproj(x, nw, wa, wdt, wu, tm):
    bsz, length, _ = x.shape
    grid = (length // tm, bsz)
    tok = lambda w, dt: (jax.ShapeDtypeStruct((bsz, length, w), dt), pl.BlockSpec((1, tm, w), lambda i, b: (b, i, 0)))
    outs = [tok(SSD_WIDTH, BF16), tok(SSD_CONV_DIM, BF16), tok(LANES, F32),
            (jax.ShapeDtypeStruct((S5_BLOCKS, length * bsz, LANES), F32), _tb_spec(tm, bsz))]
    return pl.pallas_call(
        functools.partial(_inproj_kernel, tm=tm, bsz=bsz),
        grid=grid,
        in_specs=[pl.BlockSpec((1, tm, D_MODEL), lambda i, b: (b, i, 0)),
                  _const_spec(nw.shape), _const_spec(wa.shape), _const_spec(wdt.shape), _const_spec(wu.shape)],
        out_specs=[o[1] for o in outs],
        out_shape=[o[0] for o in outs],
        compiler_params=_params(2),
        name="inproj",
    )(x, nw, wa, wdt, wu)


def _split3(v):
    hi = v.astype(BF16)
    r1 = v - hi.astype(F32)
    mid = r1.astype(BF16)
    lo = (r1 - mid.astype(F32)).astype(BF16)
    return hi, mid, lo


def _head_cols(a):
    q = a.shape[0]
    lane = lax.broadcasted_iota(jnp.int32, (q, LANES), 1)
    tiles = []
    for j in range(SSD_HEADS * SSD_HEAD_DIM // LANES):
        even = jnp.broadcast_to(a[:, 2 * j:2 * j + 1], (q, LANES))
        odd = jnp.broadcast_to(a[:, 2 * j + 1:2 * j + 2], (q, LANES))
        tiles.append(jnp.where(lane < SSD_HEAD_DIM, even, odd))
    return jnp.concatenate(tiles, axis=-1)


def _ssd_kernel(z_ref, xbc_ref, dt_ref, hist_ref, h0_ref, cw_ref, cb_ref, dtb_ref, alog_ref, d_ref, nw_ref,
                y_ref, st_out_ref, ext_scr, st_scr, *, q, valid):
    t = pl.program_id(1)

    @pl.when(t == 0)
    def _():
        for j in range(SSD_CONV_DIM // LANES):
            ext_scr[j, 0:HIST_ROWS, :] = hist_ref[0, :, j * LANES:(j + 1) * LANES]
        st_scr[...] = h0_ref[0]

    xraw = xbc_ref[0].astype(F32)
    conv = jnp.concatenate([_causal_conv(ext_scr.at[j], xraw[:, j * LANES:(j + 1) * LANES], cw_ref, cb_ref,
                                         j * LANES, q, SSD_CONV) for j in range(SSD_CONV_DIM // LANES)], axis=-1)
    xact = conv * jax.nn.sigmoid(conv)
    xs = xact[:, :SSD_WIDTH]
    xs_bf = xs.astype(BF16)

    dt = _softplus(dt_ref[0] + dtb_ref[...])
    row = lax.broadcasted_iota(jnp.int32, (q, q), 0)
    col = lax.broadcasted_iota(jnp.int32, (q, q), 1)
    causal = row >= col
    if valid < q:
        dt = jnp.where(lax.broadcasted_iota(jnp.int32, dt.shape, 0) < valid, dt, 0.0)
    a = -jnp.exp(alog_ref[...])
    tril = jnp.where(causal, 1.0, 0.0).astype(BF16)
    cs = sum(jnp.dot(tril, part, preferred_element_type=F32) for part in _split3(dt * a))
    cs_t = cs.T
    dt_t = dt.T
    cs_w = _head_cols(cs)
    dt_w = _head_cols(dt)
    cs_last_w = cs_w[q - 1:q, :]
    decay_in = jnp.exp(cs_w)
    w_end = jnp.exp(cs_last_w - cs_w) * dt_w
    chunk_decay = jnp.exp(cs_last_w)

    lane = lax.broadcasted_iota(jnp.int32, (1, SSD_GROUP_W), 1)
    ys = []
    for g in range(SSD_GROUPS):
        b_g = xact[:, SSD_WIDTH + g * SSD_STATE:SSD_WIDTH + (g + 1) * SSD_STATE].astype(BF16)
        c_off = SSD_WIDTH + SSD_GROUPS * SSD_STATE
        c_g = xact[:, c_off + g * SSD_STATE:c_off + (g + 1) * SSD_STATE].astype(BF16)
        gs = slice(g * SSD_GROUP_W, (g + 1) * SSD_GROUP_W)
        x_g = xs_bf[:, gs]
        scores = lax.dot_general(c_g, b_g, (((1,), (1,)), ((), ())), preferred_element_type=F32)
        st_g = st_scr[:, gs]
        y_g = jnp.dot(c_g, st_g.astype(BF16), preferred_element_type=F32) * decay_in[:, gs]
        for j in range(SSD_HPG):
            h = g * SSD_HPG + j
            seg = cs[:, h:h + 1] - cs_t[h:h + 1, :]
            lmat = jnp.where(causal, jnp.exp(seg), 0.0) * dt_t[h:h + 1, :]
            m = (scores * lmat).astype(BF16)
            in_head = (lane >= j * SSD_HEAD_DIM) & (lane < (j + 1) * SSD_HEAD_DIM)
            x_h = jnp.where(in_head, x_g, jnp.zeros_like(x_g))
            y_g = y_g + jnp.dot(m, x_h, preferred_element_type=F32)
        ys.append(y_g)
        wx = (xs[:, gs] * w_end[:, gs]).astype(BF16)
        new = lax.dot_general(b_g, wx, (((0,), (0,)), ((), ())), preferred_element_type=F32)
        st_scr[:, gs] = st_g * chunk_decay[:, gs] + new
    y = jnp.concatenate(ys, axis=-1) + d_ref[...] * xs

    zf = z_ref[0].astype(F32)
    gz = y * (zf * jax.nn.sigmoid(zf))
    nw = nw_ref[...]
    outs = []
    for g in range(SSD_GROUPS):
        gs = slice(g * SSD_GROUP_W, (g + 1) * SSD_GROUP_W)
        outs.append(_rms(gz[:, gs], nw[:, gs]))
    y_ref[0] = jnp.concatenate(outs, axis=-1).astype(BF16)

    @pl.when(t == pl.num_programs(1) - 1)
    def _():
        st_out_ref[0] = st_scr[...]


def _ssd(z, xbc, dt, hist, h0, cw, cb, dtb, alog, dvec, nw, q, valid):
    bsz, length, _ = z.shape
    grid = (bsz, length // q)
    tok = lambda w: pl.BlockSpec((1, q, w), lambda b, i: (b, i, 0))
    per_b = lambda r, w: pl.BlockSpec((1, r, w), lambda b, i: (b, 0, 0))
    return pl.pallas_call(
        functools.partial(_ssd_kernel, q=q, valid=valid),
        grid=grid,
        in_specs=[tok(SSD_WIDTH), tok(SSD_CONV_DIM), tok(LANES), per_b(HIST_ROWS, SSD_CONV_DIM),
                  per_b(SSD_STATE, SSD_WIDTH), _const_spec(cw.shape), _const_spec(cb.shape), _const_spec(dtb.shape),
                  _const_spec(alog.shape), _const_spec(dvec.shape), _const_spec(nw.shape)],
        out_specs=[tok(SSD_WIDTH), per_b(SSD_STATE, SSD_WIDTH)],
        out_shape=[jax.ShapeDtypeStruct((bsz, length, SSD_WIDTH), BF16),
                   jax.ShapeDtypeStruct((bsz, SSD_STATE, SSD_WIDTH), F32)],
        scratch_shapes=[pltpu.VMEM((SSD_CONV_DIM // LANES, q + HIST_ROWS, LANES), F32),
                        pltpu.VMEM((SSD_STATE, SSD_WIDTH), F32)],
        compiler_params=_params(2),
        name="ssd",
    )(z, xbc, dt, hist, h0, cw, cb, dtb, alog, dvec, nw)


def _s5_kernel(u_ref, h0re_ref, h0im_ref, lr_ref, li_ref, bb_ref, cc_ref, d_ref, gw_ref, gb_ref,
               y_ref, hre_out_ref, him_out_ref, hs_scr, hre_scr, him_scr, *, tt, bsz):
    i = pl.program_id(0)

    @pl.when(i == 0)
    def _():
        hre_scr[...] = h0re_ref[...]
        him_scr[...] = h0im_ref[...]

    half = S5_BLOCK_STATE

    blocks = range(S5_BLOCKS)
    for kb in blocks:
        hs_scr[kb] = jnp.dot(u_ref[kb].astype(BF16), bb_ref[kb], preferred_element_type=F32)

    for kb in blocks:
        lr = jnp.broadcast_to(lr_ref[kb], (bsz, half))
        li = jnp.broadcast_to(li_ref[kb], (bsz, half))

        def body(t, carry, kb=kb, lr=lr, li=li):
            hr, hi = carry
            r0 = pl.multiple_of(t * bsz, bsz)
            nr = lr * hr - li * hi + hs_scr[kb, pl.ds(r0, bsz), 0:half]
            ni = lr * hi + li * hr + hs_scr[kb, pl.ds(r0, bsz), half:2 * half]
            hs_scr[kb, pl.ds(r0, bsz), 0:half] = nr
            hs_scr[kb, pl.ds(r0, bsz), half:2 * half] = ni
            return nr, ni

        hr, hi = lax.fori_loop(0, tt, body, (hre_scr[kb], him_scr[kb]), unroll=8)
        hre_scr[kb] = hr
        him_scr[kb] = hi

    ys = [jnp.dot(hs_scr[kb].astype(BF16), cc_ref[kb], preferred_element_type=F32)
          + d_ref[:, kb * LANES:(kb + 1) * LANES] * u_ref[kb] for kb in blocks]
    ges = [_gelu_tanh(y).astype(BF16) for y in ys]
    gls = [jnp.dot(ges[kb], gw_ref[kb], preferred_element_type=F32) + gb_ref[kb] for kb in blocks]
    for kb in blocks:
        y_ref[kb] = gls[kb][:, :LANES] * jax.nn.sigmoid(gls[kb][:, LANES:])

    @pl.when(i == pl.num_programs(0) - 1)
    def _():
        hre_out_ref[...] = hre_scr[...]
        him_out_ref[...] = him_scr[...]


def _s5(u_tb, h0re, h0im, lr, li, bb, cc, dvec, gw, gb, tt):
    bsz = h0re.shape[1]
    length = u_tb.shape[1] // bsz
    grid = (length // tt,)
    st_shape = (S5_BLOCKS, bsz, S5_BLOCK_STATE)
    tok = pl.BlockSpec((S5_BLOCKS, tt * bsz, LANES), lambda i: (0, i, 0))
    return pl.pallas_call(
        functools.partial(_s5_kernel, tt=tt, bsz=bsz),
        grid=grid,
        in_specs=[tok, _const_spec(st_shape), _const_spec(st_shape), _const_spec(lr.shape), _const_spec(li.shape),
                  _const_spec(bb.shape), _const_spec(cc.shape), _const_spec(dvec.shape), _const_spec(gw.shape),
                  _const_spec(gb.shape)],
        out_specs=[tok, pl.BlockSpec(st_shape, lambda i: (0, 0, 0)), pl.BlockSpec(st_shape, lambda i: (0, 0, 0))],
        out_shape=[jax.ShapeDtypeStruct(u_tb.shape, F32),
                   jax.ShapeDtypeStruct(st_shape, F32), jax.ShapeDtypeStruct(st_shape, F32)],
        scratch_shapes=[pltpu.VMEM((S5_BLOCKS, tt * bsz, 2 * S5_BLOCK_STATE), F32),
                        pltpu.VMEM(st_shape, F32), pltpu.VMEM(st_shape, F32)],
        compiler_params=_params(1),
        name="s5",
    )(u_tb, h0re, h0im, lr, li, bb, cc, dvec, gw, gb)


def _ffn_kernel(x_ref, ys_ref, y5_ref, hist_ref, wout_ref, n1_ref, n2_ref, wup_ref, cw_ref, cb_ref, wdn_ref, n3_ref,
                y_ref, carry_ref, ext_scr, act_scr, *, tm, bsz):
    i = pl.program_id(0)
    b = pl.program_id(1)

    @pl.when(i == 0)
    def _():
        carry_ref[b] = hist_ref[b]

    y5 = jnp.concatenate([y5_ref[j, pl.ds(b, tm, stride=bsz), :] for j in range(S5_BLOCKS)], axis=-1)
    mix = (jnp.dot(ys_ref[0], wout_ref[0:SSD_WIDTH, :], preferred_element_type=F32)
           + jnp.dot(y5.astype(BF16), wout_ref[SSD_WIDTH:, :], preferred_element_type=F32))
    h = x_ref[0] + _rms(mix, n1_ref[...])
    hn = _rms(h, n2_ref[...]).astype(BF16)

    def conv_slab(e, s, cur):
        ext = ext_scr.at[e]
        ext[0:HIST_ROWS, :] = carry_ref[b, s]
        out = _causal_conv(ext, cur, cw_ref, cb_ref, s * LANES, tm, FFN_CONV)
        carry_ref[b, s] = ext[0:HIST_ROWS, :]
        return out

    spc = FFN_CHUNK // LANES
    for c in range(D_FF // FFN_CHUNK):
        up_g = jnp.dot(hn, wup_ref[:, c * FFN_CHUNK:(c + 1) * FFN_CHUNK], preferred_element_type=F32)
        up_v = jnp.dot(hn, wup_ref[:, D_FF + c * FFN_CHUNK:D_FF + (c + 1) * FFN_CHUNK], preferred_element_type=F32)
        e0 = (c % 2) * 2 * spc
        for k in range(spc):
            s = c * spc + k
            lanes = slice(k * LANES, (k + 1) * LANES)
            gate = conv_slab(e0 + k, s, up_g[:, lanes])
            val = conv_slab(e0 + spc + k, D_FF // LANES + s, up_v[:, lanes])
            act_scr[:, s * LANES:(s + 1) * LANES] = (_gelu_tanh(gate) * val).astype(BF16)

    ffn = jnp.dot(act_scr[...], wdn_ref[...], preferred_element_type=F32)
    y_ref[0] = h + _rms(ffn, n3_ref[...])


def _ffn(x, ys, y5_tb, hist, wout, n1, n2, wup, cw, cb, wdn, n3, tm):
    bsz, length, _ = x.shape
    grid = (length // tm, bsz)
    tok = lambda w: pl.BlockSpec((1, tm, w), lambda i, b: (b, i, 0))
    n_ext = 4 * FFN_CHUNK // LANES
    return pl.pallas_call(
        functools.partial(_ffn_kernel, tm=tm, bsz=bsz),
        grid=grid,
        in_specs=[tok(D_MODEL), tok(SSD_WIDTH), _tb_spec(tm, bsz), _const_spec(hist.shape),
                  _const_spec(wout.shape), _const_spec(n1.shape), _const_spec(n2.shape), _const_spec(wup.shape),
                  _const_spec(cw.shape), _const_spec(cb.shape), _const_spec(wdn.shape), _const_spec(n3.shape)],
        out_specs=[tok(D_MODEL), pl.BlockSpec(hist.shape, lambda i, b: (0, 0, 0, 0))],
        out_shape=[jax.ShapeDtypeStruct((bsz, length, D_MODEL), F32), jax.ShapeDtypeStruct(hist.shape, F32)],
        scratch_shapes=[pltpu.VMEM((n_ext, tm + HIST_ROWS, LANES), F32), pltpu.VMEM((tm, D_FF), BF16)],
        compiler_params=_params(2),
        name="outffn",
    )(x, ys, y5_tb, hist, wout, n1, n2, wup, cw, cb, wdn, n3)


def _block_diag(w):
    _, r, c = w.shape
    w4 = w.reshape(S5_BLOCKS, S5_GPB, r, c)
    eye = jnp.eye(S5_GPB, dtype=w.dtype)
    return jnp.einsum("kgrc,gh->kgrhc", w4, eye).reshape(S5_BLOCKS, S5_GPB * r, S5_GPB * c)


def _prep_weights(pre_mix_norm_w, w_in, ssd_conv_w, ssd_conv_b, ssd_dt_bias, ssd_a_log, ssd_d, ssd_norm_w,
                  s5_lambda_re, s5_lambda_im, s5_log_dt, s5_b_re, s5_b_im, s5_c_re, s5_c_im, s5_d,
                  s5_glu_w, s5_glu_b, w_out, post_mix_norm_w, pre_ffn_norm_w, w_up, ffn_conv_w, ffn_conv_b,
                  w_down, post_ffn_norm_w):
    o_dt = SSD_WIDTH + SSD_CONV_DIM
    o_u = o_dt + SSD_HEADS
    pad_heads = lambda v: jnp.pad(v, (0, LANES - SSD_HEADS)).reshape(1, LANES)
    w = dict(
        nw0=pre_mix_norm_w.reshape(1, D_MODEL),
        wa=w_in[:, :o_dt].astype(BF16),
        wdt=jnp.pad(w_in[:, o_dt:o_u], ((0, 0), (0, LANES - SSD_HEADS))).astype(BF16),
        wu=w_in[:, o_u:].astype(BF16),
        cw=ssd_conv_w, cb=ssd_conv_b.reshape(1, SSD_CONV_DIM),
        dtb=pad_heads(ssd_dt_bias), alog=pad_heads(ssd_a_log),
        dssd=jnp.repeat(ssd_d, SSD_HEAD_DIM).reshape(1, SSD_WIDTH),
        nssd=ssd_norm_w.reshape(1, SSD_WIDTH),
        wout=w_out.astype(BF16), n1=post_mix_norm_w.reshape(1, D_MODEL), n2=pre_ffn_norm_w.reshape(1, D_MODEL),
        wup=w_up.astype(BF16), fcw=ffn_conv_w, fcb=ffn_conv_b.reshape(1, 2 * D_FF),
        wdn=w_down.astype(BF16), n3=post_ffn_norm_w.reshape(1, D_MODEL),
    )
    dt = jnp.exp(s5_log_dt)[:, None]
    mag = jnp.exp(s5_lambda_re * dt)
    ang = s5_lambda_im * dt
    lb_re = mag * jnp.cos(ang)
    lb_im = mag * jnp.sin(ang)
    den = s5_lambda_re * s5_lambda_re + s5_lambda_im * s5_lambda_im
    q_re = ((lb_re - 1) * s5_lambda_re + lb_im * s5_lambda_im) / den
    q_im = (lb_im * s5_lambda_re - (lb_re - 1) * s5_lambda_im) / den
    bb_re = q_re[..., None] * s5_b_re - q_im[..., None] * s5_b_im
    bb_im = q_re[..., None] * s5_b_im + q_im[..., None] * s5_b_re
    to_in = lambda m: _block_diag(jnp.swapaxes(m, 1, 2))
    w["bb"] = jnp.concatenate([to_in(bb_re), to_in(bb_im)], axis=-1).astype(BF16)
    to_out = lambda m: _block_diag(jnp.swapaxes(m, 1, 2))
    w["cc"] = jnp.concatenate([to_out(s5_c_re), to_out(-s5_c_im)], axis=1).astype(BF16)
    w["lr"] = lb_re.reshape(S5_BLOCKS, 1, S5_BLOCK_STATE)
    w["li"] = lb_im.reshape(S5_BLOCKS, 1, S5_BLOCK_STATE)
    w["d5"] = s5_d.reshape(1, S5_WIDTH)
    w["gw"] = jnp.concatenate([_block_diag(s5_glu_w[..., :S5_GROUP_CH]), _block_diag(s5_glu_w[..., S5_GROUP_CH:])],
                              axis=-1).astype(BF16)
    gb = lambda v: v.reshape(S5_BLOCKS, 1, LANES)
    w["gb"] = jnp.concatenate([gb(s5_glu_b[:, :S5_GROUP_CH]), gb(s5_glu_b[:, S5_GROUP_CH:])], axis=-1)
    return w


def _hist_tile(hist):
    return jnp.pad(hist, ((0, 0), (HIST_ROWS - hist.shape[1], 0), (0, 0)))


def _layer(x, conv_hist, ssd_h0, s5_re, s5_im, ffn_hist, w, *, tm, q, tt):
    bsz, length, _ = x.shape
    z, xbc, dt, u = _inproj(x, w["nw0"], w["wa"], w["wdt"], w["wu"], tm)
    new_conv = xbc[:, length - (SSD_CONV - 1):, :].astype(F32)

    lpad = -length % q
    if lpad:
        assert length < q, "a padded sequence must fit one SSD chunk"
        padt = lambda a: jnp.pad(a, ((0, 0), (0, lpad), (0, 0)))
        z, xbc, dt = padt(z), padt(xbc), padt(dt)
    h0 = jnp.transpose(ssd_h0, (0, 3, 1, 2)).reshape(bsz, SSD_STATE, SSD_WIDTH)
    y_ssd, st = _ssd(z, xbc, dt, _hist_tile(conv_hist), h0, w["cw"], w["cb"], w["dtb"], w["alog"], w["dssd"],
                     w["nssd"], q, length if lpad else q)
    y_ssd = y_ssd[:, :length]
    new_ssd = jnp.transpose(st.reshape(bsz, SSD_STATE, SSD_HEADS, SSD_HEAD_DIM), (0, 2, 3, 1))

    to_blocks = lambda s: jnp.transpose(s.reshape(bsz, S5_BLOCKS, S5_BLOCK_STATE), (1, 0, 2))
    from_blocks = lambda s: jnp.transpose(s, (1, 0, 2)).reshape(bsz, S5_GROUPS, S5_STATE)
    y5, hre, him = _s5(u, to_blocks(s5_re), to_blocks(s5_im), w["lr"], w["li"], w["bb"], w["cc"], w["d5"],
                       w["gw"], w["gb"], tt)

    n_slab = 2 * D_FF // LANES
    hist = jnp.transpose(_hist_tile(ffn_hist).reshape(bsz, HIST_ROWS, n_slab, LANES), (0, 2, 1, 3))
    y, carry = _ffn(x, y_ssd, y5, hist, w["wout"], w["n1"], w["n2"], w["wup"], w["fcw"], w["fcb"],
                    w["wdn"], w["n3"], tm)
    new_ffn = jnp.transpose(carry[:, :, HIST_ROWS - (FFN_CONV - 1):, :], (0, 2, 1, 3)).reshape(
        bsz, FFN_CONV - 1, 2 * D_FF)
    return y, new_conv, new_ssd, from_blocks(hre), from_blocks(him), new_ffn


def _tiles(length):
    tm = min(length, 512)
    q = 128
    tt = min(length, 64)
    assert length % tm == 0 and length % tt == 0 and length >= SSD_CONV - 1
    return dict(tm=tm, q=q, tt=tt)


def kernel(x_prompt, x_sample, cache_ssd_conv, state_ssd, state_s5_re, state_s5_im, cache_ffn_conv, pre_mix_norm_w, w_in, ssd_conv_w, ssd_conv_b, ssd_dt_bias, ssd_a_log, ssd_d, ssd_norm_w, s5_lambda_re, s5_lambda_im, s5_log_dt, s5_b_re, s5_b_im, s5_c_re, s5_c_im, s5_d, s5_glu_w, s5_glu_b, w_out, post_mix_norm_w, pre_ffn_norm_w, w_up, ffn_conv_w, ffn_conv_b, w_down, post_ffn_norm_w):
    depth = w_in.shape[0]
    bsz = x_prompt.shape[0]
    dtp = x_prompt.dtype
    layer_params = (pre_mix_norm_w, w_in, ssd_conv_w, ssd_conv_b, ssd_dt_bias, ssd_a_log, ssd_d, ssd_norm_w,
                    s5_lambda_re, s5_lambda_im, s5_log_dt, s5_b_re, s5_b_im, s5_c_re, s5_c_im, s5_d,
                    s5_glu_w, s5_glu_b, w_out, post_mix_norm_w, pre_ffn_norm_w, w_up, ffn_conv_w, ffn_conv_b,
                    w_down, post_ffn_norm_w)
    y_prompt, y_sample = x_prompt, x_sample
    prompt_states, sample_states = [], []
    for l in range(depth):
        w = _prep_weights(*(p[l] for p in layer_params))
        y_prompt, *ps = _layer(
            y_prompt,
            jnp.zeros((bsz, SSD_CONV - 1, SSD_CONV_DIM), dtp),
            jnp.zeros((bsz, SSD_HEADS, SSD_HEAD_DIM, SSD_STATE), dtp),
            jnp.zeros((bsz, S5_GROUPS, S5_STATE), dtp),
            jnp.zeros((bsz, S5_GROUPS, S5_STATE), dtp),
            jnp.zeros((bsz, FFN_CONV - 1, 2 * D_FF), dtp),
            w, **_tiles(y_prompt.shape[1]))
        y_sample, *ss = _layer(
            y_sample, cache_ssd_conv[l], state_ssd[l], state_s5_re[l], state_s5_im[l], cache_ffn_conv[l],
            w, **_tiles(y_sample.shape[1]))
        prompt_states.append(ps)
        sample_states.append(ss)
    stack = lambda states: tuple(jnp.stack([s[k] for s in states]) for k in range(5))
    return (y_prompt, y_sample) + stack(prompt_states) + stack(sample_states)
```

```python
import functools
import math

import jax
import jax.numpy as jnp
from jax import lax
from jax.experimental import pallas as pl
from jax.experimental.pallas import tpu as pltpu

F32 = jnp.float32
BF16 = jnp.bfloat16

D_MODEL = 1024
SSD_WIDTH = 512
SSD_HEAD_DIM = 64
SSD_HEADS = 8
SSD_GROUPS = 2
SSD_HPG = SSD_HEADS // SSD_GROUPS
SSD_GROUP_W = SSD_WIDTH // SSD_GROUPS
SSD_STATE = 128
SSD_CONV = 4
SSD_CONV_DIM = SSD_WIDTH + 2 * SSD_GROUPS * SSD_STATE
S5_WIDTH = 512
S5_GROUP_CH = 16
S5_GROUPS = 32
S5_STATE = 64
S5_BLOCKS = 4
S5_GPB = S5_GROUPS // S5_BLOCKS
S5_BLOCK_STATE = S5_GPB * S5_STATE
D_FF = 2816
FFN_CONV = 3
EPS = 1e-6

LANES = 128
SUBLANES = 8
HIST_ROWS = SUBLANES
VMEM_LIMIT = 56 * 1024 * 1024

FFN_CHUNK = 256


def _rms(x, w):
    ms = jnp.mean(x * x, axis=-1, keepdims=True)
    return x * lax.rsqrt(ms + EPS) * w


def _gelu_tanh(x):
    c = math.sqrt(2.0 / math.pi)
    return 0.5 * x * (1.0 + jnp.tanh(c * (x + 0.044715 * (x * x * x))))


def _softplus(x):
    return jnp.maximum(x, 0.0) + jnp.log1p(jnp.exp(-jnp.abs(x)))


def _causal_conv(ext, cur, w_ref, b_ref, c0, rows, taps):
    cols = slice(c0, c0 + LANES)
    ext[HIST_ROWS:HIST_ROWS + rows, :] = cur
    out = b_ref[:, cols] + cur * w_ref[taps - 1:taps, cols]
    for k in range(1, taps):
        out = out + ext[pl.ds(HIST_ROWS - k, rows), :] * w_ref[taps - 1 - k:taps - k, cols]
    ext[0:HIST_ROWS, :] = ext[rows:rows + HIST_ROWS, :]
    return out


def _const_spec(shape):
    nd = len(shape)
    return pl.BlockSpec(shape, lambda *_: (0,) * nd, pipeline_mode=pl.Buffered(1))


def _params(n_axes):
    return pltpu.CompilerParams(dimension_semantics=("arbitrary",) * n_axes, vmem_limit_bytes=VMEM_LIMIT)


def _inproj_kernel(x_ref, nw_ref, wa_ref, wdt_ref, wu_ref, z_ref, xbc_ref, dt_ref, u_ref):
    x = x_ref[0]
    xn = _rms(x, nw_ref[...]).astype(BF16)
    a = jnp.dot(xn, wa_ref[...], preferred_element_type=F32)
    z_ref[0] = a[:, :SSD_WIDTH].astype(BF16)
    xbc_ref[0] = a[:, SSD_WIDTH:].astype(BF16)
    dt_ref[0] = jnp.dot(xn, wdt_ref[...], preferred_element_type=F32)
    u_ref[0] = jnp.dot(xn, wu_ref[...], preferred_element_type=F32).astype(BF16)


def _inproj(x, nw, wa, wdt, wu, tm):
    bsz, length, _ = x.shape
    grid = (bsz, length // tm)
    tok = lambda w, dt: (jax.ShapeDtypeStruct((bsz, length, w), dt), pl.BlockSpec((1, tm, w), lambda b, i: (b, i, 0)))
    outs = [tok(SSD_WIDTH, BF16), tok(SSD_CONV_DIM, BF16), tok(LANES, F32), tok(S5_WIDTH, BF16)]
    return pl.pallas_call(
        _inproj_kernel,
        grid=grid,
        in_specs=[pl.BlockSpec((1, tm, D_MODEL), lambda b, i: (b, i, 0)),
                  _const_spec(nw.shape), _const_spec(wa.shape), _const_spec(wdt.shape), _const_spec(wu.shape)],
        out_specs=[o[1] for o in outs],
        out_shape=[o[0] for o in outs],
        compiler_params=_params(2),
        name="inproj",
    )(x, nw, wa, wdt, wu)


def _split3(v):
    hi = v.astype(BF16)
    r1 = v - hi.astype(F32)
    mid = r1.astype(BF16)
    lo = (r1 - mid.astype(F32)).astype(BF16)
    return hi, mid, lo


def _head_cols(a):
    q = a.shape[0]
    lane = lax.broadcasted_iota(jnp.int32, (q, LANES), 1)
    tiles = []
    for j in range(SSD_HEADS * SSD_HEAD_DIM // LANES):
        even = jnp.broadcast_to(a[:, 2 * j:2 * j + 1], (q, LANES))
        odd = jnp.broadcast_to(a[:, 2 * j + 1:2 * j + 2], (q, LANES))
        tiles.append(jnp.where(lane < SSD_HEAD_DIM, even, odd))
    return jnp.concatenate(tiles, axis=-1)


def _ssd_kernel(z_ref, xbc_ref, dt_ref, hist_ref, h0_ref, cw_ref, cb_ref, dtb_ref, alog_ref, d_ref, nw_ref,
                y_ref, st_out_ref, ext_scr, st_scr, *, q, valid):
    t = pl.program_id(1)

    @pl.when(t == 0)
    def _():
        for j in range(SSD_CONV_DIM // LANES):
            ext_scr[j, 0:HIST_ROWS, :] = hist_ref[0, :, j * LANES:(j + 1) * LANES]
        st_scr[...] = h0_ref[0]

    xraw = xbc_ref[0].astype(F32)
    conv = jnp.concatenate([_causal_conv(ext_scr.at[j], xraw[:, j * LANES:(j + 1) * LANES], cw_ref, cb_ref,
                                         j * LANES, q, SSD_CONV) for j in range(SSD_CONV_DIM // LANES)], axis=-1)
    xact = conv * jax.nn.sigmoid(conv)
    xs = xact[:, :SSD_WIDTH]
    xs_bf = xs.astype(BF16)

    dt = _softplus(dt_ref[0] + dtb_ref[...])
    row = lax.broadcasted_iota(jnp.int32, (q, q), 0)
    col = lax.broadcasted_iota(jnp.int32, (q, q), 1)
    causal = row >= col
    if valid < q:
        dt = jnp.where(lax.broadcasted_iota(jnp.int32, dt.shape, 0) < valid, dt, 0.0)
    a = -jnp.exp(alog_ref[...])
    tril = jnp.where(causal, 1.0, 0.0).astype(BF16)
    cs = sum(jnp.dot(tril, part, preferred_element_type=F32) for part in _split3(dt * a))
    cs_t = cs.T
    dt_t = dt.T
    cs_w = _head_cols(cs)
    dt_w = _head_cols(dt)
    cs_last_w = cs_w[q - 1:q, :]
    decay_in = jnp.exp(cs_w)
    w_end = jnp.exp(cs_last_w - cs_w) * dt_w
    chunk_decay = jnp.exp(cs_last_w)

    lane = lax.broadcasted_iota(jnp.int32, (1, SSD_GROUP_W), 1)
    ys = []
    for g in range(SSD_GROUPS):
        b_g = xact[:, SSD_WIDTH + g * SSD_STATE:SSD_WIDTH + (g + 1) * SSD_STATE].astype(BF16)
        c_off = SSD_WIDTH + SSD_GROUPS * SSD_STATE
        c_g = xact[:, c_off + g * SSD_STATE:c_off + (g + 1) * SSD_STATE].astype(BF16)
        gs = slice(g * SSD_GROUP_W, (g + 1) * SSD_GROUP_W)
        x_g = xs_bf[:, gs]
        scores = lax.dot_general(c_g, b_g, (((1,), (1,)), ((), ())), preferred_element_type=F32)
        st_g = st_scr[:, gs]
        y_g = jnp.dot(c_g, st_g.astype(BF16), preferred_element_type=F32) * decay_in[:, gs]
        for j in range(SSD_HPG):
            h = g * SSD_HPG + j
            seg = cs[:, h:h + 1] - cs_t[h:h + 1, :]
            lmat = jnp.where(causal, jnp.exp(seg), 0.0) * dt_t[h:h + 1, :]
            m = (scores * lmat).astype(BF16)
            in_head = (lane >= j * SSD_HEAD_DIM) & (lane < (j + 1) * SSD_HEAD_DIM)
            x_h = jnp.where(in_head, x_g, jnp.zeros_like(x_g))
            y_g = y_g + jnp.dot(m, x_h, preferred_element_type=F32)
        ys.append(y_g)
        wx = (xs[:, gs] * w_end[:, gs]).astype(BF16)
        new = lax.dot_general(b_g, wx, (((0,), (0,)), ((), ())), preferred_element_type=F32)
        st_scr[:, gs] = st_g * chunk_decay[:, gs] + new
    y = jnp.concatenate(ys, axis=-1) + d_ref[...] * xs

    zf = z_ref[0].astype(F32)
    gz = y * (zf * jax.nn.sigmoid(zf))
    nw = nw_ref[...]
    outs = []
    for g in range(SSD_GROUPS):
        gs = slice(g * SSD_GROUP_W, (g + 1) * SSD_GROUP_W)
        outs.append(_rms(gz[:, gs], nw[:, gs]))
    y_ref[0] = jnp.concatenate(outs, axis=-1).astype(BF16)

    @pl.when(t == pl.num_programs(1) - 1)
    def _():
        st_out_ref[0] = st_scr[...]


def _ssd(z, xbc, dt, hist, h0, cw, cb, dtb, alog, dvec, nw, q, valid):
    bsz, length, _ = z.shape
    grid = (bsz, length // q)
    tok = lambda w: pl.BlockSpec((1, q, w), lambda b, i: (b, i, 0))
    per_b = lambda r, w: pl.BlockSpec((1, r, w), lambda b, i: (b, 0, 0))
    return pl.pallas_call(
        functools.partial(_ssd_kernel, q=q, valid=valid),
        grid=grid,
        in_specs=[tok(SSD_WIDTH), tok(SSD_CONV_DIM), tok(LANES), per_b(HIST_ROWS, SSD_CONV_DIM),
                  per_b(SSD_STATE, SSD_WIDTH), _const_spec(cw.shape), _const_spec(cb.shape), _const_spec(dtb.shape),
                  _const_spec(alog.shape), _const_spec(dvec.shape), _const_spec(nw.shape)],
        out_specs=[tok(SSD_WIDTH), per_b(SSD_STATE, SSD_WIDTH)],
        out_shape=[jax.ShapeDtypeStruct((bsz, length, SSD_WIDTH), BF16),
                   jax.ShapeDtypeStruct((bsz, SSD_STATE, SSD_WIDTH), F32)],
        scratch_shapes=[pltpu.VMEM((SSD_CONV_DIM // LANES, q + HIST_ROWS, LANES), F32),
                        pltpu.VMEM((SSD_STATE, SSD_WIDTH), F32)],
        compiler_params=_params(2),
        name="ssd",
    )(z, xbc, dt, hist, h0, cw, cb, dtb, alog, dvec, nw)


def _s5_kernel(u_ref, h0re_ref, h0im_ref, lr_ref, li_ref, bb_ref, cc_ref, d_ref, gw_ref, gb_ref,
               y_ref, hre_out_ref, him_out_ref, u_scr, y_scr, hs_scr, hre_scr, him_scr, *, tt, bsz):
    i = pl.program_id(0)

    @pl.when(i == 0)
    def _():
        hre_scr[...] = h0re_ref[...]
        him_scr[...] = h0im_ref[...]

    half = S5_BLOCK_STATE
    blocks = range(S5_BLOCKS)

    for b in range(bsz):
        ub = u_ref[b].astype(F32)
        for kb in blocks:
            u_scr[kb, pl.ds(b, tt, stride=bsz), :] = ub[:, kb * LANES:(kb + 1) * LANES]

    for kb in blocks:
        hs_scr[kb] = jnp.dot(u_scr[kb].astype(BF16), bb_ref[kb], preferred_element_type=F32)

    for kb in blocks:
        lr = jnp.broadcast_to(lr_ref[kb], (bsz, half))
        li = jnp.broadcast_to(li_ref[kb], (bsz, half))

        def body(t, carry, kb=kb, lr=lr, li=li):
            hr, hi = carry
            r0 = pl.multiple_of(t * bsz, bsz)
            nr = lr * hr - li * hi + hs_scr[kb, pl.ds(r0, bsz), 0:half]
            ni = lr * hi + li * hr + hs_scr[kb, pl.ds(r0, bsz), half:2 * half]
            hs_scr[kb, pl.ds(r0, bsz), 0:half] = nr
            hs_scr[kb, pl.ds(r0, bsz), half:2 * half] = ni
            return nr, ni

        hr, hi = lax.fori_loop(0, tt, body, (hre_scr[kb], him_scr[kb]), unroll=8)
        hre_scr[kb] = hr
        him_scr[kb] = hi

    ys = [jnp.dot(hs_scr[kb].astype(BF16), cc_ref[kb], preferred_element_type=F32)
          + d_ref[:, kb * LANES:(kb + 1) * LANES] * u_scr[kb] for kb in blocks]
    ges = [_gelu_tanh(y).astype(BF16) for y in ys]
    gls = [jnp.dot(ges[kb], gw_ref[kb], preferred_element_type=F32) + gb_ref[kb] for kb in blocks]
    for kb in blocks:
        y_scr[kb] = gls[kb][:, :LANES] * jax.nn.sigmoid(gls[kb][:, LANES:])
    for b in range(bsz):
        y_ref[b] = jnp.concatenate([y_scr[kb, pl.ds(b, tt, stride=bsz), :] for kb in blocks], axis=-1).astype(BF16)

    @pl.when(i == pl.num_programs(0) - 1)
    def _():
        hre_out_ref[...] = hre_scr[...]
        him_out_ref[...] = him_scr[...]


def _s5(u, h0re, h0im, lr, li, bb, cc, dvec, gw, gb, tt):
    bsz, length, _ = u.shape
    grid = (length // tt,)
    st_shape = (S5_BLOCKS, bsz, S5_BLOCK_STATE)
    tok = pl.BlockSpec((bsz, tt, S5_WIDTH), lambda i: (0, i, 0))
    tb_scratch = pltpu.VMEM((S5_BLOCKS, tt * bsz, LANES), F32)
    return pl.pallas_call(
        functools.partial(_s5_kernel, tt=tt, bsz=bsz),
        grid=grid,
        in_specs=[tok, _const_spec(st_shape), _const_spec(st_shape), _const_spec(lr.shape), _const_spec(li.shape),
                  _const_spec(bb.shape), _const_spec(cc.shape), _const_spec(dvec.shape), _const_spec(gw.shape),
                  _const_spec(gb.shape)],
        out_specs=[tok, pl.BlockSpec(st_shape, lambda i: (0, 0, 0)), pl.BlockSpec(st_shape, lambda i: (0, 0, 0))],
        out_shape=[jax.ShapeDtypeStruct(u.shape, BF16),
                   jax.ShapeDtypeStruct(st_shape, F32), jax.ShapeDtypeStruct(st_shape, F32)],
        scratch_shapes=[tb_scratch, tb_scratch, pltpu.VMEM((S5_BLOCKS, tt * bsz, 2 * S5_BLOCK_STATE), F32),
                        pltpu.VMEM(st_shape, F32), pltpu.VMEM(st_shape, F32)],
        compiler_params=_params(1),
        name="s5",
    )(u, h0re, h0im, lr, li, bb, cc, dvec, gw, gb)


def _ffn_kernel(x_ref, ys_ref, y5_ref, hist_ref, wout_ref, n1_ref, n2_ref, wup_ref, cw_ref, cb_ref, wdn_ref, n3_ref,
                y_ref, carry_ref, perm_scr, act_scr, *, tm):
    i = pl.program_id(1)
    b = 0
    nj = tm // SUBLANES
    grp = SUBLANES

    @pl.when(i == 0)
    def _():
        carry_ref[b] = hist_ref[b]

    mix = (jnp.dot(ys_ref[0], wout_ref[0:SSD_WIDTH, :], preferred_element_type=F32)
           + jnp.dot(y5_ref[0], wout_ref[SSD_WIDTH:, :], preferred_element_type=F32))
    h = x_ref[0] + _rms(mix, n1_ref[...])
    hn = _rms(h, n2_ref[...])

    n_slab = D_MODEL // LANES
    for s in range(n_slab):
        for seg in range(grp):
            perm_scr[s, pl.ds(seg, nj, stride=grp), :] = hn[seg * nj:(seg + 1) * nj, s * LANES:(s + 1) * LANES]
    hp = jnp.concatenate([perm_scr[s] for s in range(n_slab)], axis=-1).astype(BF16)

    first_sublane = lax.broadcasted_iota(jnp.int32, (grp, FFN_CHUNK), 0) == 0

    def conv_chunk(c0):
        cols = slice(c0, c0 + FFN_CHUNK)
        up = jnp.dot(hp, wup_ref[:, cols], preferred_element_type=F32)

        def wrapped(group, fill_row):
            fill = jnp.broadcast_to(carry_ref[b, fill_row:fill_row + 1, cols], (grp, FFN_CHUNK))
            return jnp.where(first_sublane, fill, pltpu.roll(group, 1, 0))

        s2 = wrapped(up[tm - 2 * grp:tm - grp], grp - 1)
        s1 = wrapped(up[tm - grp:tm], 2 * grp - 1)
        carry_ref[b, :, cols] = up[tm - 2 * grp:tm]
        prev1 = jnp.concatenate([s1, up[:tm - grp]], axis=0)
        prev2 = jnp.concatenate([s2, s1, up[:tm - 2 * grp]], axis=0)
        return (cb_ref[:, cols] + up * cw_ref[2:3, cols] + prev1 * cw_ref[1:2, cols] + prev2 * cw_ref[0:1, cols])

    for c in range(D_FF // FFN_CHUNK):
        gate = conv_chunk(c * FFN_CHUNK)
        val = conv_chunk(D_FF + c * FFN_CHUNK)
        act_scr[:, c * FFN_CHUNK:(c + 1) * FFN_CHUNK] = (_gelu_tanh(gate) * val).astype(BF16)

    ffn = _rms(jnp.dot(act_scr[...], wdn_ref[...], preferred_element_type=F32), n3_ref[...])
    for s in range(n_slab):
        perm_scr[s] = ffn[:, s * LANES:(s + 1) * LANES]
    ffn = jnp.concatenate(
        [jnp.concatenate([perm_scr[s, pl.ds(seg, nj, stride=grp), :] for seg in range(grp)], axis=0)
         for s in range(n_slab)], axis=-1)
    y_ref[0] = h + ffn


def _ffn(x, ys, y5_tb, hist, wout, n1, n2, wup, cw, cb, wdn, n3, tm):
    bsz, length, _ = x.shape
    assert FFN_CONV == 3 and tm % SUBLANES == 0 and tm // SUBLANES >= 2
    grid = (bsz, length // tm)
    tok = lambda w: pl.BlockSpec((1, tm, w), lambda b, i: (b, i, 0))
    per_b = pl.BlockSpec((1,) + hist.shape[1:], lambda b, i: (b, 0, 0))
    return pl.pallas_call(
        functools.partial(_ffn_kernel, tm=tm),
        grid=grid,
        in_specs=[tok(D_MODEL), tok(SSD_WIDTH), tok(S5_WIDTH), per_b,
                  _const_spec(wout.shape), _const_spec(n1.shape), _const_spec(n2.shape), _const_spec(wup.shape),
                  _const_spec(cw.shape), _const_spec(cb.shape), _const_spec(wdn.shape), _const_spec(n3.shape)],
        out_specs=[tok(D_MODEL), per_b],
        out_shape=[jax.ShapeDtypeStruct((bsz, length, D_MODEL), F32), jax.ShapeDtypeStruct(hist.shape, F32)],
        scratch_shapes=[pltpu.VMEM((D_MODEL // LANES, tm, LANES), F32), pltpu.VMEM((tm, D_FF), BF16)],
        compiler_params=_params(2),
        name="outffn",
    )(x, ys, y5_tb, hist, wout, n1, n2, wup, cw, cb, wdn, n3)


def _block_diag(w):
    _, r, c = w.shape
    w4 = w.reshape(S5_BLOCKS, S5_GPB, r, c)
    eye = jnp.eye(S5_GPB, dtype=w.dtype)
    return jnp.einsum("kgrc,gh->kgrhc", w4, eye).reshape(S5_BLOCKS, S5_GPB * r, S5_GPB * c)


def _prep_weights(pre_mix_norm_w, w_in, ssd_conv_w, ssd_conv_b, ssd_dt_bias, ssd_a_log, ssd_d, ssd_norm_w,
                  s5_lambda_re, s5_lambda_im, s5_log_dt, s5_b_re, s5_b_im, s5_c_re, s5_c_im, s5_d,
                  s5_glu_w, s5_glu_b, w_out, post_mix_norm_w, pre_ffn_norm_w, w_up, ffn_conv_w, ffn_conv_b,
                  w_down, post_ffn_norm_w):
    o_dt = SSD_WIDTH + SSD_CONV_DIM
    o_u = o_dt + SSD_HEADS
    pad_heads = lambda v: jnp.pad(v, (0, LANES - SSD_HEADS)).reshape(1, LANES)
    w = dict(
        nw0=pre_mix_norm_w.reshape(1, D_MODEL),
        wa=w_in[:, :o_dt].astype(BF16),
        wdt=jnp.pad(w_in[:, o_dt:o_u], ((0, 0), (0, LANES - SSD_HEADS))).astype(BF16),
        wu=w_in[:, o_u:].astype(BF16),
        cw=ssd_conv_w, cb=ssd_conv_b.reshape(1, SSD_CONV_DIM),
        dtb=pad_heads(ssd_dt_bias), alog=pad_heads(ssd_a_log),
        dssd=jnp.repeat(ssd_d, SSD_HEAD_DIM).reshape(1, SSD_WIDTH),
        nssd=ssd_norm_w.reshape(1, SSD_WIDTH),
        wout=w_out.astype(BF16), n1=post_mix_norm_w.reshape(1, D_MODEL), n2=pre_ffn_norm_w.reshape(1, D_MODEL),
        wup=w_up.astype(BF16), fcw=ffn_conv_w, fcb=ffn_conv_b.reshape(1, 2 * D_FF),
        wdn=w_down.astype(BF16), n3=post_ffn_norm_w.reshape(1, D_MODEL),
    )
    dt = jnp.exp(s5_log_dt)[:, None]
    mag = jnp.exp(s5_lambda_re * dt)
    ang = s5_lambda_im * dt
    lb_re = mag * jnp.cos(ang)
    lb_im = mag * jnp.sin(ang)
    den = s5_lambda_re * s5_lambda_re + s5_lambda_im * s5_lambda_im
    q_re = ((lb_re - 1) * s5_lambda_re + lb_im * s5_lambda_im) / den
    q_im = (lb_im * s5_lambda_re - (lb_re - 1) * s5_lambda_im) / den
    bb_re = q_re[..., None] * s5_b_re - q_im[..., None] * s5_b_im
    bb_im = q_re[..., None] * s5_b_im + q_im[..., None] * s5_b_re
    to_in = lambda m: _block_diag(jnp.swapaxes(m, 1, 2))
    w["bb"] = jnp.concatenate([to_in(bb_re), to_in(bb_im)], axis=-1).astype(BF16)
    to_out = lambda m: _block_diag(jnp.swapaxes(m, 1, 2))
    w["cc"] = jnp.concatenate([to_out(s5_c_re), to_out(-s5_c_im)], axis=1).astype(BF16)
    w["lr"] = lb_re.reshape(S5_BLOCKS, 1, S5_BLOCK_STATE)
    w["li"] = lb_im.reshape(S5_BLOCKS, 1, S5_BLOCK_STATE)
    w["d5"] = s5_d.reshape(1, S5_WIDTH)
    w["gw"] = jnp.concatenate([_block_diag(s5_glu_w[..., :S5_GROUP_CH]), _block_diag(s5_glu_w[..., S5_GROUP_CH:])],
                              axis=-1).astype(BF16)
    gb = lambda v: v.reshape(S5_BLOCKS, 1, LANES)
    w["gb"] = jnp.concatenate([gb(s5_glu_b[:, :S5_GROUP_CH]), gb(s5_glu_b[:, S5_GROUP_CH:])], axis=-1)
    return w


def _hist_tile(hist):
    return jnp.pad(hist, ((0, 0), (HIST_ROWS - hist.shape[1], 0), (0, 0)))


def _layer(x, conv_hist, ssd_h0, s5_re, s5_im, ffn_hist, w, *, tm, q, tt):
    bsz, length, _ = x.shape
    z, xbc, dt, u = _inproj(x, w["nw0"], w["wa"], w["wdt"], w["wu"], tm)
    new_conv = xbc[:, length - (SSD_CONV - 1):, :].astype(F32)

    lpad = -length % q
    if lpad:
        assert length < q, "a padded sequence must fit one SSD chunk"
        padt = lambda a: jnp.pad(a, ((0, 0), (0, lpad), (0, 0)))
        z, xbc, dt = padt(z), padt(xbc), padt(dt)
    h0 = jnp.transpose(ssd_h0, (0, 3, 1, 2)).reshape(bsz, SSD_STATE, SSD_WIDTH)
    y_ssd, st = _ssd(z, xbc, dt, _hist_tile(conv_hist), h0, w["cw"], w["cb"], w["dtb"], w["alog"], w["dssd"],
                     w["nssd"], q, length if lpad else q)
    y_ssd = y_ssd[:, :length]
    new_ssd = jnp.transpose(st.reshape(bsz, SSD_STATE, SSD_HEADS, SSD_HEAD_DIM), (0, 2, 3, 1))

    to_blocks = lambda s: jnp.transpose(s.reshape(bsz, S5_BLOCKS, S5_BLOCK_STATE), (1, 0, 2))
    from_blocks = lambda s: jnp.transpose(s, (1, 0, 2)).reshape(bsz, S5_GROUPS, S5_STATE)
    y5, hre, him = _s5(u, to_blocks(s5_re), to_blocks(s5_im), w["lr"], w["li"], w["bb"], w["cc"], w["d5"],
                       w["gw"], w["gb"], tt)

    last = SUBLANES - 1
    hist = jnp.zeros((bsz, 2 * SUBLANES, 2 * D_FF), F32).at[:, last::SUBLANES, :].set(ffn_hist)
    y, carry = _ffn(x, y_ssd, y5, hist, w["wout"], w["n1"], w["n2"], w["wup"], w["fcw"], w["fcb"],
                    w["wdn"], w["n3"], tm)
    new_ffn = carry[:, last::SUBLANES, :]
    return y, new_conv, new_ssd, from_blocks(hre), from_blocks(him), new_ffn


def _tiles(length):
    tm = min(length, 512)
    q = 128
    tt = min(length, 64)
    assert length % tm == 0 and length % tt == 0 and length >= SSD_CONV - 1
    return dict(tm=tm, q=q, tt=tt)


def kernel(x_prompt, x_sample, cache_ssd_conv, state_ssd, state_s5_re, state_s5_im, cache_ffn_conv, pre_mix_norm_w, w_in, ssd_conv_w, ssd_conv_b, ssd_dt_bias, ssd_a_log, ssd_d, ssd_norm_w, s5_lambda_re, s5_lambda_im, s5_log_dt, s5_b_re, s5_b_im, s5_c_re, s5_c_im, s5_d, s5_glu_w, s5_glu_b, w_out, post_mix_norm_w, pre_ffn_norm_w, w_up, ffn_conv_w, ffn_conv_b, w_down, post_ffn_norm_w):
    depth = w_in.shape[0]
    bsz = x_prompt.shape[0]
    dtp = x_prompt.dtype
    layer_params = (pre_mix_norm_w, w_in, ssd_conv_w, ssd_conv_b, ssd_dt_bias, ssd_a_log, ssd_d, ssd_norm_w,
                    s5_lambda_re, s5_lambda_im, s5_log_dt, s5_b_re, s5_b_im, s5_c_re, s5_c_im, s5_d,
                    s5_glu_w, s5_glu_b, w_out, post_mix_norm_w, pre_ffn_norm_w, w_up, ffn_conv_w, ffn_conv_b,
                    w_down, post_ffn_norm_w)
    y_prompt, y_sample = x_prompt, x_sample
    prompt_states, sample_states = [], []
    for l in range(depth):
        w = _prep_weights(*(p[l] for p in layer_params))
        y_prompt, *ps = _layer(
            y_prompt,
            jnp.zeros((bsz, SSD_CONV - 1, SSD_CONV_DIM), dtp),
            jnp.zeros((bsz, SSD_HEADS, SSD_HEAD_DIM, SSD_STATE), dtp),
            jnp.zeros((bsz, S5_GROUPS, S5_STATE), dtp),
            jnp.zeros((bsz, S5_GROUPS, S5_STATE), dtp),
            jnp.zeros((bsz, FFN_CONV - 1, 2 * D_FF), dtp),
            w, **_tiles(y_prompt.shape[1]))
        y_sample, *ss = _layer(
            y_sample, cache_ssd_conv[l], state_ssd[l], state_s5_re[l], state_s5_im[l], cache_ffn_conv[l],
            w, **_tiles(y_sample.shape[1]))
        prompt_states.append(ps)
        sample_states.append(ss)
    stack = lambda states: tuple(jnp.stack([s[k] for s in states]) for k in range(5))
    return (y_prompt, y_sample) + stack(prompt_states) + stack(sample_states)
```

```python
import functools
import math

import jax
import jax.numpy as jnp
from jax import lax
from jax.experimental import pallas as pl
from jax.experimental.pallas import tpu as pltpu

F32 = jnp.float32
BF16 = jnp.bfloat16

D_MODEL = 1024
SSD_WIDTH = 512
SSD_HEAD_DIM = 64
SSD_HEADS = 8
SSD_GROUPS = 2
SSD_HPG = SSD_HEADS // SSD_GROUPS
SSD_GROUP_W = SSD_WIDTH // SSD_GROUPS
SSD_STATE = 128
SSD_CONV = 4
SSD_CONV_DIM = SSD_WIDTH + 2 * SSD_GROUPS * SSD_STATE
S5_WIDTH = 512
S5_GROUP_CH = 16
S5_GROUPS = 32
S5_STATE = 64
S5_BLOCKS = 4
S5_GPB = S5_GROUPS // S5_BLOCKS
S5_BLOCK_STATE = S5_GPB * S5_STATE
D_FF = 2816
FFN_CONV = 3
EPS = 1e-6

LANES = 128
SUBLANES = 8
HIST_ROWS = SUBLANES
VMEM_LIMIT = 56 * 1024 * 1024

FFN_CHUNK = 256
SSD_SEQS_PER_STEP = 2
SSD_CHUNKS_PER_STEP = 2


def _rms(x, w):
    ms = jnp.mean(x * x, axis=-1, keepdims=True)
    return x * lax.rsqrt(ms + EPS) * w


def _gelu_tanh(x):
    c = math.sqrt(2.0 / math.pi)
    hx = 0.5 * x
    return hx + hx * jnp.tanh(x * (c + (c * 0.044715) * (x * x)))


def _softplus(x):
    return jnp.maximum(x, 0.0) + jnp.log1p(jnp.exp(-jnp.abs(x)))


def _causal_conv(ext, cur, w_ref, b_ref, c0, rows, taps):
    cols = slice(c0, c0 + LANES)
    ext[HIST_ROWS:HIST_ROWS + rows, :] = cur
    out = b_ref[:, cols] + cur * w_ref[taps - 1:taps, cols]
    for k in range(1, taps):
        out = out + ext[pl.ds(HIST_ROWS - k, rows), :] * w_ref[taps - 1 - k:taps - k, cols]
    ext[0:HIST_ROWS, :] = ext[rows:rows + HIST_ROWS, :]
    return out


def _const_spec(shape):
    nd = len(shape)
    return pl.BlockSpec(shape, lambda *_: (0,) * nd, pipeline_mode=pl.Buffered(1))


def _params(n_axes):
    return pltpu.CompilerParams(dimension_semantics=("arbitrary",) * n_axes, vmem_limit_bytes=VMEM_LIMIT)


def _inproj_kernel(x_ref, nw_ref, wa_ref, wdt_ref, wu_ref, z_ref, xbc_ref, dt_ref, u_ref):
    x = x_ref[0]
    xn = _rms(x, nw_ref[...]).astype(BF16)
    a = jnp.dot(xn, wa_ref[...], preferred_element_type=F32)
    z_ref[0] = a[:, :SSD_WIDTH].astype(BF16)
    xbc_ref[0] = a[:, SSD_WIDTH:].astype(BF16)
    dt_ref[0] = jnp.dot(xn, wdt_ref[...], preferred_element_type=F32)
    u_ref[0] = jnp.dot(xn, wu_ref[...], preferred_element_type=F32).astype(BF16)


def _inproj(x, nw, wa, wdt, wu, tm):
    bsz, length, _ = x.shape
    grid = (bsz, length // tm)
    tok = lambda w, dt: (jax.ShapeDtypeStruct((bsz, length, w), dt), pl.BlockSpec((1, tm, w), lambda b, i: (b, i, 0)))
    outs = [tok(SSD_WIDTH, BF16), tok(SSD_CONV_DIM, BF16), tok(LANES, F32), tok(S5_WIDTH, BF16)]
    return pl.pallas_call(
        _inproj_kernel,
        grid=grid,
        in_specs=[pl.BlockSpec((1, tm, D_MODEL), lambda b, i: (b, i, 0)),
                  _const_spec(nw.shape), _const_spec(wa.shape), _const_spec(wdt.shape), _const_spec(wu.shape)],
        out_specs=[o[1] for o in outs],
        out_shape=[o[0] for o in outs],
        compiler_params=_params(2),
        name="inproj",
    )(x, nw, wa, wdt, wu)


def _split3(v):
    hi = v.astype(BF16)
    r1 = v - hi.astype(F32)
    mid = r1.astype(BF16)
    lo = (r1 - mid.astype(F32)).astype(BF16)
    return hi, mid, lo


def _head_cols(a):
    q = a.shape[0]
    lane = lax.broadcasted_iota(jnp.int32, (q, LANES), 1)
    tiles = []
    for j in range(SSD_HEADS * SSD_HEAD_DIM // LANES):
        even = jnp.broadcast_to(a[:, 2 * j:2 * j + 1], (q, LANES))
        odd = jnp.broadcast_to(a[:, 2 * j + 1:2 * j + 2], (q, LANES))
        tiles.append(jnp.where(lane < SSD_HEAD_DIM, even, odd))
    return jnp.concatenate(tiles, axis=-1)


def _ssd_kernel(z_ref, xbc_ref, dt_ref, hist_ref, h0_ref, cw_ref, cb_ref, dtb_ref, alog_ref, d_ref, nw_ref,
                y_ref, st_out_ref, ext_scr, st_scr, *, q, valid, nb, nc):
    t = pl.program_id(1)
    n_slab = SSD_CONV_DIM // LANES

    @pl.when(t == 0)
    def _():
        for s in range(nb):
            for j in range(n_slab):
                ext_scr[s * n_slab + j, 0:HIST_ROWS, :] = hist_ref[s, :, j * LANES:(j + 1) * LANES]
        st_scr[...] = h0_ref[...]

    for c in range(nc):
        for s in range(nb):
            _ssd_chunk(s, slice(c * q, (c + 1) * q), z_ref, xbc_ref, dt_ref, cw_ref, cb_ref, dtb_ref, alog_ref, d_ref,
                       nw_ref, y_ref, ext_scr, st_scr, q=q, valid=valid)

    @pl.when(t == pl.num_programs(1) - 1)
    def _():
        st_out_ref[...] = st_scr[...]


def _ssd_chunk(s, rows, z_ref, xbc_ref, dt_ref, cw_ref, cb_ref, dtb_ref, alog_ref, d_ref, nw_ref, y_ref, ext_scr,
               st_scr, *, q, valid):
    n_slab = SSD_CONV_DIM // LANES
    st_scr = st_scr.at[s]
    xraw = xbc_ref[s, rows, :].astype(F32)
    conv = jnp.concatenate([_causal_conv(ext_scr.at[s * n_slab + j], xraw[:, j * LANES:(j + 1) * LANES], cw_ref, cb_ref,
                                         j * LANES, q, SSD_CONV) for j in range(n_slab)], axis=-1)
    xact = conv * jax.nn.sigmoid(conv)
    xs = xact[:, :SSD_WIDTH]
    xs_bf = xs.astype(BF16)

    dt = _softplus(dt_ref[s, rows, :] + dtb_ref[...])
    row = lax.broadcasted_iota(jnp.int32, (q, q), 0)
    col = lax.broadcasted_iota(jnp.int32, (q, q), 1)
    causal = row >= col
    if valid < q:
        dt = jnp.where(lax.broadcasted_iota(jnp.int32, dt.shape, 0) < valid, dt, 0.0)
    a = -jnp.exp(alog_ref[...])
    tril = jnp.where(causal, 1.0, 0.0).astype(BF16)
    cs = sum(jnp.dot(tril, part, preferred_element_type=F32) for part in _split3(dt * a))
    cs_t = cs.T
    dt_t = dt.T
    cs_w = _head_cols(cs)
    dt_w = _head_cols(dt)
    cs_last_w = cs_w[q - 1:q, :]
    decay_in = jnp.exp(cs_w)
    w_end = jnp.exp(cs_last_w - cs_w) * dt_w
    chunk_decay = jnp.exp(cs_last_w)

    lane = lax.broadcasted_iota(jnp.int32, (1, SSD_GROUP_W), 1)
    ys = []
    for g in range(SSD_GROUPS):
        b_g = xact[:, SSD_WIDTH + g * SSD_STATE:SSD_WIDTH + (g + 1) * SSD_STATE].astype(BF16)
        c_off = SSD_WIDTH + SSD_GROUPS * SSD_STATE
        c_g = xact[:, c_off + g * SSD_STATE:c_off + (g + 1) * SSD_STATE].astype(BF16)
        gs = slice(g * SSD_GROUP_W, (g + 1) * SSD_GROUP_W)
        x_g = xs_bf[:, gs]
        scores = lax.dot_general(c_g, b_g, (((1,), (1,)), ((), ())), preferred_element_type=F32)
        st_g = st_scr[:, gs]
        y_g = jnp.dot(c_g, st_g.astype(BF16), preferred_element_type=F32) * decay_in[:, gs]
        for j in range(SSD_HPG):
            h = g * SSD_HPG + j
            seg = cs[:, h:h + 1] - cs_t[h:h + 1, :]
            lmat = jnp.where(causal, jnp.exp(seg), 0.0) * dt_t[h:h + 1, :]
            m = (scores * lmat).astype(BF16)
            in_head = (lane >= j * SSD_HEAD_DIM) & (lane < (j + 1) * SSD_HEAD_DIM)
            x_h = jnp.where(in_head, x_g, jnp.zeros_like(x_g))
            y_g = y_g + jnp.dot(m, x_h, preferred_element_type=F32)
        ys.append(y_g)
        wx = (xs[:, gs] * w_end[:, gs]).astype(BF16)
        new = lax.dot_general(b_g, wx, (((0,), (0,)), ((), ())), preferred_element_type=F32)
        st_scr[:, gs] = st_g * chunk_decay[:, gs] + new
    y = jnp.concatenate(ys, axis=-1) + d_ref[...] * xs

    zf = z_ref[s, rows, :].astype(F32)
    gz = y * (zf * jax.nn.sigmoid(zf))
    nw = nw_ref[...]
    outs = []
    for g in range(SSD_GROUPS):
        gs = slice(g * SSD_GROUP_W, (g + 1) * SSD_GROUP_W)
        outs.append(_rms(gz[:, gs], nw[:, gs]))
    y_ref[s, rows, :] = jnp.concatenate(outs, axis=-1).astype(BF16)


def _ssd(z, xbc, dt, hist, h0, cw, cb, dtb, alog, dvec, nw, q, valid, nb, nc):
    bsz, length, _ = z.shape
    assert bsz % nb == 0 and length % (nc * q) == 0
    grid = (bsz // nb, length // (nc * q))
    tok = lambda w: pl.BlockSpec((nb, nc * q, w), lambda b, i: (b, i, 0))
    per_b = lambda r, w: pl.BlockSpec((nb, r, w), lambda b, i: (b, 0, 0))
    return pl.pallas_call(
        functools.partial(_ssd_kernel, q=q, valid=valid, nb=nb, nc=nc),
        grid=grid,
        in_specs=[tok(SSD_WIDTH), tok(SSD_CONV_DIM), tok(LANES), per_b(HIST_ROWS, SSD_CONV_DIM),
                  per_b(SSD_STATE, SSD_WIDTH), _const_spec(cw.shape), _const_spec(cb.shape), _const_spec(dtb.shape),
                  _const_spec(alog.shape), _const_spec(dvec.shape), _const_spec(nw.shape)],
        out_specs=[tok(SSD_WIDTH), per_b(SSD_STATE, SSD_WIDTH)],
        out_shape=[jax.ShapeDtypeStruct((bsz, length, SSD_WIDTH), BF16),
                   jax.ShapeDtypeStruct((bsz, SSD_STATE, SSD_WIDTH), F32)],
        scratch_shapes=[pltpu.VMEM((nb * SSD_CONV_DIM // LANES, q + HIST_ROWS, LANES), F32),
                        pltpu.VMEM((nb, SSD_STATE, SSD_WIDTH), F32)],
        compiler_params=_params(2),
        name="ssd",
    )(z, xbc, dt, hist, h0, cw, cb, dtb, alog, dvec, nw)


def _s5_kernel(u_ref, h0re_ref, h0im_ref, lr_ref, li_ref, bb_ref, cc_ref, d_ref, gw_ref, gb_ref,
               y_ref, hre_out_ref, him_out_ref, u_scr, y_scr, hs_scr, hre_scr, him_scr, *, tt, bsz):
    i = pl.program_id(0)

    @pl.when(i == 0)
    def _():
        hre_scr[...] = h0re_ref[...]
        him_scr[...] = h0im_ref[...]

    half = S5_BLOCK_STATE
    blocks = range(S5_BLOCKS)

    for b in range(bsz):
        ub = u_ref[b].astype(F32)
        for kb in blocks:
            u_scr[kb, pl.ds(b, tt, stride=bsz), :] = ub[:, kb * LANES:(kb + 1) * LANES]

    for kb in blocks:
        hs_scr[kb] = jnp.dot(u_scr[kb].astype(BF16), bb_ref[kb], preferred_element_type=F32)

    for kb in blocks:
        lr = jnp.broadcast_to(lr_ref[kb], (bsz, half))
        li = jnp.broadcast_to(li_ref[kb], (bsz, half))

        def body(t, carry, kb=kb, lr=lr, li=li):
            hr, hi = carry
            r0 = pl.multiple_of(t * bsz, bsz)
            nr = lr * hr - li * hi + hs_scr[kb, pl.ds(r0, bsz), 0:half]
            ni = lr * hi + li * hr + hs_scr[kb, pl.ds(r0, bsz), half:2 * half]
            hs_scr[kb, pl.ds(r0, bsz), 0:half] = nr
            hs_scr[kb, pl.ds(r0, bsz), half:2 * half] = ni
            return nr, ni

        hr, hi = lax.fori_loop(0, tt, body, (hre_scr[kb], him_scr[kb]), unroll=8)
        hre_scr[kb] = hr
        him_scr[kb] = hi

    ys = [jnp.dot(hs_scr[kb].astype(BF16), cc_ref[kb], preferred_element_type=F32)
          + d_ref[:, kb * LANES:(kb + 1) * LANES] * u_scr[kb] for kb in blocks]
    ges = [_gelu_tanh(y).astype(BF16) for y in ys]
    gls = [jnp.dot(ges[kb], gw_ref[kb], preferred_element_type=F32) + gb_ref[kb] for kb in blocks]
    for kb in blocks:
        y_scr[kb] = gls[kb][:, :LANES] * jax.nn.sigmoid(gls[kb][:, LANES:])
    for b in range(bsz):
        y_ref[b] = jnp.concatenate([y_scr[kb, pl.ds(b, tt, stride=bsz), :] for kb in blocks], axis=-1).astype(BF16)

    @pl.when(i == pl.num_programs(0) - 1)
    def _():
        hre_out_ref[...] = hre_scr[...]
        him_out_ref[...] = him_scr[...]


def _s5(u, h0re, h0im, lr, li, bb, cc, dvec, gw, gb, tt):
    bsz, length, _ = u.shape
    grid = (length // tt,)
    st_shape = (S5_BLOCKS, bsz, S5_BLOCK_STATE)
    tok = pl.BlockSpec((bsz, tt, S5_WIDTH), lambda i: (0, i, 0))
    tb_scratch = pltpu.VMEM((S5_BLOCKS, tt * bsz, LANES), F32)
    return pl.pallas_call(
        functools.partial(_s5_kernel, tt=tt, bsz=bsz),
        grid=grid,
        in_specs=[tok, _const_spec(st_shape), _const_spec(st_shape), _const_spec(lr.shape), _const_spec(li.shape),
                  _const_spec(bb.shape), _const_spec(cc.shape), _const_spec(dvec.shape), _const_spec(gw.shape),
                  _const_spec(gb.shape)],
        out_specs=[tok, pl.BlockSpec(st_shape, lambda i: (0, 0, 0)), pl.BlockSpec(st_shape, lambda i: (0, 0, 0))],
        out_shape=[jax.ShapeDtypeStruct(u.shape, BF16),
                   jax.ShapeDtypeStruct(st_shape, F32), jax.ShapeDtypeStruct(st_shape, F32)],
        scratch_shapes=[tb_scratch, tb_scratch, pltpu.VMEM((S5_BLOCKS, tt * bsz, 2 * S5_BLOCK_STATE), F32),
                        pltpu.VMEM(st_shape, F32), pltpu.VMEM(st_shape, F32)],
        compiler_params=_params(1),
        name="s5",
    )(u, h0re, h0im, lr, li, bb, cc, dvec, gw, gb)


def _ffn_kernel(x_ref, ys_ref, y5_ref, hist_ref, wout_ref, n1_ref, n2_ref, wup_ref, cw_ref, cb_ref, wdn_ref, n3_ref,
                y_ref, carry_ref, perm_scr, act_scr, *, tm):
    i = pl.program_id(1)
    b = 0
    nj = tm // SUBLANES
    grp = SUBLANES

    @pl.when(i == 0)
    def _():
        carry_ref[b] = hist_ref[b]

    mix = (jnp.dot(ys_ref[0], wout_ref[0:SSD_WIDTH, :], preferred_element_type=F32)
           + jnp.dot(y5_ref[0], wout_ref[SSD_WIDTH:, :], preferred_element_type=F32))
    h = x_ref[0] + _rms(mix, n1_ref[...])
    hn = _rms(h, n2_ref[...])

    n_slab = D_MODEL // LANES
    for s in range(n_slab):
        for seg in range(grp):
            perm_scr[s, pl.ds(seg, nj, stride=grp), :] = hn[seg * nj:(seg + 1) * nj, s * LANES:(s + 1) * LANES]
    hp = jnp.concatenate([perm_scr[s] for s in range(n_slab)], axis=-1).astype(BF16)

    first_sublane = lax.broadcasted_iota(jnp.int32, (grp, FFN_CHUNK), 0) == 0

    def conv_chunk(c0):
        cols = slice(c0, c0 + FFN_CHUNK)
        up = jnp.dot(hp, wup_ref[:, cols], preferred_element_type=F32)

        def wrapped(group, fill_row):
            fill = jnp.broadcast_to(carry_ref[b, fill_row:fill_row + 1, cols], (grp, FFN_CHUNK))
            return jnp.where(first_sublane, fill, pltpu.roll(group, 1, 0))

        s2 = wrapped(up[tm - 2 * grp:tm - grp], grp - 1)
        s1 = wrapped(up[tm - grp:tm], 2 * grp - 1)
        carry_ref[b, :, cols] = up[tm - 2 * grp:tm]
        prev1 = jnp.concatenate([s1, up[:tm - grp]], axis=0)
        prev2 = jnp.concatenate([s2, s1, up[:tm - 2 * grp]], axis=0)
        return (cb_ref[:, cols] + up * cw_ref[2:3, cols] + prev1 * cw_ref[1:2, cols] + prev2 * cw_ref[0:1, cols])

    for c in range(D_FF // FFN_CHUNK):
        gate = conv_chunk(c * FFN_CHUNK)
        val = conv_chunk(D_FF + c * FFN_CHUNK)
        act_scr[:, c * FFN_CHUNK:(c + 1) * FFN_CHUNK] = (_gelu_tanh(gate) * val).astype(BF16)

    ffn = _rms(jnp.dot(act_scr[...], wdn_ref[...], preferred_element_type=F32), n3_ref[...])
    for s in range(n_slab):
        perm_scr[s] = ffn[:, s * LANES:(s + 1) * LANES]
    ffn = jnp.concatenate(
        [jnp.concatenate([perm_scr[s, pl.ds(seg, nj, stride=grp), :] for seg in range(grp)], axis=0)
         for s in range(n_slab)], axis=-1)
    y_ref[0] = h + ffn


def _ffn(x, ys, y5_tb, hist, wout, n1, n2, wup, cw, cb, wdn, n3, tm):
    bsz, length, _ = x.shape
    assert FFN_CONV == 3 and tm % SUBLANES == 0 and tm // SUBLANES >= 2
    grid = (bsz, length // tm)
    tok = lambda w: pl.BlockSpec((1, tm, w), lambda b, i: (b, i, 0))
    per_b = pl.BlockSpec((1,) + hist.shape[1:], lambda b, i: (b, 0, 0))
    return pl.pallas_call(
        functools.partial(_ffn_kernel, tm=tm),
        grid=grid,
        in_specs=[tok(D_MODEL), tok(SSD_WIDTH), tok(S5_WIDTH), per_b,
                  _const_spec(wout.shape), _const_spec(n1.shape), _const_spec(n2.shape), _const_spec(wup.shape),
                  _const_spec(cw.shape), _const_spec(cb.shape), _const_spec(wdn.shape), _const_spec(n3.shape)],
        out_specs=[tok(D_MODEL), per_b],
        out_shape=[jax.ShapeDtypeStruct((bsz, length, D_MODEL), F32), jax.ShapeDtypeStruct(hist.shape, F32)],
        scratch_shapes=[pltpu.VMEM((D_MODEL // LANES, tm, LANES), F32), pltpu.VMEM((tm, D_FF), BF16)],
        compiler_params=_params(2),
        name="outffn",
    )(x, ys, y5_tb, hist, wout, n1, n2, wup, cw, cb, wdn, n3)


def _block_diag(w):
    _, r, c = w.shape
    w4 = w.reshape(S5_BLOCKS, S5_GPB, r, c)
    eye = jnp.eye(S5_GPB, dtype=w.dtype)
    return jnp.einsum("kgrc,gh->kgrhc", w4, eye).reshape(S5_BLOCKS, S5_GPB * r, S5_GPB * c)


def _prep_weights(pre_mix_norm_w, w_in, ssd_conv_w, ssd_conv_b, ssd_dt_bias, ssd_a_log, ssd_d, ssd_norm_w,
                  s5_lambda_re, s5_lambda_im, s5_log_dt, s5_b_re, s5_b_im, s5_c_re, s5_c_im, s5_d,
                  s5_glu_w, s5_glu_b, w_out, post_mix_norm_w, pre_ffn_norm_w, w_up, ffn_conv_w, ffn_conv_b,
                  w_down, post_ffn_norm_w):
    o_dt = SSD_WIDTH + SSD_CONV_DIM
    o_u = o_dt + SSD_HEADS
    pad_heads = lambda v: jnp.pad(v, (0, LANES - SSD_HEADS)).reshape(1, LANES)
    w = dict(
        nw0=pre_mix_norm_w.reshape(1, D_MODEL),
        wa=w_in[:, :o_dt].astype(BF16),
        wdt=jnp.pad(w_in[:, o_dt:o_u], ((0, 0), (0, LANES - SSD_HEADS))).astype(BF16),
        wu=w_in[:, o_u:].astype(BF16),
        cw=ssd_conv_w, cb=ssd_conv_b.reshape(1, SSD_CONV_DIM),
        dtb=pad_heads(ssd_dt_bias), alog=pad_heads(ssd_a_log),
        dssd=jnp.repeat(ssd_d, SSD_HEAD_DIM).reshape(1, SSD_WIDTH),
        nssd=ssd_norm_w.reshape(1, SSD_WIDTH),
        wout=w_out.astype(BF16), n1=post_mix_norm_w.reshape(1, D_MODEL), n2=pre_ffn_norm_w.reshape(1, D_MODEL),
        wup=w_up.astype(BF16), fcw=ffn_conv_w, fcb=ffn_conv_b.reshape(1, 2 * D_FF),
        wdn=w_down.astype(BF16), n3=post_ffn_norm_w.reshape(1, D_MODEL),
    )
    dt = jnp.exp(s5_log_dt)[:, None]
    mag = jnp.exp(s5_lambda_re * dt)
    ang = s5_lambda_im * dt
    lb_re = mag * jnp.cos(ang)
    lb_im = mag * jnp.sin(ang)
    den = s5_lambda_re * s5_lambda_re + s5_lambda_im * s5_lambda_im
    q_re = ((lb_re - 1) * s5_lambda_re + lb_im * s5_lambda_im) / den
    q_im = (lb_im * s5_lambda_re - (lb_re - 1) * s5_lambda_im) / den
    bb_re = q_re[..., None] * s5_b_re - q_im[..., None] * s5_b_im
    bb_im = q_re[..., None] * s5_b_im + q_im[..., None] * s5_b_re
    to_in = lambda m: _block_diag(jnp.swapaxes(m, 1, 2))
    w["bb"] = jnp.concatenate([to_in(bb_re), to_in(bb_im)], axis=-1).astype(BF16)
    to_out = lambda m: _block_diag(jnp.swapaxes(m, 1, 2))
    w["cc"] = jnp.concatenate([to_out(s5_c_re), to_out(-s5_c_im)], axis=1).astype(BF16)
    w["lr"] = lb_re.reshape(S5_BLOCKS, 1, S5_BLOCK_STATE)
    w["li"] = lb_im.reshape(S5_BLOCKS, 1, S5_BLOCK_STATE)
    w["d5"] = s5_d.reshape(1, S5_WIDTH)
    w["gw"] = jnp.concatenate([_block_diag(s5_glu_w[..., :S5_GROUP_CH]), _block_diag(s5_glu_w[..., S5_GROUP_CH:])],
                              axis=-1).astype(BF16)
    gb = lambda v: v.reshape(S5_BLOCKS, 1, LANES)
    w["gb"] = jnp.concatenate([gb(s5_glu_b[:, :S5_GROUP_CH]), gb(s5_glu_b[:, S5_GROUP_CH:])], axis=-1)
    return w


def _hist_tile(hist):
    return jnp.pad(hist, ((0, 0), (HIST_ROWS - hist.shape[1], 0), (0, 0)))


def _layer(x, conv_hist, ssd_h0, s5_re, s5_im, ffn_hist, w, *, tm_in, tm, q, tt, nb, nc):
    bsz, length, _ = x.shape
    z, xbc, dt, u = _inproj(x, w["nw0"], w["wa"], w["wdt"], w["wu"], tm_in)
    new_conv = xbc[:, length - (SSD_CONV - 1):, :].astype(F32)

    lpad = -length % q
    if lpad:
        assert length < q, "a padded sequence must fit one SSD chunk"
        padt = lambda a: jnp.pad(a, ((0, 0), (0, lpad), (0, 0)))
        z, xbc, dt = padt(z), padt(xbc), padt(dt)
    h0 = jnp.transpose(ssd_h0, (0, 3, 1, 2)).reshape(bsz, SSD_STATE, SSD_WIDTH)
    y_ssd, st = _ssd(z, xbc, dt, _hist_tile(conv_hist), h0, w["cw"], w["cb"], w["dtb"], w["alog"], w["dssd"],
                     w["nssd"], q, length if lpad else q, nb, nc)
    y_ssd = y_ssd[:, :length]
    new_ssd = jnp.transpose(st.reshape(bsz, SSD_STATE, SSD_HEADS, SSD_HEAD_DIM), (0, 2, 3, 1))

    to_blocks = lambda s: jnp.transpose(s.reshape(bsz, S5_BLOCKS, S5_BLOCK_STATE), (1, 0, 2))
    from_blocks = lambda s: jnp.transpose(s, (1, 0, 2)).reshape(bsz, S5_GROUPS, S5_STATE)
    y5, hre, him = _s5(u, to_blocks(s5_re), to_blocks(s5_im), w["lr"], w["li"], w["bb"], w["cc"], w["d5"],
                       w["gw"], w["gb"], tt)

    last = SUBLANES - 1
    hist = jnp.zeros((bsz, 2 * SUBLANES, 2 * D_FF), F32).at[:, last::SUBLANES, :].set(ffn_hist)
    y, carry = _ffn(x, y_ssd, y5, hist, w["wout"], w["n1"], w["n2"], w["wup"], w["fcw"], w["fcb"],
                    w["wdn"], w["n3"], tm)
    new_ffn = carry[:, last::SUBLANES, :]
    return y, new_conv, new_ssd, from_blocks(hre), from_blocks(him), new_ffn


def _tiles(length):
    tm_in = min(length, 1024)
    tm = min(length, 512)
    q = 128
    nc = max(1, min(SSD_CHUNKS_PER_STEP, length // q))
    tt = min(length, 128)
    assert length % tm_in == 0 and length % tm == 0 and length % tt == 0 and length >= SSD_CONV - 1
    return dict(tm_in=tm_in, tm=tm, q=q, tt=tt, nb=SSD_SEQS_PER_STEP, nc=nc)


def kernel(x_prompt, x_sample, cache_ssd_conv, state_ssd, state_s5_re, state_s5_im, cache_ffn_conv, pre_mix_norm_w, w_in, ssd_conv_w, ssd_conv_b, ssd_dt_bias, ssd_a_log, ssd_d, ssd_norm_w, s5_lambda_re, s5_lambda_im, s5_log_dt, s5_b_re, s5_b_im, s5_c_re, s5_c_im, s5_d, s5_glu_w, s5_glu_b, w_out, post_mix_norm_w, pre_ffn_norm_w, w_up, ffn_conv_w, ffn_conv_b, w_down, post_ffn_norm_w):
    depth = w_in.shape[0]
    bsz = x_prompt.shape[0]
    dtp = x_prompt.dtype
    layer_params = (pre_mix_norm_w, w_in, ssd_conv_w, ssd_conv_b, ssd_dt_bias, ssd_a_log, ssd_d, ssd_norm_w,
                    s5_lambda_re, s5_lambda_im, s5_log_dt, s5_b_re, s5_b_im, s5_c_re, s5_c_im, s5_d,
                    s5_glu_w, s5_glu_b, w_out, post_mix_norm_w, pre_ffn_norm_w, w_up, ffn_conv_w, ffn_conv_b,
                    w_down, post_ffn_norm_w)
    y_prompt, y_sample = x_prompt, x_sample
    prompt_states, sample_states = [], []
    for l in range(depth):
        w = _prep_weights(*(p[l] for p in layer_params))
        y_prompt, *ps = _layer(
            y_prompt,
            jnp.zeros((bsz, SSD_CONV - 1, SSD_CONV_DIM), dtp),
            jnp.zeros((bsz, SSD_HEADS, SSD_HEAD_DIM, SSD_STATE), dtp),
            jnp.zeros((bsz, S5_GROUPS, S5_STATE), dtp),
            jnp.zeros((bsz, S5_GROUPS, S5_STATE), dtp),
            jnp.zeros((bsz, FFN_CONV - 1, 2 * D_FF), dtp),
            w, **_tiles(y_prompt.shape[1]))
        y_sample, *ss = _layer(
            y_sample, cache_ssd_conv[l], state_ssd[l], state_s5_re[l], state_s5_im[l], cache_ffn_conv[l],
            w, **_tiles(y_sample.shape[1]))
        prompt_states.append(ps)
        sample_states.append(ss)
    stack = lambda states: tuple(jnp.stack([s[k] for s in states]) for k in range(5))
    return (y_prompt, y_sample) + stack(prompt_states) + stack(sample_states)
```

```python
import functools
import math

import jax
import jax.numpy as jnp
from jax import lax
from jax.experimental import pallas as pl
from jax.experimental.pallas import tpu as pltpu

F32 = jnp.float32
BF16 = jnp.bfloat16

D_MODEL = 1024
SSD_WIDTH = 512
SSD_HEAD_DIM = 64
SSD_HEADS = 8
SSD_GROUPS = 2
SSD_HPG = SSD_HEADS // SSD_GROUPS
SSD_GROUP_W = SSD_WIDTH // SSD_GROUPS
SSD_STATE = 128
SSD_CONV = 4
SSD_CONV_DIM = SSD_WIDTH + 2 * SSD_GROUPS * SSD_STATE
S5_WIDTH = 512
S5_GROUP_CH = 16
S5_GROUPS = 32
S5_STATE = 64
S5_BLOCKS = 4
S5_GPB = S5_GROUPS // S5_BLOCKS
S5_BLOCK_STATE = S5_GPB * S5_STATE
S5_SCAN_BLOCKS = 2
D_FF = 2816
FFN_CONV = 3
EPS = 1e-6

LANES = 128
SUBLANES = 8
HIST_ROWS = SUBLANES
VMEM_LIMIT = 56 * 1024 * 1024

FFN_CHUNK = 256
SSD_SEQS_PER_STEP = 2
SSD_CHUNKS_PER_STEP = 4


def _rms(x, w):
    ms = jnp.mean(x * x, axis=-1, keepdims=True)
    return x * lax.rsqrt(ms + EPS) * w


def _gelu_tanh(x):
    c = math.sqrt(2.0 / math.pi)
    hx = 0.5 * x
    return hx + hx * jnp.tanh(x * (c + (c * 0.044715) * (x * x)))


def _softplus(x):
    return jnp.maximum(x, 0.0) + jnp.log1p(jnp.exp(-jnp.abs(x)))


def _causal_conv(ext, cur, w_ref, b_ref, c0, rows, taps):
    cols = slice(c0, c0 + LANES)
    ext[HIST_ROWS:HIST_ROWS + rows, :] = cur
    out = b_ref[:, cols] + cur * w_ref[taps - 1:taps, cols]
    for k in range(1, taps):
        out = out + ext[pl.ds(HIST_ROWS - k, rows), :] * w_ref[taps - 1 - k:taps - k, cols]
    ext[0:HIST_ROWS, :] = ext[rows:rows + HIST_ROWS, :]
    return out


def _const_spec(shape):
    nd = len(shape)
    return pl.BlockSpec(shape, lambda *_: (0,) * nd, pipeline_mode=pl.Buffered(1))


def _params(n_axes):
    return pltpu.CompilerParams(dimension_semantics=("arbitrary",) * n_axes, vmem_limit_bytes=VMEM_LIMIT)


def _inproj_kernel(x_ref, nw_ref, wa_ref, wdt_ref, wu_ref, z_ref, xbc_ref, dt_ref, u_ref):
    x = x_ref[0]
    xn = _rms(x, nw_ref[...]).astype(BF16)
    a = jnp.dot(xn, wa_ref[...], preferred_element_type=F32)
    z_ref[0] = a[:, :SSD_WIDTH].astype(BF16)
    xbc_ref[0] = a[:, SSD_WIDTH:].astype(BF16)
    dt_ref[0] = jnp.dot(xn, wdt_ref[...], preferred_element_type=F32)
    u_ref[0] = jnp.dot(xn, wu_ref[...], preferred_element_type=F32).astype(BF16)


def _inproj(x, nw, wa, wdt, wu, tm):
    bsz, length, _ = x.shape
    grid = (bsz, length // tm)
    tok = lambda w, dt: (jax.ShapeDtypeStruct((bsz, length, w), dt), pl.BlockSpec((1, tm, w), lambda b, i: (b, i, 0)))
    outs = [tok(SSD_WIDTH, BF16), tok(SSD_CONV_DIM, BF16), tok(LANES, F32), tok(S5_WIDTH, BF16)]
    return pl.pallas_call(
        _inproj_kernel,
        grid=grid,
        in_specs=[pl.BlockSpec((1, tm, D_MODEL), lambda b, i: (b, i, 0)),
                  _const_spec(nw.shape), _const_spec(wa.shape), _const_spec(wdt.shape), _const_spec(wu.shape)],
        out_specs=[o[1] for o in outs],
        out_shape=[o[0] for o in outs],
        compiler_params=_params(2),
        name="inproj",
    )(x, nw, wa, wdt, wu)


def _split3(v):
    hi = v.astype(BF16)
    r1 = v - hi.astype(F32)
    mid = r1.astype(BF16)
    lo = (r1 - mid.astype(F32)).astype(BF16)
    return hi, mid, lo


def _head_cols(a):
    q = a.shape[0]
    lane = lax.broadcasted_iota(jnp.int32, (q, LANES), 1)
    tiles = []
    for j in range(SSD_HEADS * SSD_HEAD_DIM // LANES):
        even = jnp.broadcast_to(a[:, 2 * j:2 * j + 1], (q, LANES))
        odd = jnp.broadcast_to(a[:, 2 * j + 1:2 * j + 2], (q, LANES))
        tiles.append(jnp.where(lane < SSD_HEAD_DIM, even, odd))
    return jnp.concatenate(tiles, axis=-1)


def _ssd_kernel(z_ref, xbc_ref, dt_ref, hist_ref, h0_ref, cw_ref, cb_ref, dtb_ref, alog_ref, d_ref, nw_ref,
                y_ref, st_out_ref, ext_scr, st_scr, *, q, valid, nb, nc):
    t = pl.program_id(1)
    n_slab = SSD_CONV_DIM // LANES

    @pl.when(t == 0)
    def _():
        for s in range(nb):
            for j in range(n_slab):
                ext_scr[s * n_slab + j, 0:HIST_ROWS, :] = hist_ref[s, :, j * LANES:(j + 1) * LANES]
        st_scr[...] = h0_ref[...]

    for c in range(nc):
        chains = [_ssd_chunk(s, slice(c * q, (c + 1) * q), z_ref, xbc_ref, dt_ref, cw_ref, cb_ref, dtb_ref, alog_ref,
                             d_ref, nw_ref, y_ref, ext_scr, st_scr, q=q, valid=valid) for s in range(nb)]
        for _ in zip(*chains):
            pass

    @pl.when(t == pl.num_programs(1) - 1)
    def _():
        st_out_ref[...] = st_scr[...]


def _ssd_chunk(s, rows, z_ref, xbc_ref, dt_ref, cw_ref, cb_ref, dtb_ref, alog_ref, d_ref, nw_ref, y_ref, ext_scr,
               st_scr, *, q, valid):
    n_slab = SSD_CONV_DIM // LANES
    st_scr = st_scr.at[s]
    xraw = xbc_ref[s, rows, :].astype(F32)
    conv = jnp.concatenate([_causal_conv(ext_scr.at[s * n_slab + j], xraw[:, j * LANES:(j + 1) * LANES], cw_ref, cb_ref,
                                         j * LANES, q, SSD_CONV) for j in range(n_slab)], axis=-1)
    yield
    xact = conv * jax.nn.sigmoid(conv)
    xs = xact[:, :SSD_WIDTH]
    xs_bf = xs.astype(BF16)
    yield

    dt = _softplus(dt_ref[s, rows, :] + dtb_ref[...])
    row = lax.broadcasted_iota(jnp.int32, (q, q), 0)
    col = lax.broadcasted_iota(jnp.int32, (q, q), 1)
    causal = row >= col
    if valid < q:
        dt = jnp.where(lax.broadcasted_iota(jnp.int32, dt.shape, 0) < valid, dt, 0.0)
    a = -jnp.exp(alog_ref[...])
    tril = jnp.where(causal, 1.0, 0.0).astype(BF16)
    cs = sum(jnp.dot(tril, part, preferred_element_type=F32) for part in _split3(dt * a))
    cs_t = cs.T
    dt_t = dt.T
    yield
    cs_w = _head_cols(cs)
    dt_w = _head_cols(dt)
    cs_last_w = cs_w[q - 1:q, :]
    decay_in = jnp.exp(cs_w)
    w_end = jnp.exp(cs_last_w - cs_w) * dt_w
    chunk_decay = jnp.exp(cs_last_w)
    yield

    lane = lax.broadcasted_iota(jnp.int32, (1, SSD_GROUP_W), 1)
    ys = []
    for g in range(SSD_GROUPS):
        b_g = xact[:, SSD_WIDTH + g * SSD_STATE:SSD_WIDTH + (g + 1) * SSD_STATE].astype(BF16)
        c_off = SSD_WIDTH + SSD_GROUPS * SSD_STATE
        c_g = xact[:, c_off + g * SSD_STATE:c_off + (g + 1) * SSD_STATE].astype(BF16)
        gs = slice(g * SSD_GROUP_W, (g + 1) * SSD_GROUP_W)
        x_g = xs_bf[:, gs]
        scores = lax.dot_general(c_g, b_g, (((1,), (1,)), ((), ())), preferred_element_type=F32)
        st_g = st_scr[:, gs]
        y_g = jnp.dot(c_g, st_g.astype(BF16), preferred_element_type=F32) * decay_in[:, gs]
        yield
        for j in range(SSD_HPG):
            h = g * SSD_HPG + j
            seg = cs[:, h:h + 1] - cs_t[h:h + 1, :]
            lmat = jnp.where(causal, jnp.exp(seg), 0.0) * dt_t[h:h + 1, :]
            m = (scores * lmat).astype(BF16)
            in_head = (lane >= j * SSD_HEAD_DIM) & (lane < (j + 1) * SSD_HEAD_DIM)
            x_h = jnp.where(in_head, x_g, jnp.zeros_like(x_g))
            y_g = y_g + jnp.dot(m, x_h, preferred_element_type=F32)
            yield
        ys.append(y_g)
        wx = (xs[:, gs] * w_end[:, gs]).astype(BF16)
        new = lax.dot_general(b_g, wx, (((0,), (0,)), ((), ())), preferred_element_type=F32)
        st_scr[:, gs] = st_g * chunk_decay[:, gs] + new
        yield
    y = jnp.concatenate(ys, axis=-1) + d_ref[...] * xs

    zf = z_ref[s, rows, :].astype(F32)
    gz = y * (zf * jax.nn.sigmoid(zf))
    yield
    nw = nw_ref[...]
    outs = []
    for g in range(SSD_GROUPS):
        gs = slice(g * SSD_GROUP_W, (g + 1) * SSD_GROUP_W)
        outs.append(_rms(gz[:, gs], nw[:, gs]))
    y_ref[s, rows, :] = jnp.concatenate(outs, axis=-1).astype(BF16)
    yield


def _ssd(z, xbc, dt, hist, h0, cw, cb, dtb, alog, dvec, nw, q, valid, nb, nc):
    bsz, length, _ = z.shape
    assert bsz % nb == 0 and length % (nc * q) == 0
    grid = (bsz // nb, length // (nc * q))
    tok = lambda w: pl.BlockSpec((nb, nc * q, w), lambda b, i: (b, i, 0))
    per_b = lambda r, w: pl.BlockSpec((nb, r, w), lambda b, i: (b, 0, 0))
    return pl.pallas_call(
        functools.partial(_ssd_kernel, q=q, valid=valid, nb=nb, nc=nc),
        grid=grid,
        in_specs=[tok(SSD_WIDTH), tok(SSD_CONV_DIM), tok(LANES), per_b(HIST_ROWS, SSD_CONV_DIM),
                  per_b(SSD_STATE, SSD_WIDTH), _const_spec(cw.shape), _const_spec(cb.shape), _const_spec(dtb.shape),
                  _const_spec(alog.shape), _const_spec(dvec.shape), _const_spec(nw.shape)],
        out_specs=[tok(SSD_WIDTH), per_b(SSD_STATE, SSD_WIDTH)],
        out_shape=[jax.ShapeDtypeStruct((bsz, length, SSD_WIDTH), BF16),
                   jax.ShapeDtypeStruct((bsz, SSD_STATE, SSD_WIDTH), F32)],
        scratch_shapes=[pltpu.VMEM((nb * SSD_CONV_DIM // LANES, q + HIST_ROWS, LANES), F32),
                        pltpu.VMEM((nb, SSD_STATE, SSD_WIDTH), F32)],
        compiler_params=_params(2),
        name="ssd",
    )(z, xbc, dt, hist, h0, cw, cb, dtb, alog, dvec, nw)


def _s5_kernel(u_ref, h0re_ref, h0im_ref, lr_ref, li_ref, bb_ref, cc_ref, d_ref, gw_ref, gb_ref,
               y_ref, hre_out_ref, him_out_ref, u_scr, y_scr, hs_scr, hre_scr, him_scr, *, tt, bsz):
    i = pl.program_id(0)

    @pl.when(i == 0)
    def _():
        hre_scr[...] = h0re_ref[...]
        him_scr[...] = h0im_ref[...]

    half = S5_BLOCK_STATE
    blocks = range(S5_BLOCKS)

    for b in range(bsz):
        ub = u_ref[b].astype(F32)
        for kb in blocks:
            u_scr[kb, pl.ds(b, tt, stride=bsz), :] = ub[:, kb * LANES:(kb + 1) * LANES]

    for kb in blocks:
        hs_scr[kb] = jnp.dot(u_scr[kb].astype(BF16), bb_ref[kb], preferred_element_type=F32)

    for kb0 in range(0, S5_BLOCKS, S5_SCAN_BLOCKS):
        kbs = range(kb0, kb0 + S5_SCAN_BLOCKS)
        lrs = [jnp.broadcast_to(lr_ref[kb], (bsz, half)) for kb in kbs]
        lis = [jnp.broadcast_to(li_ref[kb], (bsz, half)) for kb in kbs]

        def body(t, carry, kbs=kbs, lrs=lrs, lis=lis):
            r0 = pl.multiple_of(t * bsz, bsz)
            out = []
            for kb, lr, li, (hr, hi) in zip(kbs, lrs, lis, carry):
                nr = lr * hr - li * hi + hs_scr[kb, pl.ds(r0, bsz), 0:half]
                ni = lr * hi + li * hr + hs_scr[kb, pl.ds(r0, bsz), half:2 * half]
                hs_scr[kb, pl.ds(r0, bsz), 0:half] = nr
                hs_scr[kb, pl.ds(r0, bsz), half:2 * half] = ni
                out.append((nr, ni))
            return tuple(out)

        final = lax.fori_loop(0, tt, body, tuple((hre_scr[kb], him_scr[kb]) for kb in kbs), unroll=8)
        for kb, (hr, hi) in zip(kbs, final):
            hre_scr[kb] = hr
            him_scr[kb] = hi

    ys = [jnp.dot(hs_scr[kb].astype(BF16), cc_ref[kb], preferred_element_type=F32)
          + d_ref[:, kb * LANES:(kb + 1) * LANES] * u_scr[kb] for kb in blocks]
    ges = [_gelu_tanh(y).astype(BF16) for y in ys]
    gls = [jnp.dot(ges[kb], gw_ref[kb], preferred_element_type=F32) + gb_ref[kb] for kb in blocks]
    for kb in blocks:
        y_scr[kb] = gls[kb][:, :LANES] * jax.nn.sigmoid(gls[kb][:, LANES:])
    for b in range(bsz):
        y_ref[b] = jnp.concatenate([y_scr[kb, pl.ds(b, tt, stride=bsz), :] for kb in blocks], axis=-1).astype(BF16)

    @pl.when(i == pl.num_programs(0) - 1)
    def _():
        hre_out_ref[...] = hre_scr[...]
        him_out_ref[...] = him_scr[...]


def _s5(u, h0re, h0im, lr, li, bb, cc, dvec, gw, gb, tt):
    bsz, length, _ = u.shape
    grid = (length // tt,)
    st_shape = (S5_BLOCKS, bsz, S5_BLOCK_STATE)
    tok = pl.BlockSpec((bsz, tt, S5_WIDTH), lambda i: (0, i, 0))
    tb_scratch = pltpu.VMEM((S5_BLOCKS, tt * bsz, LANES), F32)
    return pl.pallas_call(
        functools.partial(_s5_kernel, tt=tt, bsz=bsz),
        grid=grid,
        in_specs=[tok, _const_spec(st_shape), _const_spec(st_shape), _const_spec(lr.shape), _const_spec(li.shape),
                  _const_spec(bb.shape), _const_spec(cc.shape), _const_spec(dvec.shape), _const_spec(gw.shape),
                  _const_spec(gb.shape)],
        out_specs=[tok, pl.BlockSpec(st_shape, lambda i: (0, 0, 0)), pl.BlockSpec(st_shape, lambda i: (0, 0, 0))],
        out_shape=[jax.ShapeDtypeStruct(u.shape, BF16),
                   jax.ShapeDtypeStruct(st_shape, F32), jax.ShapeDtypeStruct(st_shape, F32)],
        scratch_shapes=[tb_scratch, tb_scratch, pltpu.VMEM((S5_BLOCKS, tt * bsz, 2 * S5_BLOCK_STATE), F32),
                        pltpu.VMEM(st_shape, F32), pltpu.VMEM(st_shape, F32)],
        compiler_params=_params(1),
        name="s5",
    )(u, h0re, h0im, lr, li, bb, cc, dvec, gw, gb)


def _ffn_kernel(x_ref, ys_ref, y5_ref, hist_ref, wout_ref, n1_ref, n2_ref, wup_ref, cw_ref, cb_ref, wdn_ref, n3_ref,
                y_ref, carry_ref, perm_scr, act_scr, *, tm):
    i = pl.program_id(1)
    b = 0
    nj = tm // SUBLANES
    grp = SUBLANES

    @pl.when(i == 0)
    def _():
        carry_ref[b] = hist_ref[b]

    mix = (jnp.dot(ys_ref[0], wout_ref[0:SSD_WIDTH, :], preferred_element_type=F32)
           + jnp.dot(y5_ref[0], wout_ref[SSD_WIDTH:, :], preferred_element_type=F32))
    h = x_ref[0] + _rms(mix, n1_ref[...])
    hn = _rms(h, n2_ref[...])

    n_slab = D_MODEL // LANES
    for s in range(n_slab):
        for seg in range(grp):
            perm_scr[s, pl.ds(seg, nj, stride=grp), :] = hn[seg * nj:(seg + 1) * nj, s * LANES:(s + 1) * LANES]
    hp = jnp.concatenate([perm_scr[s] for s in range(n_slab)], axis=-1).astype(BF16)

    first_sublane = lax.broadcasted_iota(jnp.int32, (grp, FFN_CHUNK), 0) == 0

    def conv_chunk(c0):
        cols = slice(c0, c0 + FFN_CHUNK)
        up = jnp.dot(hp, wup_ref[:, cols], preferred_element_type=F32)

        def wrapped(group, fill_row):
            fill = jnp.broadcast_to(carry_ref[b, fill_row:fill_row + 1, cols], (grp, FFN_CHUNK))
            return jnp.where(first_sublane, fill, pltpu.roll(group, 1, 0))

        s2 = wrapped(up[tm - 2 * grp:tm - grp], grp - 1)
        s1 = wrapped(up[tm - grp:tm], 2 * grp - 1)
        carry_ref[b, :, cols] = up[tm - 2 * grp:tm]
        prev1 = jnp.concatenate([s1, up[:tm - grp]], axis=0)
        prev2 = jnp.concatenate([s2, s1, up[:tm - 2 * grp]], axis=0)
        return (cb_ref[:, cols] + up * cw_ref[2:3, cols] + prev1 * cw_ref[1:2, cols] + prev2 * cw_ref[0:1, cols])

    for c in range(D_FF // FFN_CHUNK):
        gate = conv_chunk(c * FFN_CHUNK)
        val = conv_chunk(D_FF + c * FFN_CHUNK)
        act_scr[:, c * FFN_CHUNK:(c + 1) * FFN_CHUNK] = (_gelu_tanh(gate) * val).astype(BF16)

    ffn = _rms(jnp.dot(act_scr[...], wdn_ref[...], preferred_element_type=F32), n3_ref[...])
    for s in range(n_slab):
        perm_scr[s] = ffn[:, s * LANES:(s + 1) * LANES]
    ffn = jnp.concatenate(
        [jnp.concatenate([perm_scr[s, pl.ds(seg, nj, stride=grp), :] for seg in range(grp)], axis=0)
         for s in range(n_slab)], axis=-1)
    y_ref[0] = h + ffn


def _ffn(x, ys, y5_tb, hist, wout, n1, n2, wup, cw, cb, wdn, n3, tm):
    bsz, length, _ = x.shape
    assert FFN_CONV == 3 and tm % SUBLANES == 0 and tm // SUBLANES >= 2
    grid = (bsz, length // tm)
    tok = lambda w: pl.BlockSpec((1, tm, w), lambda b, i: (b, i, 0))
    per_b = pl.BlockSpec((1,) + hist.shape[1:], lambda b, i: (b, 0, 0))
    return pl.pallas_call(
        functools.partial(_ffn_kernel, tm=tm),
        grid=grid,
        in_specs=[tok(D_MODEL), tok(SSD_WIDTH), tok(S5_WIDTH), per_b,
                  _const_spec(wout.shape), _const_spec(n1.shape), _const_spec(n2.shape), _const_spec(wup.shape),
                  _const_spec(cw.shape), _const_spec(cb.shape), _const_spec(wdn.shape), _const_spec(n3.shape)],
        out_specs=[tok(D_MODEL), per_b],
        out_shape=[jax.ShapeDtypeStruct((bsz, length, D_MODEL), F32), jax.ShapeDtypeStruct(hist.shape, F32)],
        scratch_shapes=[pltpu.VMEM((D_MODEL // LANES, tm, LANES), F32), pltpu.VMEM((tm, D_FF), BF16)],
        compiler_params=_params(2),
        name="outffn",
    )(x, ys, y5_tb, hist, wout, n1, n2, wup, cw, cb, wdn, n3)


def _block_diag(w):
    _, r, c = w.shape
    w4 = w.reshape(S5_BLOCKS, S5_GPB, r, c)
    eye = jnp.eye(S5_GPB, dtype=w.dtype)
    return jnp.einsum("kgrc,gh->kgrhc", w4, eye).reshape(S5_BLOCKS, S5_GPB * r, S5_GPB * c)


def _prep_weights(pre_mix_norm_w, w_in, ssd_conv_w, ssd_conv_b, ssd_dt_bias, ssd_a_log, ssd_d, ssd_norm_w,
                  s5_lambda_re, s5_lambda_im, s5_log_dt, s5_b_re, s5_b_im, s5_c_re, s5_c_im, s5_d,
                  s5_glu_w, s5_glu_b, w_out, post_mix_norm_w, pre_ffn_norm_w, w_up, ffn_conv_w, ffn_conv_b,
                  w_down, post_ffn_norm_w):
    o_dt = SSD_WIDTH + SSD_CONV_DIM
    o_u = o_dt + SSD_HEADS
    pad_heads = lambda v: jnp.pad(v, (0, LANES - SSD_HEADS)).reshape(1, LANES)
    w = dict(
        nw0=pre_mix_norm_w.reshape(1, D_MODEL),
        wa=w_in[:, :o_dt].astype(BF16),
        wdt=jnp.pad(w_in[:, o_dt:o_u], ((0, 0), (0, LANES - SSD_HEADS))).astype(BF16),
        wu=w_in[:, o_u:].astype(BF16),
        cw=ssd_conv_w, cb=ssd_conv_b.reshape(1, SSD_CONV_DIM),
        dtb=pad_heads(ssd_dt_bias), alog=pad_heads(ssd_a_log),
        dssd=jnp.repeat(ssd_d, SSD_HEAD_DIM).reshape(1, SSD_WIDTH),
        nssd=ssd_norm_w.reshape(1, SSD_WIDTH),
        wout=w_out.astype(BF16), n1=post_mix_norm_w.reshape(1, D_MODEL), n2=pre_ffn_norm_w.reshape(1, D_MODEL),
        wup=w_up.astype(BF16), fcw=ffn_conv_w, fcb=ffn_conv_b.reshape(1, 2 * D_FF),
        wdn=w_down.astype(BF16), n3=post_ffn_norm_w.reshape(1, D_MODEL),
    )
    dt = jnp.exp(s5_log_dt)[:, None]
    mag = jnp.exp(s5_lambda_re * dt)
    ang = s5_lambda_im * dt
    lb_re = mag * jnp.cos(ang)
    lb_im = mag * jnp.sin(ang)
    den = s5_lambda_re * s5_lambda_re + s5_lambda_im * s5_lambda_im
    q_re = ((lb_re - 1) * s5_lambda_re + lb_im * s5_lambda_im) / den
    q_im = (lb_im * s5_lambda_re - (lb_re - 1) * s5_lambda_im) / den
    bb_re = q_re[..., None] * s5_b_re - q_im[..., None] * s5_b_im
    bb_im = q_re[..., None] * s5_b_im + q_im[..., None] * s5_b_re
    to_in = lambda m: _block_diag(jnp.swapaxes(m, 1, 2))
    w["bb"] = jnp.concatenate([to_in(bb_re), to_in(bb_im)], axis=-1).astype(BF16)
    to_out = lambda m: _block_diag(jnp.swapaxes(m, 1, 2))
    w["cc"] = jnp.concatenate([to_out(s5_c_re), to_out(-s5_c_im)], axis=1).astype(BF16)
    w["lr"] = lb_re.reshape(S5_BLOCKS, 1, S5_BLOCK_STATE)
    w["li"] = lb_im.reshape(S5_BLOCKS, 1, S5_BLOCK_STATE)
    w["d5"] = s5_d.reshape(1, S5_WIDTH)
    w["gw"] = jnp.concatenate([_block_diag(s5_glu_w[..., :S5_GROUP_CH]), _block_diag(s5_glu_w[..., S5_GROUP_CH:])],
                              axis=-1).astype(BF16)
    gb = lambda v: v.reshape(S5_BLOCKS, 1, LANES)
    w["gb"] = jnp.concatenate([gb(s5_glu_b[:, :S5_GROUP_CH]), gb(s5_glu_b[:, S5_GROUP_CH:])], axis=-1)
    return w


def _hist_tile(hist):
    return jnp.pad(hist, ((0, 0), (HIST_ROWS - hist.shape[1], 0), (0, 0)))


def _layer(x, conv_hist, ssd_h0, s5_re, s5_im, ffn_hist, w, *, tm_in, tm, q, tt, nb, nc):
    bsz, length, _ = x.shape
    z, xbc, dt, u = _inproj(x, w["nw0"], w["wa"], w["wdt"], w["wu"], tm_in)
    new_conv = xbc[:, length - (SSD_CONV - 1):, :].astype(F32)

    lpad = -length % q
    if lpad:
        assert length < q, "a padded sequence must fit one SSD chunk"
        padt = lambda a: jnp.pad(a, ((0, 0), (0, lpad), (0, 0)))
        z, xbc, dt = padt(z), padt(xbc), padt(dt)
    h0 = jnp.transpose(ssd_h0, (0, 3, 1, 2)).reshape(bsz, SSD_STATE, SSD_WIDTH)
    y_ssd, st = _ssd(z, xbc, dt, _hist_tile(conv_hist), h0, w["cw"], w["cb"], w["dtb"], w["alog"], w["dssd"],
                     w["nssd"], q, length if lpad else q, nb, nc)
    y_ssd = y_ssd[:, :length]
    new_ssd = jnp.transpose(st.reshape(bsz, SSD_STATE, SSD_HEADS, SSD_HEAD_DIM), (0, 2, 3, 1))

    to_blocks = lambda s: jnp.transpose(s.reshape(bsz, S5_BLOCKS, S5_BLOCK_STATE), (1, 0, 2))
    from_blocks = lambda s: jnp.transpose(s, (1, 0, 2)).reshape(bsz, S5_GROUPS, S5_STATE)
    y5, hre, him = _s5(u, to_blocks(s5_re), to_blocks(s5_im), w["lr"], w["li"], w["bb"], w["cc"], w["d5"],
                       w["gw"], w["gb"], tt)

    last = SUBLANES - 1
    hist = jnp.zeros((bsz, 2 * SUBLANES, 2 * D_FF), F32).at[:, last::SUBLANES, :].set(ffn_hist)
    y, carry = _ffn(x, y_ssd, y5, hist, w["wout"], w["n1"], w["n2"], w["wup"], w["fcw"], w["fcb"],
                    w["wdn"], w["n3"], tm)
    new_ffn = carry[:, last::SUBLANES, :]
    return y, new_conv, new_ssd, from_blocks(hre), from_blocks(him), new_ffn


def _tiles(length):
    tm_in = min(length, 1024)
    tm = min(length, 512)
    q = 128
    nc = max(1, min(SSD_CHUNKS_PER_STEP, length // q))
    tt = min(length, 128)
    assert length % tm_in == 0 and length % tm == 0 and length % tt == 0 and length >= SSD_CONV - 1
    return dict(tm_in=tm_in, tm=tm, q=q, tt=tt, nb=SSD_SEQS_PER_STEP, nc=nc)


def kernel(x_prompt, x_sample, cache_ssd_conv, state_ssd, state_s5_re, state_s5_im, cache_ffn_conv, pre_mix_norm_w, w_in, ssd_conv_w, ssd_conv_b, ssd_dt_bias, ssd_a_log, ssd_d, ssd_norm_w, s5_lambda_re, s5_lambda_im, s5_log_dt, s5_b_re, s5_b_im, s5_c_re, s5_c_im, s5_d, s5_glu_w, s5_glu_b, w_out, post_mix_norm_w, pre_ffn_norm_w, w_up, ffn_conv_w, ffn_conv_b, w_down, post_ffn_norm_w):
    depth = w_in.shape[0]
    bsz = x_prompt.shape[0]
    dtp = x_prompt.dtype
    layer_params = (pre_mix_norm_w, w_in, ssd_conv_w, ssd_conv_b, ssd_dt_bias, ssd_a_log, ssd_d, ssd_norm_w,
                    s5_lambda_re, s5_lambda_im, s5_log_dt, s5_b_re, s5_b_im, s5_c_re, s5_c_im, s5_d,
                    s5_glu_w, s5_glu_b, w_out, post_mix_norm_w, pre_ffn_norm_w, w_up, ffn_conv_w, ffn_conv_b,
                    w_down, post_ffn_norm_w)
    y_prompt, y_sample = x_prompt, x_sample
    prompt_states, sample_states = [], []
    for l in range(depth):
        w = _prep_weights(*(p[l] for p in layer_params))
        y_prompt, *ps = _layer(
            y_prompt,
            jnp.zeros((bsz, SSD_CONV - 1, SSD_CONV_DIM), dtp),
            jnp.zeros((bsz, SSD_HEADS, SSD_HEAD_DIM, SSD_STATE), dtp),
            jnp.zeros((bsz, S5_GROUPS, S5_STATE), dtp),
            jnp.zeros((bsz, S5_GROUPS, S5_STATE), dtp),
            jnp.zeros((bsz, FFN_CONV - 1, 2 * D_FF), dtp),
            w, **_tiles(y_prompt.shape[1]))
        y_sample, *ss = _layer(
            y_sample, cache_ssd_conv[l], state_ssd[l], state_s5_re[l], state_s5_im[l], cache_ffn_conv[l],
            w, **_tiles(y_sample.shape[1]))
        prompt_states.append(ps)
        sample_states.append(ss)
    stack = lambda states: tuple(jnp.stack([s[k] for s in states]) for k in range(5))
    return (y_prompt, y_sample) + stack(prompt_states) + stack(sample_states)
```

```python
import functools
import math
from typing import Callable, NamedTuple

import jax
import jax.numpy as jnp
from jax import lax
from jax.experimental import pallas as pl
from jax.experimental.pallas import tpu as pltpu

F32 = jnp.float32
BF16 = jnp.bfloat16

D_MODEL = 1024
SSD_WIDTH = 512
SSD_HEAD_DIM = 64
SSD_HEADS = 8
SSD_GROUPS = 2
SSD_HPG = SSD_HEADS // SSD_GROUPS
SSD_GROUP_W = SSD_WIDTH // SSD_GROUPS
SSD_STATE = 128
SSD_CONV = 4
SSD_CONV_DIM = SSD_WIDTH + 2 * SSD_GROUPS * SSD_STATE
S5_WIDTH = 512
S5_GROUP_CH = 16
S5_GROUPS = 32
S5_STATE = 64
S5_BLOCKS = 4
S5_GPB = S5_GROUPS // S5_BLOCKS
S5_BLOCK_STATE = S5_GPB * S5_STATE
S5_SCAN_BLOCKS = 2
D_FF = 2816
FFN_CONV = 3
EPS = 1e-6

LANES = 128
SUBLANES = 8
HIST_ROWS = SUBLANES
VMEM_LIMIT = 56 * 1024 * 1024

FFN_CHUNK = 256
SSD_SEQS_PER_STEP = 2
SSD_CHUNKS_PER_STEP = 4


def _rms(x, w):
    ms = jnp.mean(x * x, axis=-1, keepdims=True)
    return x * lax.rsqrt(ms + EPS) * w


def _gelu_tanh(x):
    c = math.sqrt(2.0 / math.pi)
    hx = 0.5 * x
    return hx + hx * jnp.tanh(x * (c + (c * 0.044715) * (x * x)))


def _softplus(x):
    return jnp.maximum(x, 0.0) + jnp.log1p(jnp.exp(-jnp.abs(x)))


def _causal_conv(ext, cur, w_ref, b_ref, c0, rows, taps):
    cols = slice(c0, c0 + LANES)
    ext[HIST_ROWS:HIST_ROWS + rows, :] = cur
    out = b_ref[:, cols] + cur * w_ref[taps - 1:taps, cols]
    for k in range(1, taps):
        out = out + ext[pl.ds(HIST_ROWS - k, rows), :] * w_ref[taps - 1 - k:taps - k, cols]
    ext[0:HIST_ROWS, :] = ext[rows:rows + HIST_ROWS, :]
    return out


def _const_spec(shape):
    nd = len(shape)
    return pl.BlockSpec(shape, lambda *_: (0,) * nd, pipeline_mode=pl.Buffered(1))


def _params(n_axes):
    return pltpu.CompilerParams(dimension_semantics=("arbitrary",) * n_axes, vmem_limit_bytes=VMEM_LIMIT)


def _inproj_kernel(x_ref, nw_ref, wa_ref, wdt_ref, wu_ref, z_ref, xbc_ref, dt_ref, u_ref):
    x = x_ref[0]
    xn = _rms(x, nw_ref[...]).astype(BF16)
    a = jnp.dot(xn, wa_ref[...], preferred_element_type=F32)
    z_ref[0] = a[:, :SSD_WIDTH].astype(BF16)
    xbc_ref[0] = a[:, SSD_WIDTH:].astype(BF16)
    dt_ref[0] = jnp.dot(xn, wdt_ref[...], preferred_element_type=F32)
    u_ref[0] = jnp.dot(xn, wu_ref[...], preferred_element_type=F32).astype(BF16)


def _inproj(x, nw, wa, wdt, wu, tm):
    bsz, length, _ = x.shape
    grid = (bsz, length // tm)
    tok = lambda w, dt: (jax.ShapeDtypeStruct((bsz, length, w), dt), pl.BlockSpec((1, tm, w), lambda b, i: (b, i, 0)))
    outs = [tok(SSD_WIDTH, BF16), tok(SSD_CONV_DIM, BF16), tok(LANES, F32), tok(S5_WIDTH, BF16)]
    return pl.pallas_call(
        _inproj_kernel,
        grid=grid,
        in_specs=[pl.BlockSpec((1, tm, D_MODEL), lambda b, i: (b, i, 0)),
                  _const_spec(nw.shape), _const_spec(wa.shape), _const_spec(wdt.shape), _const_spec(wu.shape)],
        out_specs=[o[1] for o in outs],
        out_shape=[o[0] for o in outs],
        compiler_params=_params(2),
        name="inproj",
    )(x, nw, wa, wdt, wu)


def _split3(v):
    hi = v.astype(BF16)
    r1 = v - hi.astype(F32)
    mid = r1.astype(BF16)
    lo = (r1 - mid.astype(F32)).astype(BF16)
    return hi, mid, lo


def _head_tiles(a):
    q = a.shape[0]
    lane = lax.broadcasted_iota(jnp.int32, (q, LANES), 1)
    tiles = []
    for j in range(SSD_HEADS * SSD_HEAD_DIM // LANES):
        even = jnp.broadcast_to(a[:, 2 * j:2 * j + 1], (q, LANES))
        odd = jnp.broadcast_to(a[:, 2 * j + 1:2 * j + 2], (q, LANES))
        tiles.append(jnp.where(lane < SSD_HEAD_DIM, even, odd))
    return tiles


def _ssd_kernel(z_ref, xbc_ref, dt_ref, hist_ref, h0_ref, cw_ref, cb_ref, dtb_ref, alog_ref, d_ref, nw_ref,
                y_ref, st_out_ref, ext_scr, st_scr, *, q, valid, nb, nc):
    t = pl.program_id(1)
    n_slab = SSD_CONV_DIM // LANES

    @pl.when(t == 0)
    def _():
        for s in range(nb):
            for j in range(n_slab):
                ext_scr[s * n_slab + j, 0:HIST_ROWS, :] = hist_ref[s, :, j * LANES:(j + 1) * LANES]
        st_scr[...] = h0_ref[...]

    def io(s):
        def put_y(rows, v):
            y_ref[s, rows, :] = v
        return SsdIO(z=lambda rows: z_ref[s, rows, :], xbc=lambda rows: xbc_ref[s, rows, :],
                     dt=lambda rows: dt_ref[s, rows, :], put_y=put_y)

    for c in range(nc):
        chains = [_ssd_chunk(s, slice(c * q, (c + 1) * q), io(s), cw_ref, cb_ref, dtb_ref, alog_ref,
                             d_ref, nw_ref, ext_scr, st_scr, q=q, valid=valid) for s in range(nb)]
        for _ in zip(*chains):
            pass

    @pl.when(t == pl.num_programs(1) - 1)
    def _():
        st_out_ref[...] = st_scr[...]


class SsdIO(NamedTuple):
    z: Callable
    xbc: Callable
    dt: Callable
    put_y: Callable


def _ssd_chunk(s, rows, io, cw_ref, cb_ref, dtb_ref, alog_ref, d_ref, nw_ref, ext_scr, st_scr, *, q, valid):
    n_slab = SSD_CONV_DIM // LANES
    st_scr = st_scr.at[s]
    xraw = io.xbc(rows).astype(F32)
    slabs = []
    for j in range(n_slab):
        conv = _causal_conv(ext_scr.at[s * n_slab + j], xraw[:, j * LANES:(j + 1) * LANES], cw_ref, cb_ref,
                            j * LANES, q, SSD_CONV)
        slabs.append(conv * jax.nn.sigmoid(conv))
        if j % 2 == 1:
            yield
    xact = jnp.concatenate(slabs, axis=-1)
    xs = xact[:, :SSD_WIDTH]
    xs_bf = xs.astype(BF16)
    yield

    dt = _softplus(io.dt(rows) + dtb_ref[...])
    row = lax.broadcasted_iota(jnp.int32, (q, q), 0)
    col = lax.broadcasted_iota(jnp.int32, (q, q), 1)
    causal = row >= col
    if valid < q:
        dt = jnp.where(lax.broadcasted_iota(jnp.int32, dt.shape, 0) < valid, dt, 0.0)
    a = -jnp.exp(alog_ref[...])
    tril = jnp.where(causal, 1.0, 0.0).astype(BF16)
    cs = sum(jnp.dot(tril, part, preferred_element_type=F32) for part in _split3(dt * a))
    cs_t = cs.T
    dt_t = dt.T
    yield
    cs_tiles = _head_tiles(cs)
    dt_tiles = _head_tiles(dt)
    cat = lambda tiles: jnp.concatenate(tiles, axis=-1)
    decay_in = cat([jnp.exp(c) for c in cs_tiles])
    w_end = cat([jnp.exp(c[q - 1:q, :] - c) * d for c, d in zip(cs_tiles, dt_tiles)])
    chunk_decay = cat([jnp.exp(c[q - 1:q, :]) for c in cs_tiles])
    yield

    lane = lax.broadcasted_iota(jnp.int32, (1, SSD_GROUP_W), 1)
    ys = []
    for g in range(SSD_GROUPS):
        b_g = xact[:, SSD_WIDTH + g * SSD_STATE:SSD_WIDTH + (g + 1) * SSD_STATE].astype(BF16)
        c_off = SSD_WIDTH + SSD_GROUPS * SSD_STATE
        c_g = xact[:, c_off + g * SSD_STATE:c_off + (g + 1) * SSD_STATE].astype(BF16)
        gs = slice(g * SSD_GROUP_W, (g + 1) * SSD_GROUP_W)
        x_g = xs_bf[:, gs]
        scores = lax.dot_general(c_g, b_g, (((1,), (1,)), ((), ())), preferred_element_type=F32)
        st_g = st_scr[:, gs]
        y_g = jnp.dot(c_g, st_g.astype(BF16), preferred_element_type=F32) * decay_in[:, gs]
        yield
        for j in range(SSD_HPG):
            h = g * SSD_HPG + j
            seg = cs[:, h:h + 1] - cs_t[h:h + 1, :]
            lmat = jnp.where(causal, jnp.exp(seg), 0.0) * dt_t[h:h + 1, :]
            m = (scores * lmat).astype(BF16)
            in_head = (lane >= j * SSD_HEAD_DIM) & (lane < (j + 1) * SSD_HEAD_DIM)
            x_h = jnp.where(in_head, x_g, jnp.zeros_like(x_g))
            y_g = y_g + jnp.dot(m, x_h, preferred_element_type=F32)
            yield
        ys.append(y_g)
        wx = (xs[:, gs] * w_end[:, gs]).astype(BF16)
        new = lax.dot_general(b_g, wx, (((0,), (0,)), ((), ())), preferred_element_type=F32)
        st_scr[:, gs] = st_g * chunk_decay[:, gs] + new
        yield
    y = jnp.concatenate(ys, axis=-1) + d_ref[...] * xs

    zf = io.z(rows).astype(F32)
    gz = y * (zf * jax.nn.sigmoid(zf))
    yield
    nw = nw_ref[...]
    outs = []
    for g in range(SSD_GROUPS):
        gs = slice(g * SSD_GROUP_W, (g + 1) * SSD_GROUP_W)
        outs.append(_rms(gz[:, gs], nw[:, gs]))
    io.put_y(rows, jnp.concatenate(outs, axis=-1).astype(BF16))
    yield


def _ssd(z, xbc, dt, hist, h0, cw, cb, dtb, alog, dvec, nw, q, valid, nb, nc):
    bsz, length, _ = z.shape
    assert bsz % nb == 0 and length % (nc * q) == 0
    grid = (bsz // nb, length // (nc * q))
    tok = lambda w: pl.BlockSpec((nb, nc * q, w), lambda b, i: (b, i, 0))
    per_b = lambda r, w: pl.BlockSpec((nb, r, w), lambda b, i: (b, 0, 0))
    return pl.pallas_call(
        functools.partial(_ssd_kernel, q=q, valid=valid, nb=nb, nc=nc),
        grid=grid,
        in_specs=[tok(SSD_WIDTH), tok(SSD_CONV_DIM), tok(LANES), per_b(HIST_ROWS, SSD_CONV_DIM),
                  per_b(SSD_STATE, SSD_WIDTH), _const_spec(cw.shape), _const_spec(cb.shape), _const_spec(dtb.shape),
                  _const_spec(alog.shape), _const_spec(dvec.shape), _const_spec(nw.shape)],
        out_specs=[tok(SSD_WIDTH), per_b(SSD_STATE, SSD_WIDTH)],
        out_shape=[jax.ShapeDtypeStruct((bsz, length, SSD_WIDTH), BF16),
                   jax.ShapeDtypeStruct((bsz, SSD_STATE, SSD_WIDTH), F32)],
        scratch_shapes=[pltpu.VMEM((nb * SSD_CONV_DIM // LANES, q + HIST_ROWS, LANES), F32),
                        pltpu.VMEM((nb, SSD_STATE, SSD_WIDTH), F32)],
        compiler_params=_params(2),
        name="ssd",
    )(z, xbc, dt, hist, h0, cw, cb, dtb, alog, dvec, nw)


MIX_COLS = 256
MIX_PHASES_PER_PIECE = 3


def _inproj_phases(x_ref, nw_ref, wa_ref, wdt_ref, wu_ref, xn_scr, zx_scr, dtp_scr, u_ref, slot, *, nb, tm):
    for j in range(nb):
        xn_scr[j * tm:(j + 1) * tm, :] = _rms(x_ref[j], nw_ref[...]).astype(BF16)
        yield
    for c0 in range(0, SSD_WIDTH + SSD_CONV_DIM, MIX_COLS):
        cols = slice(c0, c0 + MIX_COLS)
        for j in range(nb):
            rows = slice(j * tm, (j + 1) * tm)
            zx_scr[slot, rows, cols] = jnp.dot(xn_scr[rows, :], wa_ref[:, cols], preferred_element_type=F32)
            yield
    for j in range(nb):
        rows = slice(j * tm, (j + 1) * tm)
        dtp_scr[slot, rows, :] = jnp.dot(xn_scr[rows, :], wdt_ref[...], preferred_element_type=F32)
        yield
    for c0 in range(0, S5_WIDTH, MIX_COLS):
        cols = slice(c0, c0 + MIX_COLS)
        for j in range(nb):
            rows = slice(j * tm, (j + 1) * tm)
            u_ref[j, :, cols] = jnp.dot(xn_scr[rows, :], wu_ref[:, cols], preferred_element_type=F32).astype(BF16)
            yield


def _mixer_kernel(x_ref, hist_ref, h0_ref, nw0_ref, wa_ref, wdt_ref, wu_ref, cw_ref, cb_ref, dtb_ref, alog_ref, d_ref,
                  nw_ref, u_ref, y_ref, tail_ref, st_out_ref, xn_scr, zx_scr, dtp_scr, ext_scr, st_scr,
                  *, tm, q, nb, nt):
    s = pl.program_id(0)
    prev = s - 1
    wslot = lax.rem(s, 2)
    rslot = 1 - wslot
    n_slab = SSD_CONV_DIM // LANES
    seq_pos = lax.rem(prev + nt, nt)

    @pl.when(s == 0)
    def _():
        zx_scr[1] = jnp.zeros(zx_scr.shape[1:], F32)
        dtp_scr[1] = jnp.zeros(dtp_scr.shape[1:], F32)
        ext_scr[...] = jnp.zeros(ext_scr.shape, F32)
        st_scr[...] = jnp.zeros(st_scr.shape, F32)

    @pl.when((seq_pos == 0) & (s > 0))
    def _():
        for j in range(nb):
            for k in range(n_slab):
                ext_scr[j * n_slab + k, 0:HIST_ROWS, :] = hist_ref[j, :, k * LANES:(k + 1) * LANES]
        st_scr[...] = h0_ref[...]

    def io(j):
        base = j * tm
        at = lambda rows: pl.ds(base + rows.start, rows.stop - rows.start)

        def put_y(rows, v):
            y_ref[j, rows, :] = v
        return SsdIO(z=lambda rows: zx_scr[rslot, at(rows), 0:SSD_WIDTH],
                     xbc=lambda rows: zx_scr[rslot, at(rows), SSD_WIDTH:SSD_WIDTH + SSD_CONV_DIM],
                     dt=lambda rows: dtp_scr[rslot, at(rows), :], put_y=put_y)

    inproj = _inproj_phases(x_ref, nw0_ref, wa_ref, wdt_ref, wu_ref, xn_scr, zx_scr, dtp_scr, u_ref, wslot,
                            nb=nb, tm=tm)
    next(inproj)
    phase = 0
    for c in range(tm // q):
        chains = [_ssd_chunk(j, slice(c * q, (c + 1) * q), io(j), cw_ref, cb_ref, dtb_ref, alog_ref, d_ref, nw_ref,
                             ext_scr, st_scr, q=q, valid=q) for j in range(nb)]
        for _ in zip(*chains):
            phase += 1
            if phase % MIX_PHASES_PER_PIECE == 0:
                next(inproj, None)
    for _ in inproj:
        pass

    @pl.when((seq_pos == nt - 1) & (s > 0))
    def _():
        st_out_ref[...] = st_scr[...]
        for j in range(nb):
            for k in range(n_slab):
                tail_ref[j, :, k * LANES:(k + 1) * LANES] = ext_scr[j * n_slab + k, 0:HIST_ROWS, :]


def _mixer(x, hist, h0, nw0, wa, wdt, wu, cw, cb, dtb, alog, dvec, nw, tm, q, nb):
    bsz, length, _ = x.shape
    assert bsz % nb == 0 and length % tm == 0 and tm % q == 0
    nt = length // tm
    n_tiles = (bsz // nb) * nt
    cur = lambda s: jnp.minimum(s, n_tiles - 1)
    prv = lambda s: jnp.maximum(s - 1, 0)
    tile = lambda w, which: pl.BlockSpec((nb, tm, w), lambda s: (which(s) // nt, which(s) % nt, 0))
    per_seq = lambda r, w: pl.BlockSpec((nb, r, w), lambda s: (prv(s) // nt, 0, 0))
    rows = nb * tm
    return pl.pallas_call(
        functools.partial(_mixer_kernel, tm=tm, q=q, nb=nb, nt=nt),
        grid=(n_tiles + 1,),
        in_specs=[tile(D_MODEL, cur), per_seq(HIST_ROWS, SSD_CONV_DIM), per_seq(SSD_STATE, SSD_WIDTH),
                  _const_spec(nw0.shape), _const_spec(wa.shape), _const_spec(wdt.shape), _const_spec(wu.shape),
                  _const_spec(cw.shape), _const_spec(cb.shape), _const_spec(dtb.shape), _const_spec(alog.shape),
                  _const_spec(dvec.shape), _const_spec(nw.shape)],
        out_specs=[tile(S5_WIDTH, cur), tile(SSD_WIDTH, prv), per_seq(HIST_ROWS, SSD_CONV_DIM),
                   per_seq(SSD_STATE, SSD_WIDTH)],
        out_shape=[jax.ShapeDtypeStruct((bsz, length, S5_WIDTH), BF16),
                   jax.ShapeDtypeStruct((bsz, length, SSD_WIDTH), BF16),
                   jax.ShapeDtypeStruct((bsz, HIST_ROWS, SSD_CONV_DIM), F32),
                   jax.ShapeDtypeStruct((bsz, SSD_STATE, SSD_WIDTH), F32)],
        scratch_shapes=[pltpu.VMEM((rows, D_MODEL), BF16),
                        pltpu.VMEM((2, rows, SSD_WIDTH + SSD_CONV_DIM), F32),
                        pltpu.VMEM((2, rows, LANES), F32),
                        pltpu.VMEM((nb * SSD_CONV_DIM // LANES, q + HIST_ROWS, LANES), F32),
                        pltpu.VMEM((nb, SSD_STATE, SSD_WIDTH), F32)],
        compiler_params=_params(1),
        name="mixer",
    )(x, hist, h0, nw0, wa, wdt, wu, cw, cb, dtb, alog, dvec, nw)


def _s5_kernel(u_ref, h0re_ref, h0im_ref, lr_ref, li_ref, bb_ref, cc_ref, d_ref, gw_ref, gb_ref,
               y_ref, hre_out_ref, him_out_ref, u_scr, y_scr, hs_scr, hre_scr, him_scr, *, tt, bsz):
    i = pl.program_id(0)

    @pl.when(i == 0)
    def _():
        hre_scr[...] = h0re_ref[...]
        him_scr[...] = h0im_ref[...]

    half = S5_BLOCK_STATE
    blocks = range(S5_BLOCKS)

    for b in range(bsz):
        ub = u_ref[b].astype(F32)
        for kb in blocks:
            u_scr[kb, pl.ds(b, tt, stride=bsz), :] = ub[:, kb * LANES:(kb + 1) * LANES]

    for kb in blocks:
        hs_scr[kb] = jnp.dot(u_scr[kb].astype(BF16), bb_ref[kb], preferred_element_type=F32)

    for kb0 in range(0, S5_BLOCKS, S5_SCAN_BLOCKS):
        kbs = range(kb0, kb0 + S5_SCAN_BLOCKS)
        lrs = [jnp.broadcast_to(lr_ref[kb], (bsz, half)) for kb in kbs]
        lis = [jnp.broadcast_to(li_ref[kb], (bsz, half)) for kb in kbs]

        def body(t, carry, kbs=kbs, lrs=lrs, lis=lis):
            r0 = pl.multiple_of(t * bsz, bsz)
            out = []
            for kb, lr, li, (hr, hi) in zip(kbs, lrs, lis, carry):
                nr = lr * hr - li * hi + hs_scr[kb, pl.ds(r0, bsz), 0:half]
                ni = lr * hi + li * hr + hs_scr[kb, pl.ds(r0, bsz), half:2 * half]
                hs_scr[kb, pl.ds(r0, bsz), 0:half] = nr
                hs_scr[kb, pl.ds(r0, bsz), half:2 * half] = ni
                out.append((nr, ni))
            return tuple(out)

        final = lax.fori_loop(0, tt, body, tuple((hre_scr[kb], him_scr[kb]) for kb in kbs), unroll=8)
        for kb, (hr, hi) in zip(kbs, final):
            hre_scr[kb] = hr
            him_scr[kb] = hi

    ys = [jnp.dot(hs_scr[kb].astype(BF16), cc_ref[kb], preferred_element_type=F32)
          + d_ref[:, kb * LANES:(kb + 1) * LANES] * u_scr[kb] for kb in blocks]
    ges = [_gelu_tanh(y).astype(BF16) for y in ys]
    gls = [jnp.dot(ges[kb], gw_ref[kb], preferred_element_type=F32) + gb_ref[kb] for kb in blocks]
    for kb in blocks:
        y_scr[kb] = gls[kb][:, :LANES] * jax.nn.sigmoid(gls[kb][:, LANES:])
    for b in range(bsz):
        y_ref[b] = jnp.concatenate([y_scr[kb, pl.ds(b, tt, stride=bsz), :] for kb in blocks], axis=-1).astype(BF16)

    @pl.when(i == pl.num_programs(0) - 1)
    def _():
        hre_out_ref[...] = hre_scr[...]
        him_out_ref[...] = him_scr[...]


def _s5(u, h0re, h0im, lr, li, bb, cc, dvec, gw, gb, tt):
    bsz, length, _ = u.shape
    grid = (length // tt,)
    st_shape = (S5_BLOCKS, bsz, S5_BLOCK_STATE)
    tok = pl.BlockSpec((bsz, tt, S5_WIDTH), lambda i: (0, i, 0))
    tb_scratch = pltpu.VMEM((S5_BLOCKS, tt * bsz, LANES), F32)
    return pl.pallas_call(
        functools.partial(_s5_kernel, tt=tt, bsz=bsz),
        grid=grid,
        in_specs=[tok, _const_spec(st_shape), _const_spec(st_shape), _const_spec(lr.shape), _const_spec(li.shape),
                  _const_spec(bb.shape), _const_spec(cc.shape), _const_spec(dvec.shape), _const_spec(gw.shape),
                  _const_spec(gb.shape)],
        out_specs=[tok, pl.BlockSpec(st_shape, lambda i: (0, 0, 0)), pl.BlockSpec(st_shape, lambda i: (0, 0, 0))],
        out_shape=[jax.ShapeDtypeStruct(u.shape, BF16),
                   jax.ShapeDtypeStruct(st_shape, F32), jax.ShapeDtypeStruct(st_shape, F32)],
        scratch_shapes=[tb_scratch, tb_scratch, pltpu.VMEM((S5_BLOCKS, tt * bsz, 2 * S5_BLOCK_STATE), F32),
                        pltpu.VMEM(st_shape, F32), pltpu.VMEM(st_shape, F32)],
        compiler_params=_params(1),
        name="s5",
    )(u, h0re, h0im, lr, li, bb, cc, dvec, gw, gb)


def _ffn_kernel(x_ref, ys_ref, y5_ref, hist_ref, wout_ref, n1_ref, n2_ref, wup_ref, cw_ref, cb_ref, wdn_ref, n3_ref,
                y_ref, carry_ref, perm_scr, act_scr, *, tm):
    i = pl.program_id(1)
    b = 0
    nj = tm // SUBLANES
    grp = SUBLANES

    @pl.when(i == 0)
    def _():
        carry_ref[b] = hist_ref[b]

    mix = (jnp.dot(ys_ref[0], wout_ref[0:SSD_WIDTH, :], preferred_element_type=F32)
           + jnp.dot(y5_ref[0], wout_ref[SSD_WIDTH:, :], preferred_element_type=F32))
    h = x_ref[0] + _rms(mix, n1_ref[...])
    hn = _rms(h, n2_ref[...])

    n_slab = D_MODEL // LANES
    for s in range(n_slab):
        for seg in range(grp):
            perm_scr[s, pl.ds(seg, nj, stride=grp), :] = hn[seg * nj:(seg + 1) * nj, s * LANES:(s + 1) * LANES]
    hp = jnp.concatenate([perm_scr[s] for s in range(n_slab)], axis=-1).astype(BF16)

    first_sublane = lax.broadcasted_iota(jnp.int32, (grp, FFN_CHUNK), 0) == 0

    def conv_chunk(c0):
        cols = slice(c0, c0 + FFN_CHUNK)
        up = jnp.dot(hp, wup_ref[:, cols], preferred_element_type=F32)

        def wrapped(group, fill_row):
            fill = jnp.broadcast_to(carry_ref[b, fill_row:fill_row + 1, cols], (grp, FFN_CHUNK))
            return jnp.where(first_sublane, fill, pltpu.roll(group, 1, 0))

        s2 = wrapped(up[tm - 2 * grp:tm - grp], grp - 1)
        s1 = wrapped(up[tm - grp:tm], 2 * grp - 1)
        carry_ref[b, :, cols] = up[tm - 2 * grp:tm]
        prev1 = jnp.concatenate([s1, up[:tm - grp]], axis=0)
        prev2 = jnp.concatenate([s2, s1, up[:tm - 2 * grp]], axis=0)
        return (cb_ref[:, cols] + up * cw_ref[2:3, cols] + prev1 * cw_ref[1:2, cols] + prev2 * cw_ref[0:1, cols])

    for c in range(D_FF // FFN_CHUNK):
        gate = conv_chunk(c * FFN_CHUNK)
        val = conv_chunk(D_FF + c * FFN_CHUNK)
        act_scr[:, c * FFN_CHUNK:(c + 1) * FFN_CHUNK] = (_gelu_tanh(gate) * val).astype(BF16)

    ffn = _rms(jnp.dot(act_scr[...], wdn_ref[...], preferred_element_type=F32), n3_ref[...])
    for s in range(n_slab):
        perm_scr[s] = ffn[:, s * LANES:(s + 1) * LANES]
    ffn = jnp.concatenate(
        [jnp.concatenate([perm_scr[s, pl.ds(seg, nj, stride=grp), :] for seg in range(grp)], axis=0)
         for s in range(n_slab)], axis=-1)
    y_ref[0] = h + ffn


def _ffn(x, ys, y5_tb, hist, wout, n1, n2, wup, cw, cb, wdn, n3, tm):
    bsz, length, _ = x.shape
    assert FFN_CONV == 3 and tm % SUBLANES == 0 and tm // SUBLANES >= 2
    grid = (bsz, length // tm)
    tok = lambda w: pl.BlockSpec((1, tm, w), lambda b, i: (b, i, 0))
    per_b = pl.BlockSpec((1,) + hist.shape[1:], lambda b, i: (b, 0, 0))
    return pl.pallas_call(
        functools.partial(_ffn_kernel, tm=tm),
        grid=grid,
        in_specs=[tok(D_MODEL), tok(SSD_WIDTH), tok(S5_WIDTH), per_b,
                  _const_spec(wout.shape), _const_spec(n1.shape), _const_spec(n2.shape), _const_spec(wup.shape),
                  _const_spec(cw.shape), _const_spec(cb.shape), _const_spec(wdn.shape), _const_spec(n3.shape)],
        out_specs=[tok(D_MODEL), per_b],
        out_shape=[jax.ShapeDtypeStruct((bsz, length, D_MODEL), F32), jax.ShapeDtypeStruct(hist.shape, F32)],
        scratch_shapes=[pltpu.VMEM((D_MODEL // LANES, tm, LANES), F32), pltpu.VMEM((tm, D_FF), BF16)],
        compiler_params=_params(2),
        name="outffn",
    )(x, ys, y5_tb, hist, wout, n1, n2, wup, cw, cb, wdn, n3)


def _block_diag(w):
    _, r, c = w.shape
    w4 = w.reshape(S5_BLOCKS, S5_GPB, r, c)
    eye = jnp.eye(S5_GPB, dtype=w.dtype)
    return jnp.einsum("kgrc,gh->kgrhc", w4, eye).reshape(S5_BLOCKS, S5_GPB * r, S5_GPB * c)


def _prep_weights(pre_mix_norm_w, w_in, ssd_conv_w, ssd_conv_b, ssd_dt_bias, ssd_a_log, ssd_d, ssd_norm_w,
                  s5_lambda_re, s5_lambda_im, s5_log_dt, s5_b_re, s5_b_im, s5_c_re, s5_c_im, s5_d,
                  s5_glu_w, s5_glu_b, w_out, post_mix_norm_w, pre_ffn_norm_w, w_up, ffn_conv_w, ffn_conv_b,
                  w_down, post_ffn_norm_w):
    o_dt = SSD_WIDTH + SSD_CONV_DIM
    o_u = o_dt + SSD_HEADS
    pad_heads = lambda v: jnp.pad(v, (0, LANES - SSD_HEADS)).reshape(1, LANES)
    w = dict(
        nw0=pre_mix_norm_w.reshape(1, D_MODEL),
        wa=w_in[:, :o_dt].astype(BF16),
        wdt=jnp.pad(w_in[:, o_dt:o_u], ((0, 0), (0, LANES - SSD_HEADS))).astype(BF16),
        wu=w_in[:, o_u:].astype(BF16),
        cw=ssd_conv_w, cb=ssd_conv_b.reshape(1, SSD_CONV_DIM),
        dtb=pad_heads(ssd_dt_bias), alog=pad_heads(ssd_a_log),
        dssd=jnp.repeat(ssd_d, SSD_HEAD_DIM).reshape(1, SSD_WIDTH),
        nssd=ssd_norm_w.reshape(1, SSD_WIDTH),
        wout=w_out.astype(BF16), n1=post_mix_norm_w.reshape(1, D_MODEL), n2=pre_ffn_norm_w.reshape(1, D_MODEL),
        wup=w_up.astype(BF16), fcw=ffn_conv_w, fcb=ffn_conv_b.reshape(1, 2 * D_FF),
        wdn=w_down.astype(BF16), n3=post_ffn_norm_w.reshape(1, D_MODEL),
    )
    dt = jnp.exp(s5_log_dt)[:, None]
    mag = jnp.exp(s5_lambda_re * dt)
    ang = s5_lambda_im * dt
    lb_re = mag * jnp.cos(ang)
    lb_im = mag * jnp.sin(ang)
    den = s5_lambda_re * s5_lambda_re + s5_lambda_im * s5_lambda_im
    q_re = ((lb_re - 1) * s5_lambda_re + lb_im * s5_lambda_im) / den
    q_im = (lb_im * s5_lambda_re - (lb_re - 1) * s5_lambda_im) / den
    bb_re = q_re[..., None] * s5_b_re - q_im[..., None] * s5_b_im
    bb_im = q_re[..., None] * s5_b_im + q_im[..., None] * s5_b_re
    to_in = lambda m: _block_diag(jnp.swapaxes(m, 1, 2))
    w["bb"] = jnp.concatenate([to_in(bb_re), to_in(bb_im)], axis=-1).astype(BF16)
    to_out = lambda m: _block_diag(jnp.swapaxes(m, 1, 2))
    w["cc"] = jnp.concatenate([to_out(s5_c_re), to_out(-s5_c_im)], axis=1).astype(BF16)
    w["lr"] = lb_re.reshape(S5_BLOCKS, 1, S5_BLOCK_STATE)
    w["li"] = lb_im.reshape(S5_BLOCKS, 1, S5_BLOCK_STATE)
    w["d5"] = s5_d.reshape(1, S5_WIDTH)
    w["gw"] = jnp.concatenate([_block_diag(s5_glu_w[..., :S5_GROUP_CH]), _block_diag(s5_glu_w[..., S5_GROUP_CH:])],
                              axis=-1).astype(BF16)
    gb = lambda v: v.reshape(S5_BLOCKS, 1, LANES)
    w["gb"] = jnp.concatenate([gb(s5_glu_b[:, :S5_GROUP_CH]), gb(s5_glu_b[:, S5_GROUP_CH:])], axis=-1)
    return w


def _hist_tile(hist):
    return jnp.pad(hist, ((0, 0), (HIST_ROWS - hist.shape[1], 0), (0, 0)))


def _layer(x, conv_hist, ssd_h0, s5_re, s5_im, ffn_hist, w, *, tm_in, tm, q, tt, nb, nc):
    bsz, length, _ = x.shape
    h0 = jnp.transpose(ssd_h0, (0, 3, 1, 2)).reshape(bsz, SSD_STATE, SSD_WIDTH)
    if length % tm == 0 and length // tm >= 2:
        u, y_ssd, tail, st = _mixer(x, _hist_tile(conv_hist), h0, w["nw0"], w["wa"], w["wdt"], w["wu"], w["cw"],
                                    w["cb"], w["dtb"], w["alog"], w["dssd"], w["nssd"], tm, q, nb)
        new_conv = tail[:, HIST_ROWS - (SSD_CONV - 1):, :]
    else:
        z, xbc, dt, u = _inproj(x, w["nw0"], w["wa"], w["wdt"], w["wu"], tm_in)
        new_conv = xbc[:, length - (SSD_CONV - 1):, :].astype(F32)
        lpad = -length % q
        if lpad:
            assert length < q, "a padded sequence must fit one SSD chunk"
            padt = lambda a: jnp.pad(a, ((0, 0), (0, lpad), (0, 0)))
            z, xbc, dt = padt(z), padt(xbc), padt(dt)
        y_ssd, st = _ssd(z, xbc, dt, _hist_tile(conv_hist), h0, w["cw"], w["cb"], w["dtb"], w["alog"], w["dssd"],
                         w["nssd"], q, length if lpad else q, nb, nc)
        y_ssd = y_ssd[:, :length]
    new_ssd = jnp.transpose(st.reshape(bsz, SSD_STATE, SSD_HEADS, SSD_HEAD_DIM), (0, 2, 3, 1))

    to_blocks = lambda s: jnp.transpose(s.reshape(bsz, S5_BLOCKS, S5_BLOCK_STATE), (1, 0, 2))
    from_blocks = lambda s: jnp.transpose(s, (1, 0, 2)).reshape(bsz, S5_GROUPS, S5_STATE)
    y5, hre, him = _s5(u, to_blocks(s5_re), to_blocks(s5_im), w["lr"], w["li"], w["bb"], w["cc"], w["d5"],
                       w["gw"], w["gb"], tt)

    last = SUBLANES - 1
    hist = jnp.zeros((bsz, 2 * SUBLANES, 2 * D_FF), F32).at[:, last::SUBLANES, :].set(ffn_hist)
    y, carry = _ffn(x, y_ssd, y5, hist, w["wout"], w["n1"], w["n2"], w["wup"], w["fcw"], w["fcb"],
                    w["wdn"], w["n3"], tm)
    new_ffn = carry[:, last::SUBLANES, :]
    return y, new_conv, new_ssd, from_blocks(hre), from_blocks(him), new_ffn


def _tiles(length):
    tm_in = min(length, 1024)
    tm = min(length, 512)
    q = 128
    nc = max(1, min(SSD_CHUNKS_PER_STEP, length // q))
    tt = min(length, 128)
    assert length % tm_in == 0 and length % tm == 0 and length % tt == 0 and length >= SSD_CONV - 1
    return dict(tm_in=tm_in, tm=tm, q=q, tt=tt, nb=SSD_SEQS_PER_STEP, nc=nc)


def kernel(x_prompt, x_sample, cache_ssd_conv, state_ssd, state_s5_re, state_s5_im, cache_ffn_conv, pre_mix_norm_w, w_in, ssd_conv_w, ssd_conv_b, ssd_dt_bias, ssd_a_log, ssd_d, ssd_norm_w, s5_lambda_re, s5_lambda_im, s5_log_dt, s5_b_re, s5_b_im, s5_c_re, s5_c_im, s5_d, s5_glu_w, s5_glu_b, w_out, post_mix_norm_w, pre_ffn_norm_w, w_up, ffn_conv_w, ffn_conv_b, w_down, post_ffn_norm_w):
    depth = w_in.shape[0]
    bsz = x_prompt.shape[0]
    dtp = x_prompt.dtype
    layer_params = (pre_mix_norm_w, w_in, ssd_conv_w, ssd_conv_b, ssd_dt_bias, ssd_a_log, ssd_d, ssd_norm_w,
                    s5_lambda_re, s5_lambda_im, s5_log_dt, s5_b_re, s5_b_im, s5_c_re, s5_c_im, s5_d,
                    s5_glu_w, s5_glu_b, w_out, post_mix_norm_w, pre_ffn_norm_w, w_up, ffn_conv_w, ffn_conv_b,
                    w_down, post_ffn_norm_w)
    y_prompt, y_sample = x_prompt, x_sample
    prompt_states, sample_states = [], []
    for l in range(depth):
        w = _prep_weights(*(p[l] for p in layer_params))
        y_prompt, *ps = _layer(
            y_prompt,
            jnp.zeros((bsz, SSD_CONV - 1, SSD_CONV_DIM), dtp),
            jnp.zeros((bsz, SSD_HEADS, SSD_HEAD_DIM, SSD_STATE), dtp),
            jnp.zeros((bsz, S5_GROUPS, S5_STATE), dtp),
            jnp.zeros((bsz, S5_GROUPS, S5_STATE), dtp),
            jnp.zeros((bsz, FFN_CONV - 1, 2 * D_FF), dtp),
            w, **_tiles(y_prompt.shape[1]))
        y_sample, *ss = _layer(
            y_sample, cache_ssd_conv[l], state_ssd[l], state_s5_re[l], state_s5_im[l], cache_ffn_conv[l],
            w, **_tiles(y_sample.shape[1]))
        prompt_states.append(ps)
        sample_states.append(ss)
    stack = lambda states: tuple(jnp.stack([s[k] for s in states]) for k in range(5))
    return (y_prompt, y_sample) + stack(prompt_states) + stack(sample_states)
```

```python
import functools
import math
from typing import Callable, NamedTuple

import jax
import jax.numpy as jnp
from jax import lax
from jax.experimental import pallas as pl
from jax.experimental.pallas import tpu as pltpu

F32 = jnp.float32
BF16 = jnp.bfloat16

D_MODEL = 1024
SSD_WIDTH = 512
SSD_HEAD_DIM = 64
SSD_HEADS = 8
SSD_GROUPS = 2
SSD_HPG = SSD_HEADS // SSD_GROUPS
SSD_GROUP_W = SSD_WIDTH // SSD_GROUPS
SSD_STATE = 128
SSD_CONV = 4
SSD_CONV_DIM = SSD_WIDTH + 2 * SSD_GROUPS * SSD_STATE
S5_WIDTH = 512
S5_GROUP_CH = 16
S5_GROUPS = 32
S5_STATE = 64
S5_BLOCKS = 4
S5_GPB = S5_GROUPS // S5_BLOCKS
S5_BLOCK_STATE = S5_GPB * S5_STATE
S5_PART_STEPS = 64
D_FF = 2816
FFN_CONV = 3
EPS = 1e-6

LANES = 128
SUBLANES = 8
HIST_ROWS = SUBLANES
VMEM_LIMIT = 56 * 1024 * 1024

FFN_CHUNK = 256
SSD_SEQS_PER_STEP = 2
SSD_CHUNKS_PER_STEP = 4


def _rms(x, w):
    ms = jnp.mean(x * x, axis=-1, keepdims=True)
    return x * lax.rsqrt(ms + EPS) * w


def _gelu_tanh(x):
    c = math.sqrt(2.0 / math.pi)
    hx = 0.5 * x
    return hx + hx * jnp.tanh(x * (c + (c * 0.044715) * (x * x)))


def _softplus(x):
    return jnp.maximum(x, 0.0) + jnp.log1p(jnp.exp(-jnp.abs(x)))


def _causal_conv(ext, cur, w_ref, b_ref, c0, rows, taps):
    cols = slice(c0, c0 + LANES)
    ext[HIST_ROWS:HIST_ROWS + rows, :] = cur
    out = b_ref[:, cols] + cur * w_ref[taps - 1:taps, cols]
    for k in range(1, taps):
        out = out + ext[pl.ds(HIST_ROWS - k, rows), :] * w_ref[taps - 1 - k:taps - k, cols]
    ext[0:HIST_ROWS, :] = ext[rows:rows + HIST_ROWS, :]
    return out


def _drain(gen):
    for _ in gen:
        pass


def _interleave(main, sides, every):
    sides = list(sides)
    turn = 0
    for n, _ in enumerate(main, 1):
        if sides and n % every == 0:
            side = sides[turn % len(sides)]
            try:
                next(side)
                turn += 1
            except StopIteration:
                sides.remove(side)
    for side in sides:
        _drain(side)


def _const_spec(shape):
    nd = len(shape)
    return pl.BlockSpec(shape, lambda *_: (0,) * nd, pipeline_mode=pl.Buffered(1))


def _params(n_axes):
    return pltpu.CompilerParams(dimension_semantics=("arbitrary",) * n_axes, vmem_limit_bytes=VMEM_LIMIT)


def _inproj_kernel(x_ref, nw_ref, wa_ref, wdt_ref, wu_ref, z_ref, xbc_ref, dt_ref, u_ref):
    x = x_ref[0]
    xn = _rms(x, nw_ref[...]).astype(BF16)
    a = jnp.dot(xn, wa_ref[...], preferred_element_type=F32)
    z_ref[0] = a[:, :SSD_WIDTH].astype(BF16)
    xbc_ref[0] = a[:, SSD_WIDTH:].astype(BF16)
    dt_ref[0] = jnp.dot(xn, wdt_ref[...], preferred_element_type=F32)
    u_ref[0] = jnp.dot(xn, wu_ref[...], preferred_element_type=F32).astype(BF16)


def _inproj(x, nw, wa, wdt, wu, tm):
    bsz, length, _ = x.shape
    grid = (bsz, length // tm)
    tok = lambda w, dt: (jax.ShapeDtypeStruct((bsz, length, w), dt), pl.BlockSpec((1, tm, w), lambda b, i: (b, i, 0)))
    outs = [tok(SSD_WIDTH, BF16), tok(SSD_CONV_DIM, BF16), tok(LANES, F32), tok(S5_WIDTH, BF16)]
    return pl.pallas_call(
        _inproj_kernel,
        grid=grid,
        in_specs=[pl.BlockSpec((1, tm, D_MODEL), lambda b, i: (b, i, 0)),
                  _const_spec(nw.shape), _const_spec(wa.shape), _const_spec(wdt.shape), _const_spec(wu.shape)],
        out_specs=[o[1] for o in outs],
        out_shape=[o[0] for o in outs],
        compiler_params=_params(2),
        name="inproj",
    )(x, nw, wa, wdt, wu)


def _split3(v):
    hi = v.astype(BF16)
    r1 = v - hi.astype(F32)
    mid = r1.astype(BF16)
    lo = (r1 - mid.astype(F32)).astype(BF16)
    return hi, mid, lo


def _head_tiles(a):
    q = a.shape[0]
    lane = lax.broadcasted_iota(jnp.int32, (q, LANES), 1)
    tiles = []
    for j in range(SSD_HEADS * SSD_HEAD_DIM // LANES):
        even = jnp.broadcast_to(a[:, 2 * j:2 * j + 1], (q, LANES))
        odd = jnp.broadcast_to(a[:, 2 * j + 1:2 * j + 2], (q, LANES))
        tiles.append(jnp.where(lane < SSD_HEAD_DIM, even, odd))
    return tiles


def _ssd_kernel(z_ref, xbc_ref, dt_ref, hist_ref, h0_ref, cw_ref, cb_ref, dtb_ref, alog_ref, d_ref, nw_ref,
                y_ref, st_out_ref, ext_scr, st_scr, *, q, valid, nb, nc):
    t = pl.program_id(1)
    n_slab = SSD_CONV_DIM // LANES

    @pl.when(t == 0)
    def _():
        for s in range(nb):
            for j in range(n_slab):
                ext_scr[s * n_slab + j, 0:HIST_ROWS, :] = hist_ref[s, :, j * LANES:(j + 1) * LANES]
        st_scr[...] = h0_ref[...]

    def io(s):
        def put_y(rows, v):
            y_ref[s, rows, :] = v
        return SsdIO(z=lambda rows: z_ref[s, rows, :], xbc=lambda rows: xbc_ref[s, rows, :],
                     dt=lambda rows: dt_ref[s, rows, :], put_y=put_y)

    for c in range(nc):
        chains = [_ssd_chunk(s, slice(c * q, (c + 1) * q), io(s), cw_ref, cb_ref, dtb_ref, alog_ref,
                             d_ref, nw_ref, ext_scr, st_scr, q=q, valid=valid) for s in range(nb)]
        for _ in zip(*chains):
            pass

    @pl.when(t == pl.num_programs(1) - 1)
    def _():
        st_out_ref[...] = st_scr[...]


class SsdIO(NamedTuple):
    z: Callable
    xbc: Callable
    dt: Callable
    put_y: Callable


def _ssd_chunk(s, rows, io, cw_ref, cb_ref, dtb_ref, alog_ref, d_ref, nw_ref, ext_scr, st_scr, *, q, valid):
    n_slab = SSD_CONV_DIM // LANES
    st_scr = st_scr.at[s]
    xraw = io.xbc(rows).astype(F32)
    slabs = []
    for j in range(n_slab):
        conv = _causal_conv(ext_scr.at[s * n_slab + j], xraw[:, j * LANES:(j + 1) * LANES], cw_ref, cb_ref,
                            j * LANES, q, SSD_CONV)
        slabs.append(conv * jax.nn.sigmoid(conv))
        if j % 2 == 1:
            yield
    xact = jnp.concatenate(slabs, axis=-1)
    xs = xact[:, :SSD_WIDTH]
    xs_bf = xs.astype(BF16)
    yield

    dt = _softplus(io.dt(rows) + dtb_ref[...])
    row = lax.broadcasted_iota(jnp.int32, (q, q), 0)
    col = lax.broadcasted_iota(jnp.int32, (q, q), 1)
    causal = row >= col
    if valid < q:
        dt = jnp.where(lax.broadcasted_iota(jnp.int32, dt.shape, 0) < valid, dt, 0.0)
    a = -jnp.exp(alog_ref[...])
    tril = jnp.where(causal, 1.0, 0.0).astype(BF16)
    cs = sum(jnp.dot(tril, part, preferred_element_type=F32) for part in _split3(dt * a))
    cs_t = cs.T
    dt_t = dt.T
    yield
    cs_tiles = _head_tiles(cs)
    dt_tiles = _head_tiles(dt)
    cat = lambda tiles: jnp.concatenate(tiles, axis=-1)
    decay_in = cat([jnp.exp(c) for c in cs_tiles])
    w_end = cat([jnp.exp(c[q - 1:q, :] - c) * d for c, d in zip(cs_tiles, dt_tiles)])
    chunk_decay = cat([jnp.exp(c[q - 1:q, :]) for c in cs_tiles])
    yield

    lane = lax.broadcasted_iota(jnp.int32, (1, SSD_GROUP_W), 1)
    ys = []
    for g in range(SSD_GROUPS):
        b_g = xact[:, SSD_WIDTH + g * SSD_STATE:SSD_WIDTH + (g + 1) * SSD_STATE].astype(BF16)
        c_off = SSD_WIDTH + SSD_GROUPS * SSD_STATE
        c_g = xact[:, c_off + g * SSD_STATE:c_off + (g + 1) * SSD_STATE].astype(BF16)
        gs = slice(g * SSD_GROUP_W, (g + 1) * SSD_GROUP_W)
        x_g = xs_bf[:, gs]
        scores = lax.dot_general(c_g, b_g, (((1,), (1,)), ((), ())), preferred_element_type=F32)
        st_g = st_scr[:, gs]
        y_g = jnp.dot(c_g, st_g.astype(BF16), preferred_element_type=F32) * decay_in[:, gs]
        yield
        for j in range(SSD_HPG):
            h = g * SSD_HPG + j
            seg = cs[:, h:h + 1] - cs_t[h:h + 1, :]
            lmat = jnp.where(causal, jnp.exp(seg), 0.0) * dt_t[h:h + 1, :]
            m = (scores * lmat).astype(BF16)
            in_head = (lane >= j * SSD_HEAD_DIM) & (lane < (j + 1) * SSD_HEAD_DIM)
            x_h = jnp.where(in_head, x_g, jnp.zeros_like(x_g))
            y_g = y_g + jnp.dot(m, x_h, preferred_element_type=F32)
            yield
        ys.append(y_g)
        wx = (xs[:, gs] * w_end[:, gs]).astype(BF16)
        new = lax.dot_general(b_g, wx, (((0,), (0,)), ((), ())), preferred_element_type=F32)
        st_scr[:, gs] = st_g * chunk_decay[:, gs] + new
        yield
    y = jnp.concatenate(ys, axis=-1) + d_ref[...] * xs

    zf = io.z(rows).astype(F32)
    gz = y * (zf * jax.nn.sigmoid(zf))
    yield
    nw = nw_ref[...]
    outs = []
    for g in range(SSD_GROUPS):
        gs = slice(g * SSD_GROUP_W, (g + 1) * SSD_GROUP_W)
        outs.append(_rms(gz[:, gs], nw[:, gs]))
    io.put_y(rows, jnp.concatenate(outs, axis=-1).astype(BF16))
    yield


def _ssd(z, xbc, dt, hist, h0, cw, cb, dtb, alog, dvec, nw, q, valid, nb, nc):
    bsz, length, _ = z.shape
    assert bsz % nb == 0 and length % (nc * q) == 0
    grid = (bsz // nb, length // (nc * q))
    tok = lambda w: pl.BlockSpec((nb, nc * q, w), lambda b, i: (b, i, 0))
    per_b = lambda r, w: pl.BlockSpec((nb, r, w), lambda b, i: (b, 0, 0))
    return pl.pallas_call(
        functools.partial(_ssd_kernel, q=q, valid=valid, nb=nb, nc=nc),
        grid=grid,
        in_specs=[tok(SSD_WIDTH), tok(SSD_CONV_DIM), tok(LANES), per_b(HIST_ROWS, SSD_CONV_DIM),
                  per_b(SSD_STATE, SSD_WIDTH), _const_spec(cw.shape), _const_spec(cb.shape), _const_spec(dtb.shape),
                  _const_spec(alog.shape), _const_spec(dvec.shape), _const_spec(nw.shape)],
        out_specs=[tok(SSD_WIDTH), per_b(SSD_STATE, SSD_WIDTH)],
        out_shape=[jax.ShapeDtypeStruct((bsz, length, SSD_WIDTH), BF16),
                   jax.ShapeDtypeStruct((bsz, SSD_STATE, SSD_WIDTH), F32)],
        scratch_shapes=[pltpu.VMEM((nb * SSD_CONV_DIM // LANES, q + HIST_ROWS, LANES), F32),
                        pltpu.VMEM((nb, SSD_STATE, SSD_WIDTH), F32)],
        compiler_params=_params(2),
        name="ssd",
    )(z, xbc, dt, hist, h0, cw, cb, dtb, alog, dvec, nw)


MIX_COLS = 256
MIX_PHASES_PER_PIECE = 3


def _inproj_phases(x_ref, nw_ref, wa_ref, wdt_ref, wu_ref, xn_scr, zx_scr, dtp_scr, u_ref, slot, *, nb, tm):
    for j in range(nb):
        xn_scr[j * tm:(j + 1) * tm, :] = _rms(x_ref[j], nw_ref[...]).astype(BF16)
        yield
    for c0 in range(0, SSD_WIDTH + SSD_CONV_DIM, MIX_COLS):
        cols = slice(c0, c0 + MIX_COLS)
        for j in range(nb):
            rows = slice(j * tm, (j + 1) * tm)
            zx_scr[slot, rows, cols] = jnp.dot(xn_scr[rows, :], wa_ref[:, cols], preferred_element_type=F32)
            yield
    for j in range(nb):
        rows = slice(j * tm, (j + 1) * tm)
        dtp_scr[slot, rows, :] = jnp.dot(xn_scr[rows, :], wdt_ref[...], preferred_element_type=F32)
        yield
    for c0 in range(0, S5_WIDTH, MIX_COLS):
        cols = slice(c0, c0 + MIX_COLS)
        for j in range(nb):
            rows = slice(j * tm, (j + 1) * tm)
            u_ref[j, :, cols] = jnp.dot(xn_scr[rows, :], wu_ref[:, cols], preferred_element_type=F32).astype(BF16)
            yield


def _mixer_kernel(x_ref, hist_ref, h0_ref, nw0_ref, wa_ref, wdt_ref, wu_ref, cw_ref, cb_ref, dtb_ref, alog_ref, d_ref,
                  nw_ref, u_ref, y_ref, tail_ref, st_out_ref, xn_scr, zx_scr, dtp_scr, ext_scr, st_scr,
                  *, tm, q, nb, nt):
    s = pl.program_id(0)
    prev = s - 1
    wslot = lax.rem(s, 2)
    rslot = 1 - wslot
    n_slab = SSD_CONV_DIM // LANES
    seq_pos = lax.rem(prev + nt, nt)

    @pl.when(s == 0)
    def _():
        zx_scr[1] = jnp.zeros(zx_scr.shape[1:], F32)
        dtp_scr[1] = jnp.zeros(dtp_scr.shape[1:], F32)
        ext_scr[...] = jnp.zeros(ext_scr.shape, F32)
        st_scr[...] = jnp.zeros(st_scr.shape, F32)

    @pl.when((seq_pos == 0) & (s > 0))
    def _():
        for j in range(nb):
            for k in range(n_slab):
                ext_scr[j * n_slab + k, 0:HIST_ROWS, :] = hist_ref[j, :, k * LANES:(k + 1) * LANES]
        st_scr[...] = h0_ref[...]

    def io(j):
        base = j * tm
        at = lambda rows: pl.ds(base + rows.start, rows.stop - rows.start)

        def put_y(rows, v):
            y_ref[j, rows, :] = v
        return SsdIO(z=lambda rows: zx_scr[rslot, at(rows), 0:SSD_WIDTH],
                     xbc=lambda rows: zx_scr[rslot, at(rows), SSD_WIDTH:SSD_WIDTH + SSD_CONV_DIM],
                     dt=lambda rows: dtp_scr[rslot, at(rows), :], put_y=put_y)

    inproj = _inproj_phases(x_ref, nw0_ref, wa_ref, wdt_ref, wu_ref, xn_scr, zx_scr, dtp_scr, u_ref, wslot,
                            nb=nb, tm=tm)
    next(inproj)
    phase = 0
    for c in range(tm // q):
        chains = [_ssd_chunk(j, slice(c * q, (c + 1) * q), io(j), cw_ref, cb_ref, dtb_ref, alog_ref, d_ref, nw_ref,
                             ext_scr, st_scr, q=q, valid=q) for j in range(nb)]
        for _ in zip(*chains):
            phase += 1
            if phase % MIX_PHASES_PER_PIECE == 0:
                next(inproj, None)
    for _ in inproj:
        pass

    @pl.when((seq_pos == nt - 1) & (s > 0))
    def _():
        st_out_ref[...] = st_scr[...]
        for j in range(nb):
            for k in range(n_slab):
                tail_ref[j, :, k * LANES:(k + 1) * LANES] = ext_scr[j * n_slab + k, 0:HIST_ROWS, :]


def _mixer(x, hist, h0, nw0, wa, wdt, wu, cw, cb, dtb, alog, dvec, nw, tm, q, nb):
    bsz, length, _ = x.shape
    assert bsz % nb == 0 and length % tm == 0 and tm % q == 0
    nt = length // tm
    n_tiles = (bsz // nb) * nt
    cur = lambda s: jnp.minimum(s, n_tiles - 1)
    prv = lambda s: jnp.maximum(s - 1, 0)
    tile = lambda w, which: pl.BlockSpec((nb, tm, w), lambda s: (which(s) // nt, which(s) % nt, 0))
    per_seq = lambda r, w: pl.BlockSpec((nb, r, w), lambda s: (prv(s) // nt, 0, 0))
    rows = nb * tm
    return pl.pallas_call(
        functools.partial(_mixer_kernel, tm=tm, q=q, nb=nb, nt=nt),
        grid=(n_tiles + 1,),
        in_specs=[tile(D_MODEL, cur), per_seq(HIST_ROWS, SSD_CONV_DIM), per_seq(SSD_STATE, SSD_WIDTH),
                  _const_spec(nw0.shape), _const_spec(wa.shape), _const_spec(wdt.shape), _const_spec(wu.shape),
                  _const_spec(cw.shape), _const_spec(cb.shape), _const_spec(dtb.shape), _const_spec(alog.shape),
                  _const_spec(dvec.shape), _const_spec(nw.shape)],
        out_specs=[tile(S5_WIDTH, cur), tile(SSD_WIDTH, prv), per_seq(HIST_ROWS, SSD_CONV_DIM),
                   per_seq(SSD_STATE, SSD_WIDTH)],
        out_shape=[jax.ShapeDtypeStruct((bsz, length, S5_WIDTH), BF16),
                   jax.ShapeDtypeStruct((bsz, length, SSD_WIDTH), BF16),
                   jax.ShapeDtypeStruct((bsz, HIST_ROWS, SSD_CONV_DIM), F32),
                   jax.ShapeDtypeStruct((bsz, SSD_STATE, SSD_WIDTH), F32)],
        scratch_shapes=[pltpu.VMEM((rows, D_MODEL), BF16),
                        pltpu.VMEM((2, rows, SSD_WIDTH + SSD_CONV_DIM), F32),
                        pltpu.VMEM((2, rows, LANES), F32),
                        pltpu.VMEM((nb * SSD_CONV_DIM // LANES, q + HIST_ROWS, LANES), F32),
                        pltpu.VMEM((nb, SSD_STATE, SSD_WIDTH), F32)],
        compiler_params=_params(1),
        name="mixer",
    )(x, hist, h0, nw0, wa, wdt, wu, cw, cb, dtb, alog, dvec, nw)


def _s5_kernel(u_ref, h0re_ref, h0im_ref, lr_ref, li_ref, bb_ref, cc_ref, d_ref, gw_ref, gb_ref,
               y_ref, hre_out_ref, him_out_ref, u_scr, y_scr, hs_scr, hre_scr, him_scr, *, tt, bsz):
    i = pl.program_id(0)

    @pl.when(i == 0)
    def _():
        hre_scr[...] = h0re_ref[...]
        him_scr[...] = h0im_ref[...]

    half = S5_BLOCK_STATE
    blocks = range(S5_BLOCKS)

    for b in range(bsz):
        ub = u_ref[b].astype(F32)
        for kb in blocks:
            u_scr[kb, pl.ds(b, tt, stride=bsz), :] = ub[:, kb * LANES:(kb + 1) * LANES]

    n_parts = max(1, tt // S5_PART_STEPS)
    part_steps = tt // n_parts
    part_rows = lambda p: slice(p * part_steps * bsz, (p + 1) * part_steps * bsz)
    state = [(hre_scr[kb], him_scr[kb]) for kb in blocks]
    lam = [(jnp.broadcast_to(lr_ref[kb], (bsz, half)), jnp.broadcast_to(li_ref[kb], (bsz, half))) for kb in blocks]

    def expand(p):
        rows = part_rows(p)
        for kb in blocks:
            hs_scr[kb, rows, :] = jnp.dot(u_scr[kb, rows, :].astype(BF16), bb_ref[kb], preferred_element_type=F32)
            yield

    def scan(p):
        for t in range(p * part_steps, (p + 1) * part_steps):
            rows = slice(t * bsz, (t + 1) * bsz)
            for kb in blocks:
                (lr, li), (hr, hi) = lam[kb], state[kb]
                nr = lr * hr - li * hi + hs_scr[kb, rows, 0:half]
                ni = lr * hi + li * hr + hs_scr[kb, rows, half:2 * half]
                hs_scr[kb, rows, 0:half] = nr
                hs_scr[kb, rows, half:2 * half] = ni
                state[kb] = (nr, ni)
            if t % 2 == 1:
                yield

    def project(p):
        rows = part_rows(p)
        ys = []
        for kb in blocks:
            ys.append(jnp.dot(hs_scr[kb, rows, :].astype(BF16), cc_ref[kb], preferred_element_type=F32)
                      + d_ref[:, kb * LANES:(kb + 1) * LANES] * u_scr[kb, rows, :])
            yield
        ges = [_gelu_tanh(y).astype(BF16) for y in ys]
        yield
        gls = []
        for kb in blocks:
            gls.append(jnp.dot(ges[kb], gw_ref[kb], preferred_element_type=F32) + gb_ref[kb])
            yield
        for kb in blocks:
            y_scr[kb, rows, :] = gls[kb][:, :LANES] * jax.nn.sigmoid(gls[kb][:, LANES:])
        yield

    _drain(expand(0))
    for p in range(n_parts):
        side = []
        if p + 1 < n_parts:
            side.append(expand(p + 1))
        if p >= 1:
            side.append(project(p - 1))
        _interleave(scan(p), side, every=2)
    _drain(project(n_parts - 1))
    for kb in blocks:
        hre_scr[kb], him_scr[kb] = state[kb]

    for b in range(bsz):
        y_ref[b] = jnp.concatenate([y_scr[kb, pl.ds(b, tt, stride=bsz), :] for kb in blocks], axis=-1).astype(BF16)

    @pl.when(i == pl.num_programs(0) - 1)
    def _():
        hre_out_ref[...] = hre_scr[...]
        him_out_ref[...] = him_scr[...]


def _s5(u, h0re, h0im, lr, li, bb, cc, dvec, gw, gb, tt):
    bsz, length, _ = u.shape
    grid = (length // tt,)
    st_shape = (S5_BLOCKS, bsz, S5_BLOCK_STATE)
    tok = pl.BlockSpec((bsz, tt, S5_WIDTH), lambda i: (0, i, 0))
    tb_scratch = pltpu.VMEM((S5_BLOCKS, tt * bsz, LANES), F32)
    return pl.pallas_call(
        functools.partial(_s5_kernel, tt=tt, bsz=bsz),
        grid=grid,
        in_specs=[tok, _const_spec(st_shape), _const_spec(st_shape), _const_spec(lr.shape), _const_spec(li.shape),
                  _const_spec(bb.shape), _const_spec(cc.shape), _const_spec(dvec.shape), _const_spec(gw.shape),
                  _const_spec(gb.shape)],
        out_specs=[tok, pl.BlockSpec(st_shape, lambda i: (0, 0, 0)), pl.BlockSpec(st_shape, lambda i: (0, 0, 0))],
        out_shape=[jax.ShapeDtypeStruct(u.shape, BF16),
                   jax.ShapeDtypeStruct(st_shape, F32), jax.ShapeDtypeStruct(st_shape, F32)],
        scratch_shapes=[tb_scratch, tb_scratch, pltpu.VMEM((S5_BLOCKS, tt * bsz, 2 * S5_BLOCK_STATE), F32),
                        pltpu.VMEM(st_shape, F32), pltpu.VMEM(st_shape, F32)],
        compiler_params=_params(1),
        name="s5",
    )(u, h0re, h0im, lr, li, bb, cc, dvec, gw, gb)


def _ffn_kernel(x_ref, ys_ref, y5_ref, hist_ref, wout_ref, n1_ref, n2_ref, wup_ref, cw_ref, cb_ref, wdn_ref, n3_ref,
                y_ref, carry_ref, perm_scr, act_scr, *, tm):
    i = pl.program_id(1)
    b = 0
    nj = tm // SUBLANES
    grp = SUBLANES

    @pl.when(i == 0)
    def _():
        carry_ref[b] = hist_ref[b]

    mix = (jnp.dot(ys_ref[0], wout_ref[0:SSD_WIDTH, :], preferred_element_type=F32)
           + jnp.dot(y5_ref[0], wout_ref[SSD_WIDTH:, :], preferred_element_type=F32))
    h = x_ref[0] + _rms(mix, n1_ref[...])
    hn = _rms(h, n2_ref[...])

    n_slab = D_MODEL // LANES
    for s in range(n_slab):
        for seg in range(grp):
            perm_scr[s, pl.ds(seg, nj, stride=grp), :] = hn[seg * nj:(seg + 1) * nj, s * LANES:(s + 1) * LANES]
    hp = jnp.concatenate([perm_scr[s] for s in range(n_slab)], axis=-1).astype(BF16)

    first_sublane = lax.broadcasted_iota(jnp.int32, (grp, FFN_CHUNK), 0) == 0

    def conv_chunk(c0):
        cols = slice(c0, c0 + FFN_CHUNK)
        up = jnp.dot(hp, wup_ref[:, cols], preferred_element_type=F32)

        def wrapped(group, fill_row):
            fill = jnp.broadcast_to(carry_ref[b, fill_row:fill_row + 1, cols], (grp, FFN_CHUNK))
            return jnp.where(first_sublane, fill, pltpu.roll(group, 1, 0))

        s2 = wrapped(up[tm - 2 * grp:tm - grp], grp - 1)
        s1 = wrapped(up[tm - grp:tm], 2 * grp - 1)
        carry_ref[b, :, cols] = up[tm - 2 * grp:tm]
        prev1 = jnp.concatenate([s1, up[:tm - grp]], axis=0)
        prev2 = jnp.concatenate([s2, s1, up[:tm - 2 * grp]], axis=0)
        return (cb_ref[:, cols] + up * cw_ref[2:3, cols] + prev1 * cw_ref[1:2, cols] + prev2 * cw_ref[0:1, cols])

    for c in range(D_FF // FFN_CHUNK):
        gate = conv_chunk(c * FFN_CHUNK)
        val = conv_chunk(D_FF + c * FFN_CHUNK)
        act_scr[:, c * FFN_CHUNK:(c + 1) * FFN_CHUNK] = (_gelu_tanh(gate) * val).astype(BF16)

    ffn = _rms(jnp.dot(act_scr[...], wdn_ref[...], preferred_element_type=F32), n3_ref[...])
    for s in range(n_slab):
        perm_scr[s] = ffn[:, s * LANES:(s + 1) * LANES]
    ffn = jnp.concatenate(
        [jnp.concatenate([perm_scr[s, pl.ds(seg, nj, stride=grp), :] for seg in range(grp)], axis=0)
         for s in range(n_slab)], axis=-1)
    y_ref[0] = h + ffn


def _ffn(x, ys, y5_tb, hist, wout, n1, n2, wup, cw, cb, wdn, n3, tm):
    bsz, length, _ = x.shape
    assert FFN_CONV == 3 and tm % SUBLANES == 0 and tm // SUBLANES >= 2
    grid = (bsz, length // tm)
    tok = lambda w: pl.BlockSpec((1, tm, w), lambda b, i: (b, i, 0))
    per_b = pl.BlockSpec((1,) + hist.shape[1:], lambda b, i: (b, 0, 0))
    return pl.pallas_call(
        functools.partial(_ffn_kernel, tm=tm),
        grid=grid,
        in_specs=[tok(D_MODEL), tok(SSD_WIDTH), tok(S5_WIDTH), per_b,
                  _const_spec(wout.shape), _const_spec(n1.shape), _const_spec(n2.shape), _const_spec(wup.shape),
                  _const_spec(cw.shape), _const_spec(cb.shape), _const_spec(wdn.shape), _const_spec(n3.shape)],
        out_specs=[tok(D_MODEL), per_b],
        out_shape=[jax.ShapeDtypeStruct((bsz, length, D_MODEL), F32), jax.ShapeDtypeStruct(hist.shape, F32)],
        scratch_shapes=[pltpu.VMEM((D_MODEL // LANES, tm, LANES), F32), pltpu.VMEM((tm, D_FF), BF16)],
        compiler_params=_params(2),
        name="outffn",
    )(x, ys, y5_tb, hist, wout, n1, n2, wup, cw, cb, wdn, n3)


def _block_diag(w):
    _, r, c = w.shape
    w4 = w.reshape(S5_BLOCKS, S5_GPB, r, c)
    eye = jnp.eye(S5_GPB, dtype=w.dtype)
    return jnp.einsum("kgrc,gh->kgrhc", w4, eye).reshape(S5_BLOCKS, S5_GPB * r, S5_GPB * c)


def _prep_weights(pre_mix_norm_w, w_in, ssd_conv_w, ssd_conv_b, ssd_dt_bias, ssd_a_log, ssd_d, ssd_norm_w,
                  s5_lambda_re, s5_lambda_im, s5_log_dt, s5_b_re, s5_b_im, s5_c_re, s5_c_im, s5_d,
                  s5_glu_w, s5_glu_b, w_out, post_mix_norm_w, pre_ffn_norm_w, w_up, ffn_conv_w, ffn_conv_b,
                  w_down, post_ffn_norm_w):
    o_dt = SSD_WIDTH + SSD_CONV_DIM
    o_u = o_dt + SSD_HEADS
    pad_heads = lambda v: jnp.pad(v, (0, LANES - SSD_HEADS)).reshape(1, LANES)
    w = dict(
        nw0=pre_mix_norm_w.reshape(1, D_MODEL),
        wa=w_in[:, :o_dt].astype(BF16),
        wdt=jnp.pad(w_in[:, o_dt:o_u], ((0, 0), (0, LANES - SSD_HEADS))).astype(BF16),
        wu=w_in[:, o_u:].astype(BF16),
        cw=ssd_conv_w, cb=ssd_conv_b.reshape(1, SSD_CONV_DIM),
        dtb=pad_heads(ssd_dt_bias), alog=pad_heads(ssd_a_log),
        dssd=jnp.repeat(ssd_d, SSD_HEAD_DIM).reshape(1, SSD_WIDTH),
        nssd=ssd_norm_w.reshape(1, SSD_WIDTH),
        wout=w_out.astype(BF16), n1=post_mix_norm_w.reshape(1, D_MODEL), n2=pre_ffn_norm_w.reshape(1, D_MODEL),
        wup=w_up.astype(BF16), fcw=ffn_conv_w, fcb=ffn_conv_b.reshape(1, 2 * D_FF),
        wdn=w_down.astype(BF16), n3=post_ffn_norm_w.reshape(1, D_MODEL),
    )
    dt = jnp.exp(s5_log_dt)[:, None]
    mag = jnp.exp(s5_lambda_re * dt)
    ang = s5_lambda_im * dt
    lb_re = mag * jnp.cos(ang)
    lb_im = mag * jnp.sin(ang)
    den = s5_lambda_re * s5_lambda_re + s5_lambda_im * s5_lambda_im
    q_re = ((lb_re - 1) * s5_lambda_re + lb_im * s5_lambda_im) / den
    q_im = (lb_im * s5_lambda_re - (lb_re - 1) * s5_lambda_im) / den
    bb_re = q_re[..., None] * s5_b_re - q_im[..., None] * s5_b_im
    bb_im = q_re[..., None] * s5_b_im + q_im[..., None] * s5_b_re
    to_in = lambda m: _block_diag(jnp.swapaxes(m, 1, 2))
    w["bb"] = jnp.concatenate([to_in(bb_re), to_in(bb_im)], axis=-1).astype(BF16)
    to_out = lambda m: _block_diag(jnp.swapaxes(m, 1, 2))
    w["cc"] = jnp.concatenate([to_out(s5_c_re), to_out(-s5_c_im)], axis=1).astype(BF16)
    w["lr"] = lb_re.reshape(S5_BLOCKS, 1, S5_BLOCK_STATE)
    w["li"] = lb_im.reshape(S5_BLOCKS, 1, S5_BLOCK_STATE)
    w["d5"] = s5_d.reshape(1, S5_WIDTH)
    w["gw"] = jnp.concatenate([_block_diag(s5_glu_w[..., :S5_GROUP_CH]), _block_diag(s5_glu_w[..., S5_GROUP_CH:])],
                              axis=-1).astype(BF16)
    gb = lambda v: v.reshape(S5_BLOCKS, 1, LANES)
    w["gb"] = jnp.concatenate([gb(s5_glu_b[:, :S5_GROUP_CH]), gb(s5_glu_b[:, S5_GROUP_CH:])], axis=-1)
    return w


def _hist_tile(hist):
    return jnp.pad(hist, ((0, 0), (HIST_ROWS - hist.shape[1], 0), (0, 0)))


def _layer(x, conv_hist, ssd_h0, s5_re, s5_im, ffn_hist, w, *, tm_in, tm, q, tt, nb, nc):
    bsz, length, _ = x.shape
    h0 = jnp.transpose(ssd_h0, (0, 3, 1, 2)).reshape(bsz, SSD_STATE, SSD_WIDTH)
    if length % tm == 0 and length // tm >= 2:
        u, y_ssd, tail, st = _mixer(x, _hist_tile(conv_hist), h0, w["nw0"], w["wa"], w["wdt"], w["wu"], w["cw"],
                                    w["cb"], w["dtb"], w["alog"], w["dssd"], w["nssd"], tm, q, nb)
        new_conv = tail[:, HIST_ROWS - (SSD_CONV - 1):, :]
    else:
        z, xbc, dt, u = _inproj(x, w["nw0"], w["wa"], w["wdt"], w["wu"], tm_in)
        new_conv = xbc[:, length - (SSD_CONV - 1):, :].astype(F32)
        lpad = -length % q
        if lpad:
            assert length < q, "a padded sequence must fit one SSD chunk"
            padt = lambda a: jnp.pad(a, ((0, 0), (0, lpad), (0, 0)))
            z, xbc, dt = padt(z), padt(xbc), padt(dt)
        y_ssd, st = _ssd(z, xbc, dt, _hist_tile(conv_hist), h0, w["cw"], w["cb"], w["dtb"], w["alog"], w["dssd"],
                         w["nssd"], q, length if lpad else q, nb, nc)
        y_ssd = y_ssd[:, :length]
    new_ssd = jnp.transpose(st.reshape(bsz, SSD_STATE, SSD_HEADS, SSD_HEAD_DIM), (0, 2, 3, 1))

    to_blocks = lambda s: jnp.transpose(s.reshape(bsz, S5_BLOCKS, S5_BLOCK_STATE), (1, 0, 2))
    from_blocks = lambda s: jnp.transpose(s, (1, 0, 2)).reshape(bsz, S5_GROUPS, S5_STATE)
    y5, hre, him = _s5(u, to_blocks(s5_re), to_blocks(s5_im), w["lr"], w["li"], w["bb"], w["cc"], w["d5"],
                       w["gw"], w["gb"], tt)

    last = SUBLANES - 1
    hist = jnp.zeros((bsz, 2 * SUBLANES, 2 * D_FF), F32).at[:, last::SUBLANES, :].set(ffn_hist)
    y, carry = _ffn(x, y_ssd, y5, hist, w["wout"], w["n1"], w["n2"], w["wup"], w["fcw"], w["fcb"],
                    w["wdn"], w["n3"], tm)
    new_ffn = carry[:, last::SUBLANES, :]
    return y, new_conv, new_ssd, from_blocks(hre), from_blocks(him), new_ffn


def _tiles(length):
    tm_in = min(length, 1024)
    tm = min(length, 512)
    q = 128
    nc = max(1, min(SSD_CHUNKS_PER_STEP, length // q))
    tt = min(length, 128)
    assert length % tm_in == 0 and length % tm == 0 and length % tt == 0 and length >= SSD_CONV - 1
    return dict(tm_in=tm_in, tm=tm, q=q, tt=tt, nb=SSD_SEQS_PER_STEP, nc=nc)


def kernel(x_prompt, x_sample, cache_ssd_conv, state_ssd, state_s5_re, state_s5_im, cache_ffn_conv, pre_mix_norm_w, w_in, ssd_conv_w, ssd_conv_b, ssd_dt_bias, ssd_a_log, ssd_d, ssd_norm_w, s5_lambda_re, s5_lambda_im, s5_log_dt, s5_b_re, s5_b_im, s5_c_re, s5_c_im, s5_d, s5_glu_w, s5_glu_b, w_out, post_mix_norm_w, pre_ffn_norm_w, w_up, ffn_conv_w, ffn_conv_b, w_down, post_ffn_norm_w):
    depth = w_in.shape[0]
    bsz = x_prompt.shape[0]
    dtp = x_prompt.dtype
    layer_params = (pre_mix_norm_w, w_in, ssd_conv_w, ssd_conv_b, ssd_dt_bias, ssd_a_log, ssd_d, ssd_norm_w,
                    s5_lambda_re, s5_lambda_im, s5_log_dt, s5_b_re, s5_b_im, s5_c_re, s5_c_im, s5_d,
                    s5_glu_w, s5_glu_b, w_out, post_mix_norm_w, pre_ffn_norm_w, w_up, ffn_conv_w, ffn_conv_b,
                    w_down, post_ffn_norm_w)
    y_prompt, y_sample = x_prompt, x_sample
    prompt_states, sample_states = [], []
    for l in range(depth):
        w = _prep_weights(*(p[l] for p in layer_params))
        y_prompt, *ps = _layer(
            y_prompt,
            jnp.zeros((bsz, SSD_CONV - 1, SSD_CONV_DIM), dtp),
            jnp.zeros((bsz, SSD_HEADS, SSD_HEAD_DIM, SSD_STATE), dtp),
            jnp.zeros((bsz, S5_GROUPS, S5_STATE), dtp),
            jnp.zeros((bsz, S5_GROUPS, S5_STATE), dtp),
            jnp.zeros((bsz, FFN_CONV - 1, 2 * D_FF), dtp),
            w, **_tiles(y_prompt.shape[1]))
        y_sample, *ss = _layer(
            y_sample, cache_ssd_conv[l], state_ssd[l], state_s5_re[l], state_s5_im[l], cache_ffn_conv[l],
            w, **_tiles(y_sample.shape[1]))
        prompt_states.append(ps)
        sample_states.append(ss)
    stack = lambda states: tuple(jnp.stack([s[k] for s in states]) for k in range(5))
    return (y_prompt, y_sample) + stack(prompt_states) + stack(sample_states)
```

```python
import functools
import math
from typing import Callable, NamedTuple

import jax
import jax.numpy as jnp
from jax import lax
from jax.experimental import pallas as pl
from jax.experimental.pallas import tpu as pltpu

F32 = jnp.float32
BF16 = jnp.bfloat16

D_MODEL = 1024
SSD_WIDTH = 512
SSD_HEAD_DIM = 64
SSD_HEADS = 8
SSD_GROUPS = 2
SSD_HPG = SSD_HEADS // SSD_GROUPS
SSD_GROUP_W = SSD_WIDTH // SSD_GROUPS
SSD_STATE = 128
SSD_CONV = 4
SSD_CONV_DIM = SSD_WIDTH + 2 * SSD_GROUPS * SSD_STATE
S5_WIDTH = 512
S5_GROUP_CH = 16
S5_GROUPS = 32
S5_STATE = 64
S5_BLOCKS = 4
S5_GPB = S5_GROUPS // S5_BLOCKS
S5_BLOCK_STATE = S5_GPB * S5_STATE
S5_PART_STEPS = 64
D_FF = 2816
FFN_CONV = 3
EPS = 1e-6

LANES = 128
SUBLANES = 8
HIST_ROWS = SUBLANES
VMEM_LIMIT = 56 * 1024 * 1024

ROW_TILE = 512
FFN_CHUNK = 256
SSD_SEQS_PER_STEP = 2
SSD_CHUNKS_PER_STEP = 4


def _rms(x, w):
    ms = jnp.mean(x * x, axis=-1, keepdims=True)
    return x * lax.rsqrt(ms + EPS) * w


def _gelu_tanh(x):
    c = math.sqrt(2.0 / math.pi)
    hx = 0.5 * x
    return hx + hx * jnp.tanh(x * (c + (c * 0.044715) * (x * x)))


def _softplus(x):
    return jnp.maximum(x, 0.0) + jnp.log1p(jnp.exp(-jnp.abs(x)))


def _causal_conv(ext, cur, w_ref, b_ref, c0, rows, taps):
    cols = slice(c0, c0 + LANES)
    ext[HIST_ROWS:HIST_ROWS + rows, :] = cur
    out = b_ref[:, cols] + cur * w_ref[taps - 1:taps, cols]
    for k in range(1, taps):
        out = out + ext[pl.ds(HIST_ROWS - k, rows), :] * w_ref[taps - 1 - k:taps - k, cols]
    ext[0:HIST_ROWS, :] = ext[rows:rows + HIST_ROWS, :]
    return out


def _drain(gen):
    for _ in gen:
        pass


def _interleave(main, sides, every):
    sides = list(sides)
    turn = 0
    for n, _ in enumerate(main, 1):
        if sides and n % every == 0:
            side = sides[turn % len(sides)]
            try:
                next(side)
                turn += 1
            except StopIteration:
                sides.remove(side)
    for side in sides:
        _drain(side)


def _const_spec(shape):
    nd = len(shape)
    return pl.BlockSpec(shape, lambda *_: (0,) * nd, pipeline_mode=pl.Buffered(1))


def _params(n_axes):
    return pltpu.CompilerParams(dimension_semantics=("arbitrary",) * n_axes, vmem_limit_bytes=VMEM_LIMIT)


def _inproj_kernel(x_ref, nw_ref, wa_ref, wdt_ref, wu_ref, z_ref, xbc_ref, dt_ref, u_ref):
    x = x_ref[0]
    xn = _rms(x, nw_ref[...]).astype(BF16)
    a = jnp.dot(xn, wa_ref[...], preferred_element_type=F32)
    z_ref[0] = a[:, :SSD_WIDTH].astype(BF16)
    xbc_ref[0] = a[:, SSD_WIDTH:].astype(BF16)
    dt_ref[0] = jnp.dot(xn, wdt_ref[...], preferred_element_type=F32)
    u_ref[0] = jnp.dot(xn, wu_ref[...], preferred_element_type=F32).astype(BF16)


def _inproj(x, nw, wa, wdt, wu, tm):
    bsz, length, _ = x.shape
    grid = (bsz, length // tm)
    tok = lambda w, dt: (jax.ShapeDtypeStruct((bsz, length, w), dt), pl.BlockSpec((1, tm, w), lambda b, i: (b, i, 0)))
    outs = [tok(SSD_WIDTH, BF16), tok(SSD_CONV_DIM, BF16), tok(LANES, F32), tok(S5_WIDTH, BF16)]
    return pl.pallas_call(
        _inproj_kernel,
        grid=grid,
        in_specs=[pl.BlockSpec((1, tm, D_MODEL), lambda b, i: (b, i, 0)),
                  _const_spec(nw.shape), _const_spec(wa.shape), _const_spec(wdt.shape), _const_spec(wu.shape)],
        out_specs=[o[1] for o in outs],
        out_shape=[o[0] for o in outs],
        compiler_params=_params(2),
        name="inproj",
    )(x, nw, wa, wdt, wu)


def _split3(v):
    hi = v.astype(BF16)
    r1 = v - hi.astype(F32)
    mid = r1.astype(BF16)
    lo = (r1 - mid.astype(F32)).astype(BF16)
    return hi, mid, lo


def _head_tiles(a):
    q = a.shape[0]
    lane = lax.broadcasted_iota(jnp.int32, (q, LANES), 1)
    tiles = []
    for j in range(SSD_HEADS * SSD_HEAD_DIM // LANES):
        even = jnp.broadcast_to(a[:, 2 * j:2 * j + 1], (q, LANES))
        odd = jnp.broadcast_to(a[:, 2 * j + 1:2 * j + 2], (q, LANES))
        tiles.append(jnp.where(lane < SSD_HEAD_DIM, even, odd))
    return tiles


def _ssd_kernel(z_ref, xbc_ref, dt_ref, hist_ref, h0_ref, cw_ref, cb_ref, dtb_ref, alog_ref, d_ref, nw_ref,
                y_ref, st_out_ref, ext_scr, st_scr, *, q, valid, nb, nc):
    t = pl.program_id(1)
    n_slab = SSD_CONV_DIM // LANES

    @pl.when(t == 0)
    def _():
        for s in range(nb):
            for j in range(n_slab):
                ext_scr[s * n_slab + j, 0:HIST_ROWS, :] = hist_ref[s, :, j * LANES:(j + 1) * LANES]
        st_scr[...] = h0_ref[...]

    def io(s):
        def put_y(rows, v):
            y_ref[s, rows, :] = v
        return SsdIO(z=lambda rows: z_ref[s, rows, :], xbc=lambda rows: xbc_ref[s, rows, :],
                     dt=lambda rows: dt_ref[s, rows, :], put_y=put_y)

    for c in range(nc):
        chains = [_ssd_chunk(s, slice(c * q, (c + 1) * q), io(s), cw_ref, cb_ref, dtb_ref, alog_ref,
                             d_ref, nw_ref, ext_scr, st_scr, q=q, valid=valid) for s in range(nb)]
        for _ in zip(*chains):
            pass

    @pl.when(t == pl.num_programs(1) - 1)
    def _():
        st_out_ref[...] = st_scr[...]


class SsdIO(NamedTuple):
    z: Callable
    xbc: Callable
    dt: Callable
    put_y: Callable


def _ssd_chunk(s, rows, io, cw_ref, cb_ref, dtb_ref, alog_ref, d_ref, nw_ref, ext_scr, st_scr, *, q, valid):
    n_slab = SSD_CONV_DIM // LANES
    st_scr = st_scr.at[s]
    xraw = io.xbc(rows).astype(F32)
    slabs = []
    for j in range(n_slab):
        conv = _causal_conv(ext_scr.at[s * n_slab + j], xraw[:, j * LANES:(j + 1) * LANES], cw_ref, cb_ref,
                            j * LANES, q, SSD_CONV)
        slabs.append(conv * jax.nn.sigmoid(conv))
        if j % 2 == 1:
            yield
    xact = jnp.concatenate(slabs, axis=-1)
    xs = xact[:, :SSD_WIDTH]
    xs_bf = xs.astype(BF16)
    yield

    dt = _softplus(io.dt(rows) + dtb_ref[...])
    row = lax.broadcasted_iota(jnp.int32, (q, q), 0)
    col = lax.broadcasted_iota(jnp.int32, (q, q), 1)
    causal = row >= col
    if valid < q:
        dt = jnp.where(lax.broadcasted_iota(jnp.int32, dt.shape, 0) < valid, dt, 0.0)
    a = -jnp.exp(alog_ref[...])
    tril = jnp.where(causal, 1.0, 0.0).astype(BF16)
    cs = sum(jnp.dot(tril, part, preferred_element_type=F32) for part in _split3(dt * a))
    cs_t = cs.T
    dt_t = dt.T
    yield
    cs_tiles = _head_tiles(cs)
    dt_tiles = _head_tiles(dt)
    cat = lambda tiles: jnp.concatenate(tiles, axis=-1)
    decay_in = cat([jnp.exp(c) for c in cs_tiles])
    w_end = cat([jnp.exp(c[q - 1:q, :] - c) * d for c, d in zip(cs_tiles, dt_tiles)])
    chunk_decay = cat([jnp.exp(c[q - 1:q, :]) for c in cs_tiles])
    yield

    lane = lax.broadcasted_iota(jnp.int32, (1, SSD_GROUP_W), 1)
    ys = []
    for g in range(SSD_GROUPS):
        b_g = xact[:, SSD_WIDTH + g * SSD_STATE:SSD_WIDTH + (g + 1) * SSD_STATE].astype(BF16)
        c_off = SSD_WIDTH + SSD_GROUPS * SSD_STATE
        c_g = xact[:, c_off + g * SSD_STATE:c_off + (g + 1) * SSD_STATE].astype(BF16)
        gs = slice(g * SSD_GROUP_W, (g + 1) * SSD_GROUP_W)
        x_g = xs_bf[:, gs]
        scores = lax.dot_general(c_g, b_g, (((1,), (1,)), ((), ())), preferred_element_type=F32)
        st_g = st_scr[:, gs]
        y_g = jnp.dot(c_g, st_g.astype(BF16), preferred_element_type=F32) * decay_in[:, gs]
        yield
        for j in range(SSD_HPG):
            h = g * SSD_HPG + j
            seg = cs[:, h:h + 1] - cs_t[h:h + 1, :]
            lmat = jnp.where(causal, jnp.exp(seg), 0.0) * dt_t[h:h + 1, :]
            m = (scores * lmat).astype(BF16)
            in_head = (lane >= j * SSD_HEAD_DIM) & (lane < (j + 1) * SSD_HEAD_DIM)
            x_h = jnp.where(in_head, x_g, jnp.zeros_like(x_g))
            y_g = y_g + jnp.dot(m, x_h, preferred_element_type=F32)
            yield
        ys.append(y_g)
        wx = (xs[:, gs] * w_end[:, gs]).astype(BF16)
        new = lax.dot_general(b_g, wx, (((0,), (0,)), ((), ())), preferred_element_type=F32)
        st_scr[:, gs] = st_g * chunk_decay[:, gs] + new
        yield
    y = jnp.concatenate(ys, axis=-1) + d_ref[...] * xs

    zf = io.z(rows).astype(F32)
    gz = y * (zf * jax.nn.sigmoid(zf))
    yield
    nw = nw_ref[...]
    outs = []
    for g in range(SSD_GROUPS):
        gs = slice(g * SSD_GROUP_W, (g + 1) * SSD_GROUP_W)
        outs.append(_rms(gz[:, gs], nw[:, gs]))
    io.put_y(rows, jnp.concatenate(outs, axis=-1).astype(BF16))
    yield


def _ssd(z, xbc, dt, hist, h0, cw, cb, dtb, alog, dvec, nw, q, valid, nb, nc):
    bsz, length, _ = z.shape
    assert bsz % nb == 0 and length % (nc * q) == 0
    grid = (bsz // nb, length // (nc * q))
    tok = lambda w: pl.BlockSpec((nb, nc * q, w), lambda b, i: (b, i, 0))
    per_b = lambda r, w: pl.BlockSpec((nb, r, w), lambda b, i: (b, 0, 0))
    return pl.pallas_call(
        functools.partial(_ssd_kernel, q=q, valid=valid, nb=nb, nc=nc),
        grid=grid,
        in_specs=[tok(SSD_WIDTH), tok(SSD_CONV_DIM), tok(LANES), per_b(HIST_ROWS, SSD_CONV_DIM),
                  per_b(SSD_STATE, SSD_WIDTH), _const_spec(cw.shape), _const_spec(cb.shape), _const_spec(dtb.shape),
                  _const_spec(alog.shape), _const_spec(dvec.shape), _const_spec(nw.shape)],
        out_specs=[tok(SSD_WIDTH), per_b(SSD_STATE, SSD_WIDTH)],
        out_shape=[jax.ShapeDtypeStruct((bsz, length, SSD_WIDTH), BF16),
                   jax.ShapeDtypeStruct((bsz, SSD_STATE, SSD_WIDTH), F32)],
        scratch_shapes=[pltpu.VMEM((nb * SSD_CONV_DIM // LANES, q + HIST_ROWS, LANES), F32),
                        pltpu.VMEM((nb, SSD_STATE, SSD_WIDTH), F32)],
        compiler_params=_params(2),
        name="ssd",
    )(z, xbc, dt, hist, h0, cw, cb, dtb, alog, dvec, nw)


MIX_COLS = 256
MIX_PHASES_PER_PIECE = 3


def _inproj_phases(x_ref, nw_ref, wa_ref, wdt_ref, wu_ref, xn_scr, zx_scr, dtp_scr, u_ref, slot, *, nb, tm):
    for j in range(nb):
        xn_scr[j * tm:(j + 1) * tm, :] = _rms(x_ref[j], nw_ref[...]).astype(BF16)
        yield
    for c0 in range(0, SSD_WIDTH + SSD_CONV_DIM, MIX_COLS):
        cols = slice(c0, c0 + MIX_COLS)
        for j in range(nb):
            rows = slice(j * tm, (j + 1) * tm)
            zx_scr[slot, rows, cols] = jnp.dot(xn_scr[rows, :], wa_ref[:, cols], preferred_element_type=F32)
            yield
    for j in range(nb):
        rows = slice(j * tm, (j + 1) * tm)
        dtp_scr[slot, rows, :] = jnp.dot(xn_scr[rows, :], wdt_ref[...], preferred_element_type=F32)
        yield
    for c0 in range(0, S5_WIDTH, MIX_COLS):
        cols = slice(c0, c0 + MIX_COLS)
        for j in range(nb):
            rows = slice(j * tm, (j + 1) * tm)
            u_ref[j, :, cols] = jnp.dot(xn_scr[rows, :], wu_ref[:, cols], preferred_element_type=F32).astype(BF16)
            yield


def _mixer_kernel(x_ref, hist_ref, h0_ref, nw0_ref, wa_ref, wdt_ref, wu_ref, cw_ref, cb_ref, dtb_ref, alog_ref, d_ref,
                  nw_ref, u_ref, y_ref, tail_ref, st_out_ref, xn_scr, zx_scr, dtp_scr, ext_scr, st_scr,
                  *, tm, q, nb, nt):
    s = pl.program_id(0)
    prev = s - 1
    wslot = lax.rem(s, 2)
    rslot = 1 - wslot
    n_slab = SSD_CONV_DIM // LANES
    seq_pos = lax.rem(prev + nt, nt)

    @pl.when(s == 0)
    def _():
        zx_scr[1] = jnp.zeros(zx_scr.shape[1:], F32)
        dtp_scr[1] = jnp.zeros(dtp_scr.shape[1:], F32)
        ext_scr[...] = jnp.zeros(ext_scr.shape, F32)
        st_scr[...] = jnp.zeros(st_scr.shape, F32)

    @pl.when((seq_pos == 0) & (s > 0))
    def _():
        for j in range(nb):
            for k in range(n_slab):
                ext_scr[j * n_slab + k, 0:HIST_ROWS, :] = hist_ref[j, :, k * LANES:(k + 1) * LANES]
        st_scr[...] = h0_ref[...]

    def io(j):
        base = j * tm
        at = lambda rows: pl.ds(base + rows.start, rows.stop - rows.start)

        def put_y(rows, v):
            y_ref[j, rows, :] = v
        return SsdIO(z=lambda rows: zx_scr[rslot, at(rows), 0:SSD_WIDTH],
                     xbc=lambda rows: zx_scr[rslot, at(rows), SSD_WIDTH:SSD_WIDTH + SSD_CONV_DIM],
                     dt=lambda rows: dtp_scr[rslot, at(rows), :], put_y=put_y)

    inproj = _inproj_phases(x_ref, nw0_ref, wa_ref, wdt_ref, wu_ref, xn_scr, zx_scr, dtp_scr, u_ref, wslot,
                            nb=nb, tm=tm)
    next(inproj)
    phase = 0
    for c in range(tm // q):
        chains = [_ssd_chunk(j, slice(c * q, (c + 1) * q), io(j), cw_ref, cb_ref, dtb_ref, alog_ref, d_ref, nw_ref,
                             ext_scr, st_scr, q=q, valid=q) for j in range(nb)]
        for _ in zip(*chains):
            phase += 1
            if phase % MIX_PHASES_PER_PIECE == 0:
                next(inproj, None)
    for _ in inproj:
        pass

    @pl.when((seq_pos == nt - 1) & (s > 0))
    def _():
        st_out_ref[...] = st_scr[...]
        for j in range(nb):
            for k in range(n_slab):
                tail_ref[j, :, k * LANES:(k + 1) * LANES] = ext_scr[j * n_slab + k, 0:HIST_ROWS, :]


def _mixer(x, hist, h0, nw0, wa, wdt, wu, cw, cb, dtb, alog, dvec, nw, tm, q, nb):
    bsz, length, _ = x.shape
    assert bsz % nb == 0 and length % tm == 0 and tm % q == 0
    nt = length // tm
    n_tiles = (bsz // nb) * nt
    cur = lambda s: jnp.minimum(s, n_tiles - 1)
    prv = lambda s: jnp.maximum(s - 1, 0)
    tile = lambda w, which: pl.BlockSpec((nb, tm, w), lambda s: (which(s) // nt, which(s) % nt, 0))
    per_seq = lambda r, w: pl.BlockSpec((nb, r, w), lambda s: (prv(s) // nt, 0, 0))
    rows = nb * tm
    return pl.pallas_call(
        functools.partial(_mixer_kernel, tm=tm, q=q, nb=nb, nt=nt),
        grid=(n_tiles + 1,),
        in_specs=[tile(D_MODEL, cur), per_seq(HIST_ROWS, SSD_CONV_DIM), per_seq(SSD_STATE, SSD_WIDTH),
                  _const_spec(nw0.shape), _const_spec(wa.shape), _const_spec(wdt.shape), _const_spec(wu.shape),
                  _const_spec(cw.shape), _const_spec(cb.shape), _const_spec(dtb.shape), _const_spec(alog.shape),
                  _const_spec(dvec.shape), _const_spec(nw.shape)],
        out_specs=[tile(S5_WIDTH, cur), tile(SSD_WIDTH, prv), per_seq(HIST_ROWS, SSD_CONV_DIM),
                   per_seq(SSD_STATE, SSD_WIDTH)],
        out_shape=[jax.ShapeDtypeStruct((bsz, length, S5_WIDTH), BF16),
                   jax.ShapeDtypeStruct((bsz, length, SSD_WIDTH), BF16),
                   jax.ShapeDtypeStruct((bsz, HIST_ROWS, SSD_CONV_DIM), F32),
                   jax.ShapeDtypeStruct((bsz, SSD_STATE, SSD_WIDTH), F32)],
        scratch_shapes=[pltpu.VMEM((rows, D_MODEL), BF16),
                        pltpu.VMEM((2, rows, SSD_WIDTH + SSD_CONV_DIM), F32),
                        pltpu.VMEM((2, rows, LANES), F32),
                        pltpu.VMEM((nb * SSD_CONV_DIM // LANES, q + HIST_ROWS, LANES), F32),
                        pltpu.VMEM((nb, SSD_STATE, SSD_WIDTH), F32)],
        compiler_params=_params(1),
        name="mixer",
    )(x, hist, h0, nw0, wa, wdt, wu, cw, cb, dtb, alog, dvec, nw)


def _s5_kernel(u_ref, h0re_ref, h0im_ref, lr_ref, li_ref, bb_ref, cc_ref, d_ref, gw_ref, gb_ref,
               y_ref, hre_out_ref, him_out_ref, u_scr, y_scr, hs_scr, hre_scr, him_scr, *, tt, bsz):
    i = pl.program_id(0)

    @pl.when(i == 0)
    def _():
        hre_scr[...] = h0re_ref[...]
        him_scr[...] = h0im_ref[...]

    half = S5_BLOCK_STATE
    blocks = range(S5_BLOCKS)

    for b in range(bsz):
        ub = u_ref[b].astype(F32)
        for kb in blocks:
            u_scr[kb, pl.ds(b, tt, stride=bsz), :] = ub[:, kb * LANES:(kb + 1) * LANES]

    n_parts = max(1, tt // S5_PART_STEPS)
    part_steps = tt // n_parts
    part_rows = lambda p: slice(p * part_steps * bsz, (p + 1) * part_steps * bsz)
    state = [(hre_scr[kb], him_scr[kb]) for kb in blocks]
    lam = [(jnp.broadcast_to(lr_ref[kb], (bsz, half)), jnp.broadcast_to(li_ref[kb], (bsz, half))) for kb in blocks]

    def expand(p):
        rows = part_rows(p)
        for kb in blocks:
            hs_scr[kb, rows, :] = jnp.dot(u_scr[kb, rows, :].astype(BF16), bb_ref[kb], preferred_element_type=F32)
            yield

    def scan(p):
        for t in range(p * part_steps, (p + 1) * part_steps):
            rows = slice(t * bsz, (t + 1) * bsz)
            for kb in blocks:
                (lr, li), (hr, hi) = lam[kb], state[kb]
                nr = lr * hr - li * hi + hs_scr[kb, rows, 0:half]
                ni = lr * hi + li * hr + hs_scr[kb, rows, half:2 * half]
                hs_scr[kb, rows, 0:half] = nr
                hs_scr[kb, rows, half:2 * half] = ni
                state[kb] = (nr, ni)
            if t % 2 == 1:
                yield

    def project(p):
        rows = part_rows(p)
        ys = []
        for kb in blocks:
            ys.append(jnp.dot(hs_scr[kb, rows, :].astype(BF16), cc_ref[kb], preferred_element_type=F32)
                      + d_ref[:, kb * LANES:(kb + 1) * LANES] * u_scr[kb, rows, :])
            yield
        ges = [_gelu_tanh(y).astype(BF16) for y in ys]
        yield
        gls = []
        for kb in blocks:
            gls.append(jnp.dot(ges[kb], gw_ref[kb], preferred_element_type=F32) + gb_ref[kb])
            yield
        for kb in blocks:
            y_scr[kb, rows, :] = gls[kb][:, :LANES] * jax.nn.sigmoid(gls[kb][:, LANES:])
        yield

    _drain(expand(0))
    for p in range(n_parts):
        side = []
        if p + 1 < n_parts:
            side.append(expand(p + 1))
        if p >= 1:
            side.append(project(p - 1))
        _interleave(scan(p), side, every=2)
    _drain(project(n_parts - 1))
    for kb in blocks:
        hre_scr[kb], him_scr[kb] = state[kb]

    for b in range(bsz):
        y_ref[b] = jnp.concatenate([y_scr[kb, pl.ds(b, tt, stride=bsz), :] for kb in blocks], axis=-1).astype(BF16)

    @pl.when(i == pl.num_programs(0) - 1)
    def _():
        hre_out_ref[...] = hre_scr[...]
        him_out_ref[...] = him_scr[...]


def _s5(u, h0re, h0im, lr, li, bb, cc, dvec, gw, gb, tt):
    bsz, length, _ = u.shape
    grid = (length // tt,)
    st_shape = (S5_BLOCKS, bsz, S5_BLOCK_STATE)
    tok = pl.BlockSpec((bsz, tt, S5_WIDTH), lambda i: (0, i, 0))
    tb_scratch = pltpu.VMEM((S5_BLOCKS, tt * bsz, LANES), F32)
    return pl.pallas_call(
        functools.partial(_s5_kernel, tt=tt, bsz=bsz),
        grid=grid,
        in_specs=[tok, _const_spec(st_shape), _const_spec(st_shape), _const_spec(lr.shape), _const_spec(li.shape),
                  _const_spec(bb.shape), _const_spec(cc.shape), _const_spec(dvec.shape), _const_spec(gw.shape),
                  _const_spec(gb.shape)],
        out_specs=[tok, pl.BlockSpec(st_shape, lambda i: (0, 0, 0)), pl.BlockSpec(st_shape, lambda i: (0, 0, 0))],
        out_shape=[jax.ShapeDtypeStruct(u.shape, BF16),
                   jax.ShapeDtypeStruct(st_shape, F32), jax.ShapeDtypeStruct(st_shape, F32)],
        scratch_shapes=[tb_scratch, tb_scratch, pltpu.VMEM((S5_BLOCKS, tt * bsz, 2 * S5_BLOCK_STATE), F32),
                        pltpu.VMEM(st_shape, F32), pltpu.VMEM(st_shape, F32)],
        compiler_params=_params(1),
        name="s5",
    )(u, h0re, h0im, lr, li, bb, cc, dvec, gw, gb)


FFN_HEAD_BLOCKS = 4
FFN_MAIN_PHASES_PER_HEAD_PHASE = 4


def _ffn_kernel(x_ref, ys_ref, y5_ref, hist_ref, wout_ref, n1_ref, n2_ref, wup_ref, cw_ref, cb_ref, wdn_ref, n3_ref,
                y_ref, carry_ref, h_scr, hp_scr, permin_scr, permout_scr, act_scr, *, tm, nt, whole_seqs):
    s = pl.program_id(0)
    prev = s - 1
    wslot = lax.rem(s, 2)
    rslot = 1 - wslot
    seq_pos = lax.rem(prev + nt, nt)
    nj = tm // SUBLANES
    grp = SUBLANES
    n_slab = D_MODEL // LANES

    @pl.when(s == 0)
    def _():
        h_scr[1] = jnp.zeros(h_scr.shape[1:], F32)
        hp_scr[1] = jnp.zeros(hp_scr.shape[1:], BF16)
        carry_ref[...] = jnp.zeros(carry_ref.shape, F32)

    @pl.when((seq_pos == 0) & (s > 0))
    def _():
        carry_ref[0] = hist_ref[0]

    def tile_rows(ref, seg0, seg1):
        if whole_seqs:
            return ref[seg0:seg1].reshape((seg1 - seg0) * nj, ref.shape[-1])
        return ref[0, seg0 * nj:seg1 * nj, :]

    def head():
        mix = jnp.dot(tile_rows(ys_ref, 0, grp), wout_ref[0:SSD_WIDTH, :], preferred_element_type=F32)
        yield
        mix = mix + jnp.dot(tile_rows(y5_ref, 0, grp), wout_ref[SSD_WIDTH:, :], preferred_element_type=F32)
        yield
        segs = grp // FFN_HEAD_BLOCKS
        for k in range(FFN_HEAD_BLOCKS):
            rows = slice(k * segs * nj, (k + 1) * segs * nj)
            h = tile_rows(x_ref, k * segs, (k + 1) * segs) + _rms(mix[rows], n1_ref[...])
            h_scr[wslot, rows, :] = h
            hn = _rms(h, n2_ref[...])
            for sl in range(n_slab):
                for q in range(segs):
                    permin_scr[sl, pl.ds(k * segs + q, nj, stride=grp), :] = (
                        hn[q * nj:(q + 1) * nj, sl * LANES:(sl + 1) * LANES])
            yield
        for sl in range(n_slab):
            hp_scr[wslot, :, sl * LANES:(sl + 1) * LANES] = permin_scr[sl].astype(BF16)
            if sl % 2 == 1:
                yield

    first_sublane = lax.broadcasted_iota(jnp.int32, (grp, FFN_CHUNK), 0) == 0

    def conv_chunk(hp, c0):
        cols = slice(c0, c0 + FFN_CHUNK)
        up = jnp.dot(hp, wup_ref[:, cols], preferred_element_type=F32)

        def wrapped(group, fill_row):
            fill = jnp.broadcast_to(carry_ref[0, fill_row:fill_row + 1, cols], (grp, FFN_CHUNK))
            return jnp.where(first_sublane, fill, pltpu.roll(group, 1, 0))

        if whole_seqs:
            s2 = carry_ref[0, 0:grp, cols]
            s1 = carry_ref[0, grp:2 * grp, cols]
        else:
            s2 = wrapped(up[tm - 2 * grp:tm - grp], grp - 1)
            s1 = wrapped(up[tm - grp:tm], 2 * grp - 1)
        carry_ref[0, :, cols] = up[tm - 2 * grp:tm]
        prev1 = jnp.concatenate([s1, up[:tm - grp]], axis=0)
        prev2 = jnp.concatenate([s2, s1, up[:tm - 2 * grp]], axis=0)
        return (cb_ref[:, cols] + up * cw_ref[2:3, cols] + prev1 * cw_ref[1:2, cols] + prev2 * cw_ref[0:1, cols])

    def ffn():
        hp = hp_scr[rslot]
        for c in range(D_FF // FFN_CHUNK):
            gate = conv_chunk(hp, c * FFN_CHUNK)
            yield
            val = conv_chunk(hp, D_FF + c * FFN_CHUNK)
            yield
            act_scr[:, c * FFN_CHUNK:(c + 1) * FFN_CHUNK] = (_gelu_tanh(gate) * val).astype(BF16)
            yield
        out = _rms(jnp.dot(act_scr[...], wdn_ref[...], preferred_element_type=F32), n3_ref[...])
        yield
        for sl in range(n_slab):
            permout_scr[sl] = out[:, sl * LANES:(sl + 1) * LANES]
        yield
        for sl in range(n_slab):
            cols = slice(sl * LANES, (sl + 1) * LANES)
            nat = jnp.concatenate([permout_scr[sl, pl.ds(seg, nj, stride=grp), :] for seg in range(grp)], axis=0)
            y = h_scr[rslot, :, cols] + nat
            if whole_seqs:
                y_ref[:, :, cols] = y.reshape(grp, nj, LANES)
            else:
                y_ref[0, :, cols] = y
            if sl % 2 == 1:
                yield

    _interleave(ffn(), [head()], every=FFN_MAIN_PHASES_PER_HEAD_PHASE)


def _ffn(x, ys, y5, ffn_hist, wout, n1, n2, wup, cw, cb, wdn, n3, tm):
    bsz, length, _ = x.shape
    assert FFN_CONV == 3 and SUBLANES % FFN_HEAD_BLOCKS == 0
    whole_seqs = SUBLANES * length <= tm
    last = SUBLANES - 1
    if whole_seqs:
        assert bsz % SUBLANES == 0 and length % SUBLANES == 0 and length >= 2
        blk, nt, tm = (SUBLANES, length), 1, SUBLANES * length
        n_tiles = bsz // SUBLANES
        hist = jnp.transpose(ffn_hist.reshape(n_tiles, SUBLANES, 2, 2 * D_FF), (0, 2, 1, 3)).reshape(
            n_tiles, 2 * SUBLANES, 2 * D_FF)
    else:
        assert length % tm == 0 and tm % SUBLANES == 0 and tm // SUBLANES >= 2
        blk, nt = (1, tm), length // tm
        n_tiles = bsz * nt
        hist = jnp.zeros((bsz, 2 * SUBLANES, 2 * D_FF), F32).at[:, last::SUBLANES, :].set(ffn_hist)
    cur = lambda s: jnp.minimum(s, n_tiles - 1)
    prv = lambda s: jnp.maximum(s - 1, 0)
    tok = lambda w, which: pl.BlockSpec(blk + (w,), lambda s: (which(s) // nt, which(s) % nt, 0))
    per_seq = pl.BlockSpec((1,) + hist.shape[1:], lambda s: (prv(s) // nt, 0, 0))
    slabs = pltpu.VMEM((D_MODEL // LANES, tm, LANES), F32)
    y, carry = pl.pallas_call(
        functools.partial(_ffn_kernel, tm=tm, nt=nt, whole_seqs=whole_seqs),
        grid=(n_tiles + 1,),
        in_specs=[tok(D_MODEL, cur), tok(SSD_WIDTH, cur), tok(S5_WIDTH, cur), per_seq,
                  _const_spec(wout.shape), _const_spec(n1.shape), _const_spec(n2.shape), _const_spec(wup.shape),
                  _const_spec(cw.shape), _const_spec(cb.shape), _const_spec(wdn.shape), _const_spec(n3.shape)],
        out_specs=[tok(D_MODEL, prv), per_seq],
        out_shape=[jax.ShapeDtypeStruct((bsz, length, D_MODEL), F32), jax.ShapeDtypeStruct(hist.shape, F32)],
        scratch_shapes=[pltpu.VMEM((2, tm, D_MODEL), F32), pltpu.VMEM((2, tm, D_MODEL), BF16), slabs, slabs,
                        pltpu.VMEM((tm, D_FF), BF16)],
        compiler_params=_params(1),
        name="outffn",
    )(x, ys, y5, hist, wout, n1, n2, wup, cw, cb, wdn, n3)
    if whole_seqs:
        new_hist = jnp.transpose(carry.reshape(n_tiles, 2, SUBLANES, 2 * D_FF), (0, 2, 1, 3)).reshape(
            bsz, FFN_CONV - 1, 2 * D_FF)
    else:
        new_hist = carry[:, last::SUBLANES, :]
    return y, new_hist


def _block_diag(w):
    _, r, c = w.shape
    w4 = w.reshape(S5_BLOCKS, S5_GPB, r, c)
    eye = jnp.eye(S5_GPB, dtype=w.dtype)
    return jnp.einsum("kgrc,gh->kgrhc", w4, eye).reshape(S5_BLOCKS, S5_GPB * r, S5_GPB * c)


def _prep_weights(pre_mix_norm_w, w_in, ssd_conv_w, ssd_conv_b, ssd_dt_bias, ssd_a_log, ssd_d, ssd_norm_w,
                  s5_lambda_re, s5_lambda_im, s5_log_dt, s5_b_re, s5_b_im, s5_c_re, s5_c_im, s5_d,
                  s5_glu_w, s5_glu_b, w_out, post_mix_norm_w, pre_ffn_norm_w, w_up, ffn_conv_w, ffn_conv_b,
                  w_down, post_ffn_norm_w):
    o_dt = SSD_WIDTH + SSD_CONV_DIM
    o_u = o_dt + SSD_HEADS
    pad_heads = lambda v: jnp.pad(v, (0, LANES - SSD_HEADS)).reshape(1, LANES)
    w = dict(
        nw0=pre_mix_norm_w.reshape(1, D_MODEL),
        wa=w_in[:, :o_dt].astype(BF16),
        wdt=jnp.pad(w_in[:, o_dt:o_u], ((0, 0), (0, LANES - SSD_HEADS))).astype(BF16),
        wu=w_in[:, o_u:].astype(BF16),
        cw=ssd_conv_w, cb=ssd_conv_b.reshape(1, SSD_CONV_DIM),
        dtb=pad_heads(ssd_dt_bias), alog=pad_heads(ssd_a_log),
        dssd=jnp.repeat(ssd_d, SSD_HEAD_DIM).reshape(1, SSD_WIDTH),
        nssd=ssd_norm_w.reshape(1, SSD_WIDTH),
        wout=w_out.astype(BF16), n1=post_mix_norm_w.reshape(1, D_MODEL), n2=pre_ffn_norm_w.reshape(1, D_MODEL),
        wup=w_up.astype(BF16), fcw=ffn_conv_w, fcb=ffn_conv_b.reshape(1, 2 * D_FF),
        wdn=w_down.astype(BF16), n3=post_ffn_norm_w.reshape(1, D_MODEL),
    )
    dt = jnp.exp(s5_log_dt)[:, None]
    mag = jnp.exp(s5_lambda_re * dt)
    ang = s5_lambda_im * dt
    lb_re = mag * jnp.cos(ang)
    lb_im = mag * jnp.sin(ang)
    den = s5_lambda_re * s5_lambda_re + s5_lambda_im * s5_lambda_im
    q_re = ((lb_re - 1) * s5_lambda_re + lb_im * s5_lambda_im) / den
    q_im = (lb_im * s5_lambda_re - (lb_re - 1) * s5_lambda_im) / den
    bb_re = q_re[..., None] * s5_b_re - q_im[..., None] * s5_b_im
    bb_im = q_re[..., None] * s5_b_im + q_im[..., None] * s5_b_re
    to_in = lambda m: _block_diag(jnp.swapaxes(m, 1, 2))
    w["bb"] = jnp.concatenate([to_in(bb_re), to_in(bb_im)], axis=-1).astype(BF16)
    to_out = lambda m: _block_diag(jnp.swapaxes(m, 1, 2))
    w["cc"] = jnp.concatenate([to_out(s5_c_re), to_out(-s5_c_im)], axis=1).astype(BF16)
    w["lr"] = lb_re.reshape(S5_BLOCKS, 1, S5_BLOCK_STATE)
    w["li"] = lb_im.reshape(S5_BLOCKS, 1, S5_BLOCK_STATE)
    w["d5"] = s5_d.reshape(1, S5_WIDTH)
    w["gw"] = jnp.concatenate([_block_diag(s5_glu_w[..., :S5_GROUP_CH]), _block_diag(s5_glu_w[..., S5_GROUP_CH:])],
                              axis=-1).astype(BF16)
    gb = lambda v: v.reshape(S5_BLOCKS, 1, LANES)
    w["gb"] = jnp.concatenate([gb(s5_glu_b[:, :S5_GROUP_CH]), gb(s5_glu_b[:, S5_GROUP_CH:])], axis=-1)
    return w


def _hist_tile(hist):
    return jnp.pad(hist, ((0, 0), (HIST_ROWS - hist.shape[1], 0), (0, 0)))


def _layer(x, conv_hist, ssd_h0, s5_re, s5_im, ffn_hist, w, *, tm_in, tm, q, tt, nb, nc):
    bsz, length, _ = x.shape
    h0 = jnp.transpose(ssd_h0, (0, 3, 1, 2)).reshape(bsz, SSD_STATE, SSD_WIDTH)
    if length % tm == 0 and length // tm >= 2:
        u, y_ssd, tail, st = _mixer(x, _hist_tile(conv_hist), h0, w["nw0"], w["wa"], w["wdt"], w["wu"], w["cw"],
                                    w["cb"], w["dtb"], w["alog"], w["dssd"], w["nssd"], tm, q, nb)
        new_conv = tail[:, HIST_ROWS - (SSD_CONV - 1):, :]
    else:
        z, xbc, dt, u = _inproj(x, w["nw0"], w["wa"], w["wdt"], w["wu"], tm_in)
        new_conv = xbc[:, length - (SSD_CONV - 1):, :].astype(F32)
        lpad = -length % q
        if lpad:
            assert length < q, "a padded sequence must fit one SSD chunk"
            padt = lambda a: jnp.pad(a, ((0, 0), (0, lpad), (0, 0)))
            z, xbc, dt = padt(z), padt(xbc), padt(dt)
        y_ssd, st = _ssd(z, xbc, dt, _hist_tile(conv_hist), h0, w["cw"], w["cb"], w["dtb"], w["alog"], w["dssd"],
                         w["nssd"], q, length if lpad else q, nb, nc)
        y_ssd = y_ssd[:, :length]
    new_ssd = jnp.transpose(st.reshape(bsz, SSD_STATE, SSD_HEADS, SSD_HEAD_DIM), (0, 2, 3, 1))

    to_blocks = lambda s: jnp.transpose(s.reshape(bsz, S5_BLOCKS, S5_BLOCK_STATE), (1, 0, 2))
    from_blocks = lambda s: jnp.transpose(s, (1, 0, 2)).reshape(bsz, S5_GROUPS, S5_STATE)
    y5, hre, him = _s5(u, to_blocks(s5_re), to_blocks(s5_im), w["lr"], w["li"], w["bb"], w["cc"], w["d5"],
                       w["gw"], w["gb"], tt)

    y, new_ffn = _ffn(x, y_ssd, y5, ffn_hist, w["wout"], w["n1"], w["n2"], w["wup"], w["fcw"], w["fcb"],
                      w["wdn"], w["n3"], tm)
    return y, new_conv, new_ssd, from_blocks(hre), from_blocks(him), new_ffn


def _tiles(length):
    tm_in = min(length, 1024)
    q = 128
    nc = max(1, min(SSD_CHUNKS_PER_STEP, length // q))
    tt = min(length, 128)
    assert length % tm_in == 0 and length % tt == 0 and length >= SSD_CONV - 1
    return dict(tm_in=tm_in, tm=ROW_TILE, q=q, tt=tt, nb=SSD_SEQS_PER_STEP, nc=nc)


def kernel(x_prompt, x_sample, cache_ssd_conv, state_ssd, state_s5_re, state_s5_im, cache_ffn_conv, pre_mix_norm_w, w_in, ssd_conv_w, ssd_conv_b, ssd_dt_bias, ssd_a_log, ssd_d, ssd_norm_w, s5_lambda_re, s5_lambda_im, s5_log_dt, s5_b_re, s5_b_im, s5_c_re, s5_c_im, s5_d, s5_glu_w, s5_glu_b, w_out, post_mix_norm_w, pre_ffn_norm_w, w_up, ffn_conv_w, ffn_conv_b, w_down, post_ffn_norm_w):
    depth = w_in.shape[0]
    bsz = x_prompt.shape[0]
    dtp = x_prompt.dtype
    layer_params = (pre_mix_norm_w, w_in, ssd_conv_w, ssd_conv_b, ssd_dt_bias, ssd_a_log, ssd_d, ssd_norm_w,
                    s5_lambda_re, s5_lambda_im, s5_log_dt, s5_b_re, s5_b_im, s5_c_re, s5_c_im, s5_d,
                    s5_glu_w, s5_glu_b, w_out, post_mix_norm_w, pre_ffn_norm_w, w_up, ffn_conv_w, ffn_conv_b,
                    w_down, post_ffn_norm_w)
    y_prompt, y_sample = x_prompt, x_sample
    prompt_states, sample_states = [], []
    for l in range(depth):
        w = _prep_weights(*(p[l] for p in layer_params))
        y_prompt, *ps = _layer(
            y_prompt,
            jnp.zeros((bsz, SSD_CONV - 1, SSD_CONV_DIM), dtp),
            jnp.zeros((bsz, SSD_HEADS, SSD_HEAD_DIM, SSD_STATE), dtp),
            jnp.zeros((bsz, S5_GROUPS, S5_STATE), dtp),
            jnp.zeros((bsz, S5_GROUPS, S5_STATE), dtp),
            jnp.zeros((bsz, FFN_CONV - 1, 2 * D_FF), dtp),
            w, **_tiles(y_prompt.shape[1]))
        y_sample, *ss = _layer(
            y_sample, cache_ssd_conv[l], state_ssd[l], state_s5_re[l], state_s5_im[l], cache_ffn_conv[l],
            w, **_tiles(y_sample.shape[1]))
        prompt_states.append(ps)
        sample_states.append(ss)
    stack = lambda states: tuple(jnp.stack([s[k] for s in states]) for k in range(5))
    return (y_prompt, y_sample) + stack(prompt_states) + stack(sample_states)
```

```python
import functools
import math
from typing import Callable, NamedTuple

import jax
import jax.numpy as jnp
from jax import lax
from jax.experimental import pallas as pl
from jax.experimental.pallas import tpu as pltpu

F32 = jnp.float32
BF16 = jnp.bfloat16

D_MODEL = 1024
SSD_WIDTH = 512
SSD_HEAD_DIM = 64
SSD_HEADS = 8
SSD_GROUPS = 2
SSD_HPG = SSD_HEADS // SSD_GROUPS
SSD_GROUP_W = SSD_WIDTH // SSD_GROUPS
SSD_STATE = 128
SSD_CONV = 4
SSD_CONV_DIM = SSD_WIDTH + 2 * SSD_GROUPS * SSD_STATE
S5_WIDTH = 512
S5_GROUP_CH = 16
S5_GROUPS = 32
S5_STATE = 64
S5_BLOCKS = 4
S5_GPB = S5_GROUPS // S5_BLOCKS
S5_BLOCK_STATE = S5_GPB * S5_STATE
S5_PART_STEPS = 64
D_FF = 2816
FFN_CONV = 3
EPS = 1e-6

LANES = 128
SUBLANES = 8
HIST_ROWS = SUBLANES
VMEM_LIMIT = 56 * 1024 * 1024

ROW_TILE = 512
FFN_CHUNK = 256
SSD_SEQS_PER_STEP = 2
SSD_CHUNKS_PER_STEP = 4


def _rms(x, w):
    ms = jnp.mean(x * x, axis=-1, keepdims=True)
    return x * lax.rsqrt(ms + EPS) * w


def _gelu_tanh(x):
    c = math.sqrt(2.0 / math.pi)
    hx = 0.5 * x
    return hx + hx * jnp.tanh(x * (c + (c * 0.044715) * (x * x)))


def _softplus(x):
    return jnp.maximum(x, 0.0) + jnp.log1p(jnp.exp(-jnp.abs(x)))


def _causal_conv(ext, cur, w_ref, b_ref, c0, rows, taps):
    cols = slice(c0, c0 + LANES)
    ext[HIST_ROWS:HIST_ROWS + rows, :] = cur
    out = b_ref[:, cols] + cur * w_ref[taps - 1:taps, cols]
    for k in range(1, taps):
        out = out + ext[pl.ds(HIST_ROWS - k, rows), :] * w_ref[taps - 1 - k:taps - k, cols]
    ext[0:HIST_ROWS, :] = ext[rows:rows + HIST_ROWS, :]
    return out


def _drain(gen):
    for _ in gen:
        pass


def _interleave(main, sides, every):
    sides = list(sides)
    turn = 0
    for n, _ in enumerate(main, 1):
        while sides and turn < n / every:
            side = sides[turn % len(sides)]
            try:
                next(side)
                turn += 1
            except StopIteration:
                sides.remove(side)
    for side in sides:
        _drain(side)


def _const_spec(shape):
    nd = len(shape)
    return pl.BlockSpec(shape, lambda *_: (0,) * nd, pipeline_mode=pl.Buffered(1))


def _params(n_axes):
    return pltpu.CompilerParams(dimension_semantics=("arbitrary",) * n_axes, vmem_limit_bytes=VMEM_LIMIT)


def _inproj_kernel(x_ref, nw_ref, wa_ref, wdt_ref, wu_ref, z_ref, xbc_ref, dt_ref, u_ref):
    nb, tm, _ = x_ref.shape
    put = lambda ref, v: ref.__setitem__(..., v.reshape(ref.shape))
    xn = _rms(x_ref[...].reshape(nb * tm, D_MODEL), nw_ref[...]).astype(BF16)
    a = jnp.dot(xn, wa_ref[...], preferred_element_type=F32)
    put(z_ref, a[:, :SSD_WIDTH].astype(BF16))
    put(xbc_ref, a[:, SSD_WIDTH:].astype(BF16))
    put(dt_ref, jnp.dot(xn, wdt_ref[...], preferred_element_type=F32))
    put(u_ref, jnp.dot(xn, wu_ref[...], preferred_element_type=F32).astype(BF16))


def _inproj(x, nw, wa, wdt, wu, tm):
    bsz, length, _ = x.shape
    nb = max(1, min(bsz, tm // length))
    tm = min(tm, length)
    grid = (bsz // nb, length // tm)
    tok = lambda w, dt: (jax.ShapeDtypeStruct((bsz, length, w), dt), pl.BlockSpec((nb, tm, w), lambda b, i: (b, i, 0)))
    outs = [tok(SSD_WIDTH, BF16), tok(SSD_CONV_DIM, BF16), tok(LANES, F32), tok(S5_WIDTH, BF16)]
    return pl.pallas_call(
        _inproj_kernel,
        grid=grid,
        in_specs=[pl.BlockSpec((nb, tm, D_MODEL), lambda b, i: (b, i, 0)),
                  _const_spec(nw.shape), _const_spec(wa.shape), _const_spec(wdt.shape), _const_spec(wu.shape)],
        out_specs=[o[1] for o in outs],
        out_shape=[o[0] for o in outs],
        compiler_params=_params(2),
        name="inproj",
    )(x, nw, wa, wdt, wu)


def _split3(v):
    hi = v.astype(BF16)
    r1 = v - hi.astype(F32)
    mid = r1.astype(BF16)
    lo = (r1 - mid.astype(F32)).astype(BF16)
    return hi, mid, lo


def _head_tiles(a):
    q = a.shape[0]
    lane = lax.broadcasted_iota(jnp.int32, (q, LANES), 1)
    tiles = []
    for j in range(SSD_HEADS * SSD_HEAD_DIM // LANES):
        even = jnp.broadcast_to(a[:, 2 * j:2 * j + 1], (q, LANES))
        odd = jnp.broadcast_to(a[:, 2 * j + 1:2 * j + 2], (q, LANES))
        tiles.append(jnp.where(lane < SSD_HEAD_DIM, even, odd))
    return tiles


def _ssd_kernel(z_ref, xbc_ref, dt_ref, hist_ref, h0_ref, cw_ref, cb_ref, dtb_ref, alog_ref, d_ref, nw_ref,
                y_ref, st_out_ref, ext_scr, st_scr, *, q, valid, nb, nc):
    t = pl.program_id(1)
    n_slab = SSD_CONV_DIM // LANES

    @pl.when(t == 0)
    def _():
        for s in range(nb):
            for j in range(n_slab):
                ext_scr[s * n_slab + j, 0:HIST_ROWS, :] = hist_ref[s, :, j * LANES:(j + 1) * LANES]
        st_scr[...] = h0_ref[...]

    def io(s):
        def put_y(rows, v):
            y_ref[s, rows, :] = v
        return SsdIO(z=lambda rows: z_ref[s, rows, :], xbc=lambda rows: xbc_ref[s, rows, :],
                     dt=lambda rows: dt_ref[s, rows, :], put_y=put_y)

    for c in range(nc):
        chains = [_ssd_chunk(s, slice(c * q, (c + 1) * q), io(s), cw_ref, cb_ref, dtb_ref, alog_ref,
                             d_ref, nw_ref, ext_scr, st_scr, q=q, valid=valid) for s in range(nb)]
        for _ in zip(*chains):
            pass

    @pl.when(t == pl.num_programs(1) - 1)
    def _():
        st_out_ref[...] = st_scr[...]


class SsdIO(NamedTuple):
    z: Callable
    xbc: Callable
    dt: Callable
    put_y: Callable


def _ssd_chunk(s, rows, io, cw_ref, cb_ref, dtb_ref, alog_ref, d_ref, nw_ref, ext_scr, st_scr, *, q, valid):
    n_slab = SSD_CONV_DIM // LANES
    st_scr = st_scr.at[s]
    xraw = io.xbc(rows).astype(F32)
    slabs = []
    for j in range(n_slab):
        conv = _causal_conv(ext_scr.at[s * n_slab + j], xraw[:, j * LANES:(j + 1) * LANES], cw_ref, cb_ref,
                            j * LANES, q, SSD_CONV)
        slabs.append(conv * jax.nn.sigmoid(conv))
        if j % 2 == 1:
            yield
    xact = jnp.concatenate(slabs, axis=-1)
    xs = xact[:, :SSD_WIDTH]
    xs_bf = xs.astype(BF16)
    yield

    dt = _softplus(io.dt(rows) + dtb_ref[...])
    row = lax.broadcasted_iota(jnp.int32, (q, q), 0)
    col = lax.broadcasted_iota(jnp.int32, (q, q), 1)
    causal = row >= col
    if valid < q:
        dt = jnp.where(lax.broadcasted_iota(jnp.int32, dt.shape, 0) < valid, dt, 0.0)
    a = -jnp.exp(alog_ref[...])
    tril = jnp.where(causal, 1.0, 0.0).astype(BF16)
    cs = sum(jnp.dot(tril, part, preferred_element_type=F32) for part in _split3(dt * a))
    cs_t = cs.T
    dt_t = dt.T
    yield
    cs_tiles = _head_tiles(cs)
    dt_tiles = _head_tiles(dt)
    cat = lambda tiles: jnp.concatenate(tiles, axis=-1)
    decay_in = cat([jnp.exp(c) for c in cs_tiles])
    w_end = cat([jnp.exp(c[q - 1:q, :] - c) * d for c, d in zip(cs_tiles, dt_tiles)])
    chunk_decay = cat([jnp.exp(c[q - 1:q, :]) for c in cs_tiles])
    yield

    lane = lax.broadcasted_iota(jnp.int32, (1, SSD_GROUP_W), 1)
    ys = []
    for g in range(SSD_GROUPS):
        b_g = xact[:, SSD_WIDTH + g * SSD_STATE:SSD_WIDTH + (g + 1) * SSD_STATE].astype(BF16)
        c_off = SSD_WIDTH + SSD_GROUPS * SSD_STATE
        c_g = xact[:, c_off + g * SSD_STATE:c_off + (g + 1) * SSD_STATE].astype(BF16)
        gs = slice(g * SSD_GROUP_W, (g + 1) * SSD_GROUP_W)
        x_g = xs_bf[:, gs]
        scores = lax.dot_general(c_g, b_g, (((1,), (1,)), ((), ())), preferred_element_type=F32)
        st_g = st_scr[:, gs]
        y_g = jnp.dot(c_g, st_g.astype(BF16), preferred_element_type=F32) * decay_in[:, gs]
        yield
        for j in range(SSD_HPG):
            h = g * SSD_HPG + j
            seg = cs[:, h:h + 1] - cs_t[h:h + 1, :]
            lmat = jnp.where(causal, jnp.exp(seg), 0.0) * dt_t[h:h + 1, :]
            m = (scores * lmat).astype(BF16)
            in_head = (lane >= j * SSD_HEAD_DIM) & (lane < (j + 1) * SSD_HEAD_DIM)
            x_h = jnp.where(in_head, x_g, jnp.zeros_like(x_g))
            y_g = y_g + jnp.dot(m, x_h, preferred_element_type=F32)
            yield
        ys.append(y_g)
        wx = (xs[:, gs] * w_end[:, gs]).astype(BF16)
        new = lax.dot_general(b_g, wx, (((0,), (0,)), ((), ())), preferred_element_type=F32)
        st_scr[:, gs] = st_g * chunk_decay[:, gs] + new
        yield
    y = jnp.concatenate(ys, axis=-1) + d_ref[...] * xs

    zf = io.z(rows).astype(F32)
    gz = y * (zf * jax.nn.sigmoid(zf))
    yield
    nw = nw_ref[...]
    outs = []
    for g in range(SSD_GROUPS):
        gs = slice(g * SSD_GROUP_W, (g + 1) * SSD_GROUP_W)
        outs.append(_rms(gz[:, gs], nw[:, gs]))
    io.put_y(rows, jnp.concatenate(outs, axis=-1).astype(BF16))
    yield


def _ssd(z, xbc, dt, hist, h0, cw, cb, dtb, alog, dvec, nw, q, valid, nb, nc):
    bsz, length, _ = z.shape
    assert bsz % nb == 0 and length % (nc * q) == 0
    grid = (bsz // nb, length // (nc * q))
    tok = lambda w: pl.BlockSpec((nb, nc * q, w), lambda b, i: (b, i, 0))
    per_b = lambda r, w: pl.BlockSpec((nb, r, w), lambda b, i: (b, 0, 0))
    return pl.pallas_call(
        functools.partial(_ssd_kernel, q=q, valid=valid, nb=nb, nc=nc),
        grid=grid,
        in_specs=[tok(SSD_WIDTH), tok(SSD_CONV_DIM), tok(LANES), per_b(HIST_ROWS, SSD_CONV_DIM),
                  per_b(SSD_STATE, SSD_WIDTH), _const_spec(cw.shape), _const_spec(cb.shape), _const_spec(dtb.shape),
                  _const_spec(alog.shape), _const_spec(dvec.shape), _const_spec(nw.shape)],
        out_specs=[tok(SSD_WIDTH), per_b(SSD_STATE, SSD_WIDTH)],
        out_shape=[jax.ShapeDtypeStruct((bsz, length, SSD_WIDTH), BF16),
                   jax.ShapeDtypeStruct((bsz, SSD_STATE, SSD_WIDTH), F32)],
        scratch_shapes=[pltpu.VMEM((nb * SSD_CONV_DIM // LANES, q + HIST_ROWS, LANES), F32),
                        pltpu.VMEM((nb, SSD_STATE, SSD_WIDTH), F32)],
        compiler_params=_params(2),
        name="ssd",
    )(z, xbc, dt, hist, h0, cw, cb, dtb, alog, dvec, nw)


MIX_COLS = 256
MIX_PHASES_PER_PIECE = 3


def _inproj_phases(x_ref, nw_ref, wa_ref, wdt_ref, wu_ref, xn_scr, zx_scr, dtp_scr, u_ref, slot, *, nb, tm):
    for j in range(nb):
        xn_scr[j * tm:(j + 1) * tm, :] = _rms(x_ref[j], nw_ref[...]).astype(BF16)
        yield
    for c0 in range(0, SSD_WIDTH + SSD_CONV_DIM, MIX_COLS):
        cols = slice(c0, c0 + MIX_COLS)
        for j in range(nb):
            rows = slice(j * tm, (j + 1) * tm)
            zx_scr[slot, rows, cols] = jnp.dot(xn_scr[rows, :], wa_ref[:, cols], preferred_element_type=F32)
            yield
    for j in range(nb):
        rows = slice(j * tm, (j + 1) * tm)
        dtp_scr[slot, rows, :] = jnp.dot(xn_scr[rows, :], wdt_ref[...], preferred_element_type=F32)
        yield
    for c0 in range(0, S5_WIDTH, MIX_COLS):
        cols = slice(c0, c0 + MIX_COLS)
        for j in range(nb):
            rows = slice(j * tm, (j + 1) * tm)
            u_ref[j, :, cols] = jnp.dot(xn_scr[rows, :], wu_ref[:, cols], preferred_element_type=F32).astype(BF16)
            yield


def _mixer_kernel(x_ref, hist_ref, h0_ref, nw0_ref, wa_ref, wdt_ref, wu_ref, cw_ref, cb_ref, dtb_ref, alog_ref, d_ref,
                  nw_ref, u_ref, y_ref, tail_ref, st_out_ref, xn_scr, zx_scr, dtp_scr, ext_scr, st_scr,
                  *, tm, q, nb, nt):
    s = pl.program_id(0)
    prev = s - 1
    wslot = lax.rem(s, 2)
    rslot = 1 - wslot
    n_slab = SSD_CONV_DIM // LANES
    seq_pos = lax.rem(prev + nt, nt)

    @pl.when(s == 0)
    def _():
        zx_scr[1] = jnp.zeros(zx_scr.shape[1:], F32)
        dtp_scr[1] = jnp.zeros(dtp_scr.shape[1:], F32)
        ext_scr[...] = jnp.zeros(ext_scr.shape, F32)
        st_scr[...] = jnp.zeros(st_scr.shape, F32)

    @pl.when((seq_pos == 0) & (s > 0))
    def _():
        for j in range(nb):
            for k in range(n_slab):
                ext_scr[j * n_slab + k, 0:HIST_ROWS, :] = hist_ref[j, :, k * LANES:(k + 1) * LANES]
        st_scr[...] = h0_ref[...]

    def io(j):
        base = j * tm
        at = lambda rows: pl.ds(base + rows.start, rows.stop - rows.start)

        def put_y(rows, v):
            y_ref[j, rows, :] = v
        return SsdIO(z=lambda rows: zx_scr[rslot, at(rows), 0:SSD_WIDTH],
                     xbc=lambda rows: zx_scr[rslot, at(rows), SSD_WIDTH:SSD_WIDTH + SSD_CONV_DIM],
                     dt=lambda rows: dtp_scr[rslot, at(rows), :], put_y=put_y)

    inproj = _inproj_phases(x_ref, nw0_ref, wa_ref, wdt_ref, wu_ref, xn_scr, zx_scr, dtp_scr, u_ref, wslot,
                            nb=nb, tm=tm)
    next(inproj)
    phase = 0
    for c in range(tm // q):
        chains = [_ssd_chunk(j, slice(c * q, (c + 1) * q), io(j), cw_ref, cb_ref, dtb_ref, alog_ref, d_ref, nw_ref,
                             ext_scr, st_scr, q=q, valid=q) for j in range(nb)]
        for _ in zip(*chains):
            phase += 1
            if phase % MIX_PHASES_PER_PIECE == 0:
                next(inproj, None)
    for _ in inproj:
        pass

    @pl.when((seq_pos == nt - 1) & (s > 0))
    def _():
        st_out_ref[...] = st_scr[...]
        for j in range(nb):
            for k in range(n_slab):
                tail_ref[j, :, k * LANES:(k + 1) * LANES] = ext_scr[j * n_slab + k, 0:HIST_ROWS, :]


def _mixer(x, hist, h0, nw0, wa, wdt, wu, cw, cb, dtb, alog, dvec, nw, tm, q, nb):
    bsz, length, _ = x.shape
    assert bsz % nb == 0 and length % tm == 0 and tm % q == 0
    nt = length // tm
    n_tiles = (bsz // nb) * nt
    cur = lambda s: jnp.minimum(s, n_tiles - 1)
    prv = lambda s: jnp.maximum(s - 1, 0)
    tile = lambda w, which: pl.BlockSpec((nb, tm, w), lambda s: (which(s) // nt, which(s) % nt, 0))
    per_seq = lambda r, w: pl.BlockSpec((nb, r, w), lambda s: (prv(s) // nt, 0, 0))
    rows = nb * tm
    return pl.pallas_call(
        functools.partial(_mixer_kernel, tm=tm, q=q, nb=nb, nt=nt),
        grid=(n_tiles + 1,),
        in_specs=[tile(D_MODEL, cur), per_seq(HIST_ROWS, SSD_CONV_DIM), per_seq(SSD_STATE, SSD_WIDTH),
                  _const_spec(nw0.shape), _const_spec(wa.shape), _const_spec(wdt.shape), _const_spec(wu.shape),
                  _const_spec(cw.shape), _const_spec(cb.shape), _const_spec(dtb.shape), _const_spec(alog.shape),
                  _const_spec(dvec.shape), _const_spec(nw.shape)],
        out_specs=[tile(S5_WIDTH, cur), tile(SSD_WIDTH, prv), per_seq(HIST_ROWS, SSD_CONV_DIM),
                   per_seq(SSD_STATE, SSD_WIDTH)],
        out_shape=[jax.ShapeDtypeStruct((bsz, length, S5_WIDTH), BF16),
                   jax.ShapeDtypeStruct((bsz, length, SSD_WIDTH), BF16),
                   jax.ShapeDtypeStruct((bsz, HIST_ROWS, SSD_CONV_DIM), F32),
                   jax.ShapeDtypeStruct((bsz, SSD_STATE, SSD_WIDTH), F32)],
        scratch_shapes=[pltpu.VMEM((rows, D_MODEL), BF16),
                        pltpu.VMEM((2, rows, SSD_WIDTH + SSD_CONV_DIM), F32),
                        pltpu.VMEM((2, rows, LANES), F32),
                        pltpu.VMEM((nb * SSD_CONV_DIM // LANES, q + HIST_ROWS, LANES), F32),
                        pltpu.VMEM((nb, SSD_STATE, SSD_WIDTH), F32)],
        compiler_params=_params(1),
        name="mixer",
    )(x, hist, h0, nw0, wa, wdt, wu, cw, cb, dtb, alog, dvec, nw)


def _s5_kernel(u_ref, h0re_ref, h0im_ref, lr_ref, li_ref, bb_ref, cc_ref, d_ref, gw_ref, gb_ref,
               y_ref, hre_out_ref, him_out_ref, u_scr, y_scr, hs_scr, hre_scr, him_scr, *, tt, bsz):
    i = pl.program_id(0)

    @pl.when(i == 0)
    def _():
        hre_scr[...] = h0re_ref[...]
        him_scr[...] = h0im_ref[...]

    half = S5_BLOCK_STATE
    blocks = range(S5_BLOCKS)

    for b in range(bsz):
        ub = u_ref[b].astype(F32)
        for kb in blocks:
            u_scr[kb, pl.ds(b, tt, stride=bsz), :] = ub[:, kb * LANES:(kb + 1) * LANES]

    n_parts = max(1, tt // S5_PART_STEPS)
    part_steps = tt // n_parts
    part_rows = lambda p: slice(p * part_steps * bsz, (p + 1) * part_steps * bsz)
    state = [(hre_scr[kb], him_scr[kb]) for kb in blocks]
    lam = [(jnp.broadcast_to(lr_ref[kb], (bsz, half)), jnp.broadcast_to(li_ref[kb], (bsz, half))) for kb in blocks]

    def expand(p):
        rows = part_rows(p)
        for kb in blocks:
            hs_scr[kb, rows, :] = jnp.dot(u_scr[kb, rows, :].astype(BF16), bb_ref[kb], preferred_element_type=F32)
            yield

    def scan(p):
        for t in range(p * part_steps, (p + 1) * part_steps):
            rows = slice(t * bsz, (t + 1) * bsz)
            for kb in blocks:
                (lr, li), (hr, hi) = lam[kb], state[kb]
                nr = lr * hr - li * hi + hs_scr[kb, rows, 0:half]
                ni = lr * hi + li * hr + hs_scr[kb, rows, half:2 * half]
                hs_scr[kb, rows, 0:half] = nr
                hs_scr[kb, rows, half:2 * half] = ni
                state[kb] = (nr, ni)
            if t % 2 == 1:
                yield

    def project(p):
        rows = part_rows(p)
        ys = []
        for kb in blocks:
            ys.append(jnp.dot(hs_scr[kb, rows, :].astype(BF16), cc_ref[kb], preferred_element_type=F32)
                      + d_ref[:, kb * LANES:(kb + 1) * LANES] * u_scr[kb, rows, :])
            yield
        ges = [_gelu_tanh(y).astype(BF16) for y in ys]
        yield
        gls = []
        for kb in blocks:
            gls.append(jnp.dot(ges[kb], gw_ref[kb], preferred_element_type=F32) + gb_ref[kb])
            yield
        for kb in blocks:
            y_scr[kb, rows, :] = gls[kb][:, :LANES] * jax.nn.sigmoid(gls[kb][:, LANES:])
        yield

    _drain(expand(0))
    for p in range(n_parts):
        side = []
        if p + 1 < n_parts:
            side.append(expand(p + 1))
        if p >= 1:
            side.append(project(p - 1))
        _interleave(scan(p), side, every=2)
    _drain(project(n_parts - 1))
    for kb in blocks:
        hre_scr[kb], him_scr[kb] = state[kb]

    for b in range(bsz):
        y_ref[b] = jnp.concatenate([y_scr[kb, pl.ds(b, tt, stride=bsz), :] for kb in blocks], axis=-1).astype(BF16)

    @pl.when(i == pl.num_programs(0) - 1)
    def _():
        hre_out_ref[...] = hre_scr[...]
        him_out_ref[...] = him_scr[...]


def _s5(u, h0re, h0im, lr, li, bb, cc, dvec, gw, gb, tt):
    bsz, length, _ = u.shape
    grid = (length // tt,)
    st_shape = (S5_BLOCKS, bsz, S5_BLOCK_STATE)
    tok = pl.BlockSpec((bsz, tt, S5_WIDTH), lambda i: (0, i, 0))
    tb_scratch = pltpu.VMEM((S5_BLOCKS, tt * bsz, LANES), F32)
    return pl.pallas_call(
        functools.partial(_s5_kernel, tt=tt, bsz=bsz),
        grid=grid,
        in_specs=[tok, _const_spec(st_shape), _const_spec(st_shape), _const_spec(lr.shape), _const_spec(li.shape),
                  _const_spec(bb.shape), _const_spec(cc.shape), _const_spec(dvec.shape), _const_spec(gw.shape),
                  _const_spec(gb.shape)],
        out_specs=[tok, pl.BlockSpec(st_shape, lambda i: (0, 0, 0)), pl.BlockSpec(st_shape, lambda i: (0, 0, 0))],
        out_shape=[jax.ShapeDtypeStruct(u.shape, BF16),
                   jax.ShapeDtypeStruct(st_shape, F32), jax.ShapeDtypeStruct(st_shape, F32)],
        scratch_shapes=[tb_scratch, tb_scratch, pltpu.VMEM((S5_BLOCKS, tt * bsz, 2 * S5_BLOCK_STATE), F32),
                        pltpu.VMEM(st_shape, F32), pltpu.VMEM(st_shape, F32)],
        compiler_params=_params(1),
        name="s5",
    )(u, h0re, h0im, lr, li, bb, cc, dvec, gw, gb)


FFN_HEAD_ROWS = 32


def _ffn_kernel(x_ref, ys_ref, y5_ref, hist_ref, wout_ref, n1_ref, n2_ref, wup_ref, cw_ref, cb_ref, wdn_ref, n3_ref,
                y_ref, carry_ref, h_scr, hp_scr, permin_scr, out_scr, act_scr, *, tm, nt, whole_seqs):
    nj = tm // SUBLANES
    grp = SUBLANES
    n_slab = D_MODEL // LANES

    @pl.when(lax.rem(pl.program_id(0), nt) == 0)
    def _():
        carry_ref[0] = hist_ref[0]

    def tile_rows(ref, seg0, seg1):
        if whole_seqs:
            return ref[seg0:seg1].reshape((seg1 - seg0) * nj, ref.shape[-1])
        return ref[0, seg0 * nj:seg1 * nj, :]

    def head():
        mix = jnp.dot(tile_rows(ys_ref, 0, grp), wout_ref[0:SSD_WIDTH, :], preferred_element_type=F32)
        yield
        mix = mix + jnp.dot(tile_rows(y5_ref, 0, grp), wout_ref[SSD_WIDTH:, :], preferred_element_type=F32)
        yield
        rb = min(FFN_HEAD_ROWS, nj)
        for seg in range(grp):
            for j0 in range(0, nj, rb):
                rows = slice(seg * nj + j0, seg * nj + j0 + rb)
                if whole_seqs:
                    xr = x_ref[seg, j0:j0 + rb, :]
                else:
                    xr = x_ref[0, rows, :]
                h = xr + _rms(mix[rows], n1_ref[...])
                h_scr[rows, :] = h
                hn = _rms(h, n2_ref[...])
                for sl in range(n_slab):
                    permin_scr[sl, pl.ds(seg + grp * j0, rb, stride=grp), :] = hn[:, sl * LANES:(sl + 1) * LANES]
                yield
        for sl in range(n_slab):
            hp_scr[:, sl * LANES:(sl + 1) * LANES] = permin_scr[sl].astype(BF16)
            yield

    first_sublane = lax.broadcasted_iota(jnp.int32, (grp, FFN_CHUNK), 0) == 0

    def conv_chunk(hp, c0):
        cols = slice(c0, c0 + FFN_CHUNK)
        up = jnp.dot(hp, wup_ref[:, cols], preferred_element_type=F32)

        def wrapped(group, fill_row):
            fill = jnp.broadcast_to(carry_ref[0, fill_row:fill_row + 1, cols], (grp, FFN_CHUNK))
            return jnp.where(first_sublane, fill, pltpu.roll(group, 1, 0))

        if whole_seqs:
            s2 = carry_ref[0, 0:grp, cols]
            s1 = carry_ref[0, grp:2 * grp, cols]
        else:
            s2 = wrapped(up[tm - 2 * grp:tm - grp], grp - 1)
            s1 = wrapped(up[tm - grp:tm], 2 * grp - 1)
        carry_ref[0, :, cols] = up[tm - 2 * grp:tm]
        prev1 = jnp.concatenate([s1, up[:tm - grp]], axis=0)
        prev2 = jnp.concatenate([s2, s1, up[:tm - 2 * grp]], axis=0)
        return (cb_ref[:, cols] + up * cw_ref[2:3, cols] + prev1 * cw_ref[1:2, cols] + prev2 * cw_ref[0:1, cols])

    def ffn():
        hp = hp_scr[...]
        for c in range(D_FF // FFN_CHUNK):
            gate = conv_chunk(hp, c * FFN_CHUNK)
            yield
            val = conv_chunk(hp, D_FF + c * FFN_CHUNK)
            yield
            act_scr[:, c * FFN_CHUNK:(c + 1) * FFN_CHUNK] = (_gelu_tanh(gate) * val).astype(BF16)
            yield
        out = jnp.dot(act_scr[...], wdn_ref[...], preferred_element_type=F32)
        for sl in range(n_slab):
            out_scr[sl] = out[:, sl * LANES:(sl + 1) * LANES]
        yield

    def tail():
        for seg in range(grp):
            rows = slice(seg * nj, (seg + 1) * nj)
            nat = jnp.concatenate([out_scr[sl, pl.ds(seg, nj, stride=grp), :] for sl in range(n_slab)], axis=-1)
            y = h_scr[rows, :] + _rms(nat, n3_ref[...])
            if whole_seqs:
                y_ref[seg] = y
            else:
                y_ref[0, rows, :] = y
            yield

    _drain(head())
    _drain(ffn())
    _drain(tail())


def _ffn(x, ys, y5, ffn_hist, wout, n1, n2, wup, cw, cb, wdn, n3, tm):
    bsz, length, _ = x.shape
    assert FFN_CONV == 3
    whole_seqs = SUBLANES * length <= tm
    last = SUBLANES - 1
    if whole_seqs:
        assert bsz % SUBLANES == 0 and length % SUBLANES == 0 and length >= 2
        blk, nt, tm = (SUBLANES, length), 1, SUBLANES * length
        n_tiles = bsz // SUBLANES
        hist = jnp.transpose(ffn_hist.reshape(n_tiles, SUBLANES, 2, 2 * D_FF), (0, 2, 1, 3)).reshape(
            n_tiles, 2 * SUBLANES, 2 * D_FF)
    else:
        assert length % tm == 0 and tm % SUBLANES == 0 and tm // SUBLANES >= 2
        blk, nt = (1, tm), length // tm
        n_tiles = bsz * nt
        hist = jnp.zeros((bsz, 2 * SUBLANES, 2 * D_FF), F32).at[:, last::SUBLANES, :].set(ffn_hist)
    tok = lambda w: pl.BlockSpec(blk + (w,), lambda s: (s // nt, s % nt, 0))
    per_seq = pl.BlockSpec((1,) + hist.shape[1:], lambda s: (s // nt, 0, 0))
    slabs = pltpu.VMEM((D_MODEL // LANES, tm, LANES), F32)
    y, carry = pl.pallas_call(
        functools.partial(_ffn_kernel, tm=tm, nt=nt, whole_seqs=whole_seqs),
        grid=(n_tiles,),
        in_specs=[tok(D_MODEL), tok(SSD_WIDTH), tok(S5_WIDTH), per_seq,
                  _const_spec(wout.shape), _const_spec(n1.shape), _const_spec(n2.shape), _const_spec(wup.shape),
                  _const_spec(cw.shape), _const_spec(cb.shape), _const_spec(wdn.shape), _const_spec(n3.shape)],
        out_specs=[tok(D_MODEL), per_seq],
        out_shape=[jax.ShapeDtypeStruct((bsz, length, D_MODEL), F32), jax.ShapeDtypeStruct(hist.shape, F32)],
        scratch_shapes=[pltpu.VMEM((tm, D_MODEL), F32), pltpu.VMEM((tm, D_MODEL), BF16), slabs, slabs,
                        pltpu.VMEM((tm, D_FF), BF16)],
        compiler_params=_params(1),
        name="outffn",
    )(x, ys, y5, hist, wout, n1, n2, wup, cw, cb, wdn, n3)
    if whole_seqs:
        new_hist = jnp.transpose(carry.reshape(n_tiles, 2, SUBLANES, 2 * D_FF), (0, 2, 1, 3)).reshape(
            bsz, FFN_CONV - 1, 2 * D_FF)
    else:
        new_hist = carry[:, last::SUBLANES, :]
    return y, new_hist


def _block_diag(w):
    _, r, c = w.shape
    w4 = w.reshape(S5_BLOCKS, S5_GPB, r, c)
    eye = jnp.eye(S5_GPB, dtype=w.dtype)
    return jnp.einsum("kgrc,gh->kgrhc", w4, eye).reshape(S5_BLOCKS, S5_GPB * r, S5_GPB * c)


def _prep_weights(pre_mix_norm_w, w_in, ssd_conv_w, ssd_conv_b, ssd_dt_bias, ssd_a_log, ssd_d, ssd_norm_w,
                  s5_lambda_re, s5_lambda_im, s5_log_dt, s5_b_re, s5_b_im, s5_c_re, s5_c_im, s5_d,
                  s5_glu_w, s5_glu_b, w_out, post_mix_norm_w, pre_ffn_norm_w, w_up, ffn_conv_w, ffn_conv_b,
                  w_down, post_ffn_norm_w):
    o_dt = SSD_WIDTH + SSD_CONV_DIM
    o_u = o_dt + SSD_HEADS
    pad_heads = lambda v: jnp.pad(v, (0, LANES - SSD_HEADS)).reshape(1, LANES)
    w = dict(
        nw0=pre_mix_norm_w.reshape(1, D_MODEL),
        wa=w_in[:, :o_dt].astype(BF16),
        wdt=jnp.pad(w_in[:, o_dt:o_u], ((0, 0), (0, LANES - SSD_HEADS))).astype(BF16),
        wu=w_in[:, o_u:].astype(BF16),
        cw=ssd_conv_w, cb=ssd_conv_b.reshape(1, SSD_CONV_DIM),
        dtb=pad_heads(ssd_dt_bias), alog=pad_heads(ssd_a_log),
        dssd=jnp.repeat(ssd_d, SSD_HEAD_DIM).reshape(1, SSD_WIDTH),
        nssd=ssd_norm_w.reshape(1, SSD_WIDTH),
        wout=w_out.astype(BF16), n1=post_mix_norm_w.reshape(1, D_MODEL), n2=pre_ffn_norm_w.reshape(1, D_MODEL),
        wup=w_up.astype(BF16), fcw=ffn_conv_w, fcb=ffn_conv_b.reshape(1, 2 * D_FF),
        wdn=w_down.astype(BF16), n3=post_ffn_norm_w.reshape(1, D_MODEL),
    )
    dt = jnp.exp(s5_log_dt)[:, None]
    mag = jnp.exp(s5_lambda_re * dt)
    ang = s5_lambda_im * dt
    lb_re = mag * jnp.cos(ang)
    lb_im = mag * jnp.sin(ang)
    den = s5_lambda_re * s5_lambda_re + s5_lambda_im * s5_lambda_im
    q_re = ((lb_re - 1) * s5_lambda_re + lb_im * s5_lambda_im) / den
    q_im = (lb_im * s5_lambda_re - (lb_re - 1) * s5_lambda_im) / den
    bb_re = q_re[..., None] * s5_b_re - q_im[..., None] * s5_b_im
    bb_im = q_re[..., None] * s5_b_im + q_im[..., None] * s5_b_re
    to_in = lambda m: _block_diag(jnp.swapaxes(m, 1, 2))
    w["bb"] = jnp.concatenate([to_in(bb_re), to_in(bb_im)], axis=-1).astype(BF16)
    to_out = lambda m: _block_diag(jnp.swapaxes(m, 1, 2))
    w["cc"] = jnp.concatenate([to_out(s5_c_re), to_out(-s5_c_im)], axis=1).astype(BF16)
    w["lr"] = lb_re.reshape(S5_BLOCKS, 1, S5_BLOCK_STATE)
    w["li"] = lb_im.reshape(S5_BLOCKS, 1, S5_BLOCK_STATE)
    w["d5"] = s5_d.reshape(1, S5_WIDTH)
    w["gw"] = jnp.concatenate([_block_diag(s5_glu_w[..., :S5_GROUP_CH]), _block_diag(s5_glu_w[..., S5_GROUP_CH:])],
                              axis=-1).astype(BF16)
    gb = lambda v: v.reshape(S5_BLOCKS, 1, LANES)
    w["gb"] = jnp.concatenate([gb(s5_glu_b[:, :S5_GROUP_CH]), gb(s5_glu_b[:, S5_GROUP_CH:])], axis=-1)
    return w


def _hist_tile(hist):
    return jnp.pad(hist, ((0, 0), (HIST_ROWS - hist.shape[1], 0), (0, 0)))


def _layer(x, conv_hist, ssd_h0, s5_re, s5_im, ffn_hist, w, *, tm_in, tm, q, tt, nb, nc):
    bsz, length, _ = x.shape
    h0 = jnp.transpose(ssd_h0, (0, 3, 1, 2)).reshape(bsz, SSD_STATE, SSD_WIDTH)
    if length % tm == 0 and length // tm >= 2:
        u, y_ssd, tail, st = _mixer(x, _hist_tile(conv_hist), h0, w["nw0"], w["wa"], w["wdt"], w["wu"], w["cw"],
                                    w["cb"], w["dtb"], w["alog"], w["dssd"], w["nssd"], tm, q, nb)
        new_conv = tail[:, HIST_ROWS - (SSD_CONV - 1):, :]
    else:
        z, xbc, dt, u = _inproj(x, w["nw0"], w["wa"], w["wdt"], w["wu"], tm_in)
        new_conv = xbc[:, length - (SSD_CONV - 1):, :].astype(F32)
        lpad = -length % q
        if lpad:
            assert length < q, "a padded sequence must fit one SSD chunk"
            padt = lambda a: jnp.pad(a, ((0, 0), (0, lpad), (0, 0)))
            z, xbc, dt = padt(z), padt(xbc), padt(dt)
        y_ssd, st = _ssd(z, xbc, dt, _hist_tile(conv_hist), h0, w["cw"], w["cb"], w["dtb"], w["alog"], w["dssd"],
                         w["nssd"], q, length if lpad else q, bsz if lpad else nb, nc)
        y_ssd = y_ssd[:, :length]
    new_ssd = jnp.transpose(st.reshape(bsz, SSD_STATE, SSD_HEADS, SSD_HEAD_DIM), (0, 2, 3, 1))

    to_blocks = lambda s: jnp.transpose(s.reshape(bsz, S5_BLOCKS, S5_BLOCK_STATE), (1, 0, 2))
    from_blocks = lambda s: jnp.transpose(s, (1, 0, 2)).reshape(bsz, S5_GROUPS, S5_STATE)
    y5, hre, him = _s5(u, to_blocks(s5_re), to_blocks(s5_im), w["lr"], w["li"], w["bb"], w["cc"], w["d5"],
                       w["gw"], w["gb"], tt)

    y, new_ffn = _ffn(x, y_ssd, y5, ffn_hist, w["wout"], w["n1"], w["n2"], w["wup"], w["fcw"], w["fcb"],
                      w["wdn"], w["n3"], tm)
    return y, new_conv, new_ssd, from_blocks(hre), from_blocks(him), new_ffn


def _tiles(length):
    q = 128
    nc = max(1, min(SSD_CHUNKS_PER_STEP, length // q))
    tt = min(length, 128)
    assert length % tt == 0 and length >= SSD_CONV - 1
    return dict(tm_in=2 * ROW_TILE, tm=ROW_TILE, q=q, tt=tt, nb=SSD_SEQS_PER_STEP, nc=nc)


def kernel(x_prompt, x_sample, cache_ssd_conv, state_ssd, state_s5_re, state_s5_im, cache_ffn_conv, pre_mix_norm_w, w_in, ssd_conv_w, ssd_conv_b, ssd_dt_bias, ssd_a_log, ssd_d, ssd_norm_w, s5_lambda_re, s5_lambda_im, s5_log_dt, s5_b_re, s5_b_im, s5_c_re, s5_c_im, s5_d, s5_glu_w, s5_glu_b, w_out, post_mix_norm_w, pre_ffn_norm_w, w_up, ffn_conv_w, ffn_conv_b, w_down, post_ffn_norm_w):
    depth = w_in.shape[0]
    bsz = x_prompt.shape[0]
    dtp = x_prompt.dtype
    layer_params = (pre_mix_norm_w, w_in, ssd_conv_w, ssd_conv_b, ssd_dt_bias, ssd_a_log, ssd_d, ssd_norm_w,
                    s5_lambda_re, s5_lambda_im, s5_log_dt, s5_b_re, s5_b_im, s5_c_re, s5_c_im, s5_d,
                    s5_glu_w, s5_glu_b, w_out, post_mix_norm_w, pre_ffn_norm_w, w_up, ffn_conv_w, ffn_conv_b,
                    w_down, post_ffn_norm_w)
    y_prompt, y_sample = x_prompt, x_sample
    prompt_states, sample_states = [], []
    for l in range(depth):
        w = _prep_weights(*(p[l] for p in layer_params))
        y_prompt, *ps = _layer(
            y_prompt,
            jnp.zeros((bsz, SSD_CONV - 1, SSD_CONV_DIM), dtp),
            jnp.zeros((bsz, SSD_HEADS, SSD_HEAD_DIM, SSD_STATE), dtp),
            jnp.zeros((bsz, S5_GROUPS, S5_STATE), dtp),
            jnp.zeros((bsz, S5_GROUPS, S5_STATE), dtp),
            jnp.zeros((bsz, FFN_CONV - 1, 2 * D_FF), dtp),
            w, **_tiles(y_prompt.shape[1]))
        y_sample, *ss = _layer(
            y_sample, cache_ssd_conv[l], state_ssd[l], state_s5_re[l], state_s5_im[l], cache_ffn_conv[l],
            w, **_tiles(y_sample.shape[1]))
        prompt_states.append(ps)
        sample_states.append(ss)
    stack = lambda states: tuple(jnp.stack([s[k] for s in states]) for k in range(5))
    return (y_prompt, y_sample) + stack(prompt_states) + stack(sample_states)
```

```python
import functools
import math
from typing import Callable, NamedTuple

import jax
import jax.numpy as jnp
from jax import lax
from jax.experimental import pallas as pl
from jax.experimental.pallas import tpu as pltpu

F32 = jnp.float32
BF16 = jnp.bfloat16

D_MODEL = 1024
SSD_WIDTH = 512
SSD_HEAD_DIM = 64
SSD_HEADS = 8
SSD_GROUPS = 2
SSD_HPG = SSD_HEADS // SSD_GROUPS
SSD_GROUP_W = SSD_WIDTH // SSD_GROUPS
SSD_STATE = 128
SSD_CONV = 4
SSD_CONV_DIM = SSD_WIDTH + 2 * SSD_GROUPS * SSD_STATE
S5_WIDTH = 512
S5_GROUP_CH = 16
S5_GROUPS = 32
S5_STATE = 64
S5_BLOCKS = 4
S5_GPB = S5_GROUPS // S5_BLOCKS
S5_BLOCK_STATE = S5_GPB * S5_STATE
S5_PART_STEPS = 32
D_FF = 2816
FFN_CONV = 3
EPS = 1e-6

LANES = 128
SUBLANES = 8
HIST_ROWS = SUBLANES
VMEM_LIMIT = 56 * 1024 * 1024

ROW_TILE = 512
FFN_CHUNK = 256
SSD_SEQS_PER_STEP = 2
SSD_CHUNKS_PER_STEP = 4


def _rms(x, w):
    ms = jnp.mean(x * x, axis=-1, keepdims=True)
    return x * lax.rsqrt(ms + EPS) * w


def _gelu_tanh(x):
    c = math.sqrt(2.0 / math.pi)
    hx = 0.5 * x
    return hx + hx * jnp.tanh(x * (c + (c * 0.044715) * (x * x)))


def _softplus(x):
    return jnp.maximum(x, 0.0) + jnp.log1p(jnp.exp(-jnp.abs(x)))


def _causal_conv(ext, cur, w_ref, b_ref, c0, rows, taps):
    cols = slice(c0, c0 + LANES)
    ext[HIST_ROWS:HIST_ROWS + rows, :] = cur
    out = b_ref[:, cols] + cur * w_ref[taps - 1:taps, cols]
    for k in range(1, taps):
        out = out + ext[pl.ds(HIST_ROWS - k, rows), :] * w_ref[taps - 1 - k:taps - k, cols]
    ext[0:HIST_ROWS, :] = ext[rows:rows + HIST_ROWS, :]
    return out


def _drain(gen):
    for _ in gen:
        pass


def _interleave(main, sides, every):
    sides = list(sides)
    turn = 0
    for n, _ in enumerate(main, 1):
        while sides and turn < n / every:
            side = sides[turn % len(sides)]
            try:
                next(side)
                turn += 1
            except StopIteration:
                sides.remove(side)
    for side in sides:
        _drain(side)


def _const_spec(shape):
    nd = len(shape)
    return pl.BlockSpec(shape, lambda *_: (0,) * nd, pipeline_mode=pl.Buffered(1))


def _params(n_axes):
    return pltpu.CompilerParams(dimension_semantics=("arbitrary",) * n_axes, vmem_limit_bytes=VMEM_LIMIT)


def _inproj_kernel(x_ref, nw_ref, wa_ref, wdt_ref, wu_ref, z_ref, xbc_ref, dt_ref, u_ref):
    nb, tm, _ = x_ref.shape
    put = lambda ref, v: ref.__setitem__(..., v.reshape(ref.shape))
    xn = _rms(x_ref[...].reshape(nb * tm, D_MODEL), nw_ref[...]).astype(BF16)
    a = jnp.dot(xn, wa_ref[...], preferred_element_type=F32)
    put(z_ref, a[:, :SSD_WIDTH].astype(BF16))
    put(xbc_ref, a[:, SSD_WIDTH:].astype(BF16))
    put(dt_ref, jnp.dot(xn, wdt_ref[...], preferred_element_type=F32))
    put(u_ref, jnp.dot(xn, wu_ref[...], preferred_element_type=F32).astype(BF16))


def _inproj(x, nw, wa, wdt, wu, tm):
    bsz, length, _ = x.shape
    nb = max(1, min(bsz, tm // length))
    tm = min(tm, length)
    grid = (bsz // nb, length // tm)
    tok = lambda w, dt: (jax.ShapeDtypeStruct((bsz, length, w), dt), pl.BlockSpec((nb, tm, w), lambda b, i: (b, i, 0)))
    outs = [tok(SSD_WIDTH, BF16), tok(SSD_CONV_DIM, BF16), tok(LANES, F32), tok(S5_WIDTH, BF16)]
    return pl.pallas_call(
        _inproj_kernel,
        grid=grid,
        in_specs=[pl.BlockSpec((nb, tm, D_MODEL), lambda b, i: (b, i, 0)),
                  _const_spec(nw.shape), _const_spec(wa.shape), _const_spec(wdt.shape), _const_spec(wu.shape)],
        out_specs=[o[1] for o in outs],
        out_shape=[o[0] for o in outs],
        compiler_params=_params(2),
        name="inproj",
    )(x, nw, wa, wdt, wu)


def _split3(v):
    hi = v.astype(BF16)
    r1 = v - hi.astype(F32)
    mid = r1.astype(BF16)
    lo = (r1 - mid.astype(F32)).astype(BF16)
    return hi, mid, lo


def _head_tiles(a):
    q = a.shape[0]
    lane = lax.broadcasted_iota(jnp.int32, (q, LANES), 1)
    tiles = []
    for j in range(SSD_HEADS * SSD_HEAD_DIM // LANES):
        even = jnp.broadcast_to(a[:, 2 * j:2 * j + 1], (q, LANES))
        odd = jnp.broadcast_to(a[:, 2 * j + 1:2 * j + 2], (q, LANES))
        tiles.append(jnp.where(lane < SSD_HEAD_DIM, even, odd))
    return tiles


def _ssd_kernel(z_ref, xbc_ref, dt_ref, hist_ref, h0_ref, cw_ref, cb_ref, dtb_ref, alog_ref, d_ref, nw_ref,
                y_ref, st_out_ref, ext_scr, st_scr, *, q, valid, nb, nc):
    t = pl.program_id(1)
    n_slab = SSD_CONV_DIM // LANES

    @pl.when(t == 0)
    def _():
        for s in range(nb):
            for j in range(n_slab):
                ext_scr[s * n_slab + j, 0:HIST_ROWS, :] = hist_ref[s, :, j * LANES:(j + 1) * LANES]
        st_scr[...] = h0_ref[...]

    def io(s):
        def put_y(rows, v):
            y_ref[s, rows, :] = v
        return SsdIO(z=lambda rows: z_ref[s, rows, :], xbc=lambda rows: xbc_ref[s, rows, :],
                     dt=lambda rows: dt_ref[s, rows, :], put_y=put_y)

    for c in range(nc):
        chains = [_ssd_chunk(s, slice(c * q, (c + 1) * q), io(s), cw_ref, cb_ref, dtb_ref, alog_ref,
                             d_ref, nw_ref, ext_scr, st_scr, q=q, valid=valid) for s in range(nb)]
        for _ in zip(*chains):
            pass

    @pl.when(t == pl.num_programs(1) - 1)
    def _():
        st_out_ref[...] = st_scr[...]


class SsdIO(NamedTuple):
    z: Callable
    xbc: Callable
    dt: Callable
    put_y: Callable


def _ssd_chunk(s, rows, io, cw_ref, cb_ref, dtb_ref, alog_ref, d_ref, nw_ref, ext_scr, st_scr, *, q, valid):
    n_slab = SSD_CONV_DIM // LANES
    st_scr = st_scr.at[s]
    xraw = io.xbc(rows).astype(F32)
    slabs = []
    for j in range(n_slab):
        conv = _causal_conv(ext_scr.at[s * n_slab + j], xraw[:, j * LANES:(j + 1) * LANES], cw_ref, cb_ref,
                            j * LANES, q, SSD_CONV)
        slabs.append(conv * jax.nn.sigmoid(conv))
        if j % 2 == 1:
            yield
    xact = jnp.concatenate(slabs, axis=-1)
    xs = xact[:, :SSD_WIDTH]
    xs_bf = xs.astype(BF16)
    yield

    dt = _softplus(io.dt(rows) + dtb_ref[...])
    row = lax.broadcasted_iota(jnp.int32, (q, q), 0)
    col = lax.broadcasted_iota(jnp.int32, (q, q), 1)
    causal = row >= col
    if valid < q:
        dt = jnp.where(lax.broadcasted_iota(jnp.int32, dt.shape, 0) < valid, dt, 0.0)
    a = -jnp.exp(alog_ref[...])
    tril = jnp.where(causal, 1.0, 0.0).astype(BF16)
    cs = sum(jnp.dot(tril, part, preferred_element_type=F32) for part in _split3(dt * a))
    cs_t = cs.T
    dt_t = dt.T
    yield
    cs_tiles = _head_tiles(cs)
    dt_tiles = _head_tiles(dt)
    cat = lambda tiles: jnp.concatenate(tiles, axis=-1)
    decay_in = cat([jnp.exp(c) for c in cs_tiles])
    w_end = cat([jnp.exp(c[q - 1:q, :] - c) * d for c, d in zip(cs_tiles, dt_tiles)])
    chunk_decay = cat([jnp.exp(c[q - 1:q, :]) for c in cs_tiles])
    yield

    lane = lax.broadcasted_iota(jnp.int32, (1, SSD_GROUP_W), 1)
    ys = []
    for g in range(SSD_GROUPS):
        b_g = xact[:, SSD_WIDTH + g * SSD_STATE:SSD_WIDTH + (g + 1) * SSD_STATE].astype(BF16)
        c_off = SSD_WIDTH + SSD_GROUPS * SSD_STATE
        c_g = xact[:, c_off + g * SSD_STATE:c_off + (g + 1) * SSD_STATE].astype(BF16)
        gs = slice(g * SSD_GROUP_W, (g + 1) * SSD_GROUP_W)
        x_g = xs_bf[:, gs]
        scores = lax.dot_general(c_g, b_g, (((1,), (1,)), ((), ())), preferred_element_type=F32)
        st_g = st_scr[:, gs]
        y_g = jnp.dot(c_g, st_g.astype(BF16), preferred_element_type=F32) * decay_in[:, gs]
        yield
        for j in range(SSD_HPG):
            h = g * SSD_HPG + j
            seg = cs[:, h:h + 1] - cs_t[h:h + 1, :]
            lmat = jnp.where(causal, jnp.exp(seg), 0.0) * dt_t[h:h + 1, :]
            m = (scores * lmat).astype(BF16)
            in_head = (lane >= j * SSD_HEAD_DIM) & (lane < (j + 1) * SSD_HEAD_DIM)
            x_h = jnp.where(in_head, x_g, jnp.zeros_like(x_g))
            y_g = y_g + jnp.dot(m, x_h, preferred_element_type=F32)
            yield
        ys.append(y_g)
        wx = (xs[:, gs] * w_end[:, gs]).astype(BF16)
        new = lax.dot_general(b_g, wx, (((0,), (0,)), ((), ())), preferred_element_type=F32)
        st_scr[:, gs] = st_g * chunk_decay[:, gs] + new
        yield
    y = jnp.concatenate(ys, axis=-1) + d_ref[...] * xs

    zf = io.z(rows).astype(F32)
    gz = y * (zf * jax.nn.sigmoid(zf))
    yield
    nw = nw_ref[...]
    outs = []
    for g in range(SSD_GROUPS):
        gs = slice(g * SSD_GROUP_W, (g + 1) * SSD_GROUP_W)
        outs.append(_rms(gz[:, gs], nw[:, gs]))
    io.put_y(rows, jnp.concatenate(outs, axis=-1).astype(BF16))
    yield


def _ssd(z, xbc, dt, hist, h0, cw, cb, dtb, alog, dvec, nw, q, valid, nb, nc):
    bsz, length, _ = z.shape
    assert bsz % nb == 0 and length % (nc * q) == 0
    grid = (bsz // nb, length // (nc * q))
    tok = lambda w: pl.BlockSpec((nb, nc * q, w), lambda b, i: (b, i, 0))
    per_b = lambda r, w: pl.BlockSpec((nb, r, w), lambda b, i: (b, 0, 0))
    return pl.pallas_call(
        functools.partial(_ssd_kernel, q=q, valid=valid, nb=nb, nc=nc),
        grid=grid,
        in_specs=[tok(SSD_WIDTH), tok(SSD_CONV_DIM), tok(LANES), per_b(HIST_ROWS, SSD_CONV_DIM),
                  per_b(SSD_STATE, SSD_WIDTH), _const_spec(cw.shape), _const_spec(cb.shape), _const_spec(dtb.shape),
                  _const_spec(alog.shape), _const_spec(dvec.shape), _const_spec(nw.shape)],
        out_specs=[tok(SSD_WIDTH), per_b(SSD_STATE, SSD_WIDTH)],
        out_shape=[jax.ShapeDtypeStruct((bsz, length, SSD_WIDTH), BF16),
                   jax.ShapeDtypeStruct((bsz, SSD_STATE, SSD_WIDTH), F32)],
        scratch_shapes=[pltpu.VMEM((nb * SSD_CONV_DIM // LANES, q + HIST_ROWS, LANES), F32),
                        pltpu.VMEM((nb, SSD_STATE, SSD_WIDTH), F32)],
        compiler_params=_params(2),
        name="ssd",
    )(z, xbc, dt, hist, h0, cw, cb, dtb, alog, dvec, nw)


MIX_COLS = 256
MIX_PHASES_PER_PIECE = 5


def _inproj_phases(x_ref, nw_ref, wa_ref, wdt_ref, wu_ref, xn_scr, zx_scr, dtp_scr, u_ref, slot, *, nb, tm):
    for j in range(nb):
        xn_scr[j * tm:(j + 1) * tm, :] = _rms(x_ref[j], nw_ref[...]).astype(BF16)
        yield
    for c0 in range(0, SSD_WIDTH + SSD_CONV_DIM, MIX_COLS):
        cols = slice(c0, c0 + MIX_COLS)
        for j in range(nb):
            rows = slice(j * tm, (j + 1) * tm)
            zx_scr[slot, rows, cols] = jnp.dot(xn_scr[rows, :], wa_ref[:, cols], preferred_element_type=F32)
            yield
    for j in range(nb):
        rows = slice(j * tm, (j + 1) * tm)
        dtp_scr[slot, rows, :] = jnp.dot(xn_scr[rows, :], wdt_ref[...], preferred_element_type=F32)
        yield
    for c0 in range(0, S5_WIDTH, MIX_COLS):
        cols = slice(c0, c0 + MIX_COLS)
        for j in range(nb):
            rows = slice(j * tm, (j + 1) * tm)
            u_ref[j, :, cols] = jnp.dot(xn_scr[rows, :], wu_ref[:, cols], preferred_element_type=F32).astype(BF16)
            yield


def _mixer_kernel(x_ref, hist_ref, h0_ref, nw0_ref, wa_ref, wdt_ref, wu_ref, cw_ref, cb_ref, dtb_ref, alog_ref, d_ref,
                  nw_ref, u_ref, y_ref, tail_ref, st_out_ref, xn_scr, zx_scr, dtp_scr, ext_scr, st_scr,
                  *, tm, q, nb, nt):
    s = pl.program_id(0)
    prev = s - 1
    wslot = lax.rem(s, 2)
    rslot = 1 - wslot
    n_slab = SSD_CONV_DIM // LANES
    seq_pos = lax.rem(prev + nt, nt)

    @pl.when(s == 0)
    def _():
        zx_scr[1] = jnp.zeros(zx_scr.shape[1:], F32)
        dtp_scr[1] = jnp.zeros(dtp_scr.shape[1:], F32)
        ext_scr[...] = jnp.zeros(ext_scr.shape, F32)
        st_scr[...] = jnp.zeros(st_scr.shape, F32)

    @pl.when((seq_pos == 0) & (s > 0))
    def _():
        for j in range(nb):
            for k in range(n_slab):
                ext_scr[j * n_slab + k, 0:HIST_ROWS, :] = hist_ref[j, :, k * LANES:(k + 1) * LANES]
        st_scr[...] = h0_ref[...]

    def io(j):
        base = j * tm
        at = lambda rows: pl.ds(base + rows.start, rows.stop - rows.start)

        def put_y(rows, v):
            y_ref[j, rows, :] = v
        return SsdIO(z=lambda rows: zx_scr[rslot, at(rows), 0:SSD_WIDTH],
                     xbc=lambda rows: zx_scr[rslot, at(rows), SSD_WIDTH:SSD_WIDTH + SSD_CONV_DIM],
                     dt=lambda rows: dtp_scr[rslot, at(rows), :], put_y=put_y)

    inproj = _inproj_phases(x_ref, nw0_ref, wa_ref, wdt_ref, wu_ref, xn_scr, zx_scr, dtp_scr, u_ref, wslot,
                            nb=nb, tm=tm)
    next(inproj)
    phase = 0
    for c in range(tm // q):
        chains = [_ssd_chunk(j, slice(c * q, (c + 1) * q), io(j), cw_ref, cb_ref, dtb_ref, alog_ref, d_ref, nw_ref,
                             ext_scr, st_scr, q=q, valid=q) for j in range(nb)]
        for _ in zip(*chains):
            phase += 1
            if phase % MIX_PHASES_PER_PIECE == 0:
                next(inproj, None)
    for _ in inproj:
        pass

    @pl.when((seq_pos == nt - 1) & (s > 0))
    def _():
        st_out_ref[...] = st_scr[...]
        for j in range(nb):
            for k in range(n_slab):
                tail_ref[j, :, k * LANES:(k + 1) * LANES] = ext_scr[j * n_slab + k, 0:HIST_ROWS, :]


def _mixer(x, hist, h0, nw0, wa, wdt, wu, cw, cb, dtb, alog, dvec, nw, tm, q, nb):
    bsz, length, _ = x.shape
    assert bsz % nb == 0 and length % tm == 0 and tm % q == 0
    nt = length // tm
    n_tiles = (bsz // nb) * nt
    cur = lambda s: jnp.minimum(s, n_tiles - 1)
    prv = lambda s: jnp.maximum(s - 1, 0)
    tile = lambda w, which: pl.BlockSpec((nb, tm, w), lambda s: (which(s) // nt, which(s) % nt, 0))
    per_seq = lambda r, w: pl.BlockSpec((nb, r, w), lambda s: (prv(s) // nt, 0, 0))
    rows = nb * tm
    return pl.pallas_call(
        functools.partial(_mixer_kernel, tm=tm, q=q, nb=nb, nt=nt),
        grid=(n_tiles + 1,),
        in_specs=[tile(D_MODEL, cur), per_seq(HIST_ROWS, SSD_CONV_DIM), per_seq(SSD_STATE, SSD_WIDTH),
                  _const_spec(nw0.shape), _const_spec(wa.shape), _const_spec(wdt.shape), _const_spec(wu.shape),
                  _const_spec(cw.shape), _const_spec(cb.shape), _const_spec(dtb.shape), _const_spec(alog.shape),
                  _const_spec(dvec.shape), _const_spec(nw.shape)],
        out_specs=[tile(S5_WIDTH, cur), tile(SSD_WIDTH, prv), per_seq(HIST_ROWS, SSD_CONV_DIM),
                   per_seq(SSD_STATE, SSD_WIDTH)],
        out_shape=[jax.ShapeDtypeStruct((bsz, length, S5_WIDTH), BF16),
                   jax.ShapeDtypeStruct((bsz, length, SSD_WIDTH), BF16),
                   jax.ShapeDtypeStruct((bsz, HIST_ROWS, SSD_CONV_DIM), F32),
                   jax.ShapeDtypeStruct((bsz, SSD_STATE, SSD_WIDTH), F32)],
        scratch_shapes=[pltpu.VMEM((rows, D_MODEL), BF16),
                        pltpu.VMEM((2, rows, SSD_WIDTH + SSD_CONV_DIM), F32),
                        pltpu.VMEM((2, rows, LANES), F32),
                        pltpu.VMEM((nb * SSD_CONV_DIM // LANES, q + HIST_ROWS, LANES), F32),
                        pltpu.VMEM((nb, SSD_STATE, SSD_WIDTH), F32)],
        compiler_params=_params(1),
        name="mixer",
    )(x, hist, h0, nw0, wa, wdt, wu, cw, cb, dtb, alog, dvec, nw)


def _s5_kernel(u_ref, h0re_ref, h0im_ref, lr_ref, li_ref, bb_ref, cc_ref, d_ref, gw_ref, gb_ref,
               y_ref, hre_out_ref, him_out_ref, u_scr, y_scr, hs_scr, hre_scr, him_scr, *, tt, bsz):
    i = pl.program_id(0)

    @pl.when(i == 0)
    def _():
        hre_scr[...] = h0re_ref[...]
        him_scr[...] = h0im_ref[...]

    half = S5_BLOCK_STATE
    blocks = range(S5_BLOCKS)

    for b in range(bsz):
        ub = u_ref[b].astype(F32)
        for kb in blocks:
            u_scr[kb, pl.ds(b, tt, stride=bsz), :] = ub[:, kb * LANES:(kb + 1) * LANES]

    n_parts = max(1, tt // S5_PART_STEPS)
    part_steps = tt // n_parts
    part_rows = lambda p: slice(p * part_steps * bsz, (p + 1) * part_steps * bsz)
    state = [(hre_scr[kb], him_scr[kb]) for kb in blocks]
    lam = [(jnp.broadcast_to(lr_ref[kb], (bsz, half)), jnp.broadcast_to(li_ref[kb], (bsz, half))) for kb in blocks]

    def expand(p):
        rows = part_rows(p)
        for kb in blocks:
            hs_scr[kb, rows, :] = jnp.dot(u_scr[kb, rows, :].astype(BF16), bb_ref[kb], preferred_element_type=F32)
            yield

    def scan(p):
        for t in range(p * part_steps, (p + 1) * part_steps):
            rows = slice(t * bsz, (t + 1) * bsz)
            for kb in blocks:
                (lr, li), (hr, hi) = lam[kb], state[kb]
                nr = lr * hr - li * hi + hs_scr[kb, rows, 0:half]
                ni = lr * hi + li * hr + hs_scr[kb, rows, half:2 * half]
                hs_scr[kb, rows, 0:half] = nr
                hs_scr[kb, rows, half:2 * half] = ni
                state[kb] = (nr, ni)
            if t % 2 == 1:
                yield

    def project(p):
        rows = part_rows(p)
        ys = []
        for kb in blocks:
            ys.append(jnp.dot(hs_scr[kb, rows, :].astype(BF16), cc_ref[kb], preferred_element_type=F32)
                      + d_ref[:, kb * LANES:(kb + 1) * LANES] * u_scr[kb, rows, :])
            yield
        ges = [_gelu_tanh(y).astype(BF16) for y in ys]
        yield
        gls = []
        for kb in blocks:
            gls.append(jnp.dot(ges[kb], gw_ref[kb], preferred_element_type=F32) + gb_ref[kb])
            yield
        for kb in blocks:
            y_scr[kb, rows, :] = gls[kb][:, :LANES] * jax.nn.sigmoid(gls[kb][:, LANES:])
        yield

    _drain(expand(0))
    for p in range(n_parts):
        side = []
        if p + 1 < n_parts:
            side.append(expand(p + 1))
        if p >= 1:
            side.append(project(p - 1))
        _interleave(scan(p), side, every=2)
    _drain(project(n_parts - 1))
    for kb in blocks:
        hre_scr[kb], him_scr[kb] = state[kb]

    for b in range(bsz):
        y_ref[b] = jnp.concatenate([y_scr[kb, pl.ds(b, tt, stride=bsz), :] for kb in blocks], axis=-1).astype(BF16)

    @pl.when(i == pl.num_programs(0) - 1)
    def _():
        hre_out_ref[...] = hre_scr[...]
        him_out_ref[...] = him_scr[...]


def _s5(u, h0re, h0im, lr, li, bb, cc, dvec, gw, gb, tt):
    bsz, length, _ = u.shape
    grid = (length // tt,)
    st_shape = (S5_BLOCKS, bsz, S5_BLOCK_STATE)
    tok = pl.BlockSpec((bsz, tt, S5_WIDTH), lambda i: (0, i, 0))
    tb_scratch = pltpu.VMEM((S5_BLOCKS, tt * bsz, LANES), F32)
    return pl.pallas_call(
        functools.partial(_s5_kernel, tt=tt, bsz=bsz),
        grid=grid,
        in_specs=[tok, _const_spec(st_shape), _const_spec(st_shape), _const_spec(lr.shape), _const_spec(li.shape),
                  _const_spec(bb.shape), _const_spec(cc.shape), _const_spec(dvec.shape), _const_spec(gw.shape),
                  _const_spec(gb.shape)],
        out_specs=[tok, pl.BlockSpec(st_shape, lambda i: (0, 0, 0)), pl.BlockSpec(st_shape, lambda i: (0, 0, 0))],
        out_shape=[jax.ShapeDtypeStruct(u.shape, BF16),
                   jax.ShapeDtypeStruct(st_shape, F32), jax.ShapeDtypeStruct(st_shape, F32)],
        scratch_shapes=[tb_scratch, tb_scratch, pltpu.VMEM((S5_BLOCKS, tt * bsz, 2 * S5_BLOCK_STATE), F32),
                        pltpu.VMEM(st_shape, F32), pltpu.VMEM(st_shape, F32)],
        compiler_params=_params(1),
        name="s5",
    )(u, h0re, h0im, lr, li, bb, cc, dvec, gw, gb)


FFN_HEAD_ROWS = 32


def _ffn_kernel(x_ref, ys_ref, y5_ref, hist_ref, wout_ref, n1_ref, n2_ref, wup_ref, cw_ref, cb_ref, wdn_ref, n3_ref,
                y_ref, carry_ref, h_scr, hp_scr, permin_scr, out_scr, act_scr, *, tm, nt, whole_seqs):
    nj = tm // SUBLANES
    grp = SUBLANES
    n_slab = D_MODEL // LANES

    @pl.when(lax.rem(pl.program_id(0), nt) == 0)
    def _():
        carry_ref[0] = hist_ref[0]

    def tile_rows(ref, seg0, seg1):
        if whole_seqs:
            return ref[seg0:seg1].reshape((seg1 - seg0) * nj, ref.shape[-1])
        return ref[0, seg0 * nj:seg1 * nj, :]

    def head():
        mix = jnp.dot(tile_rows(ys_ref, 0, grp), wout_ref[0:SSD_WIDTH, :], preferred_element_type=F32)
        yield
        mix = mix + jnp.dot(tile_rows(y5_ref, 0, grp), wout_ref[SSD_WIDTH:, :], preferred_element_type=F32)
        yield
        rb = min(FFN_HEAD_ROWS, nj)
        for seg in range(grp):
            for j0 in range(0, nj, rb):
                rows = slice(seg * nj + j0, seg * nj + j0 + rb)
                if whole_seqs:
                    xr = x_ref[seg, j0:j0 + rb, :]
                else:
                    xr = x_ref[0, rows, :]
                h = xr + _rms(mix[rows], n1_ref[...])
                h_scr[rows, :] = h
                hn = _rms(h, n2_ref[...])
                for sl in range(n_slab):
                    permin_scr[sl, pl.ds(seg + grp * j0, rb, stride=grp), :] = hn[:, sl * LANES:(sl + 1) * LANES]
                yield
        for sl in range(n_slab):
            hp_scr[:, sl * LANES:(sl + 1) * LANES] = permin_scr[sl].astype(BF16)
            yield

    first_sublane = lax.broadcasted_iota(jnp.int32, (grp, FFN_CHUNK), 0) == 0

    def conv_chunk(hp, c0):
        cols = slice(c0, c0 + FFN_CHUNK)
        up = jnp.dot(hp, wup_ref[:, cols], preferred_element_type=F32)

        def wrapped(group, fill_row):
            fill = jnp.broadcast_to(carry_ref[0, fill_row:fill_row + 1, cols], (grp, FFN_CHUNK))
            return jnp.where(first_sublane, fill, pltpu.roll(group, 1, 0))

        if whole_seqs:
            s2 = carry_ref[0, 0:grp, cols]
            s1 = carry_ref[0, grp:2 * grp, cols]
        else:
            s2 = wrapped(up[tm - 2 * grp:tm - grp], grp - 1)
            s1 = wrapped(up[tm - grp:tm], 2 * grp - 1)
        carry_ref[0, :, cols] = up[tm - 2 * grp:tm]
        prev1 = jnp.concatenate([s1, up[:tm - grp]], axis=0)
        prev2 = jnp.concatenate([s2, s1, up[:tm - 2 * grp]], axis=0)
        return (cb_ref[:, cols] + up * cw_ref[2:3, cols] + prev1 * cw_ref[1:2, cols] + prev2 * cw_ref[0:1, cols])

    def ffn():
        hp = hp_scr[...]
        for c in range(D_FF // FFN_CHUNK):
            gate = conv_chunk(hp, c * FFN_CHUNK)
            yield
            val = conv_chunk(hp, D_FF + c * FFN_CHUNK)
            yield
            act_scr[:, c * FFN_CHUNK:(c + 1) * FFN_CHUNK] = (_gelu_tanh(gate) * val).astype(BF16)
            yield
        out = jnp.dot(act_scr[...], wdn_ref[...], preferred_element_type=F32)
        for sl in range(n_slab):
            out_scr[sl] = out[:, sl * LANES:(sl + 1) * LANES]
        yield

    def tail():
        for seg in range(grp):
            rows = slice(seg * nj, (seg + 1) * nj)
            nat = jnp.concatenate([out_scr[sl, pl.ds(seg, nj, stride=grp), :] for sl in range(n_slab)], axis=-1)
            y = h_scr[rows, :] + _rms(nat, n3_ref[...])
            if whole_seqs:
                y_ref[seg] = y
            else:
                y_ref[0, rows, :] = y
            yield

    _drain(head())
    _drain(ffn())
    _drain(tail())


def _ffn(x, ys, y5, ffn_hist, wout, n1, n2, wup, cw, cb, wdn, n3, tm):
    bsz, length, _ = x.shape
    assert FFN_CONV == 3
    whole_seqs = SUBLANES * length <= tm
    last = SUBLANES - 1
    if whole_seqs:
        assert bsz % SUBLANES == 0 and length % SUBLANES == 0 and length >= 2
        blk, nt, tm = (SUBLANES, length), 1, SUBLANES * length
        n_tiles = bsz // SUBLANES
        hist = jnp.transpose(ffn_hist.reshape(n_tiles, SUBLANES, 2, 2 * D_FF), (0, 2, 1, 3)).reshape(
            n_tiles, 2 * SUBLANES, 2 * D_FF)
    else:
        assert length % tm == 0 and tm % SUBLANES == 0 and tm // SUBLANES >= 2
        blk, nt = (1, tm), length // tm
        n_tiles = bsz * nt
        hist = jnp.zeros((bsz, 2 * SUBLANES, 2 * D_FF), F32).at[:, last::SUBLANES, :].set(ffn_hist)
    tok = lambda w: pl.BlockSpec(blk + (w,), lambda s: (s // nt, s % nt, 0))
    per_seq = pl.BlockSpec((1,) + hist.shape[1:], lambda s: (s // nt, 0, 0))
    slabs = pltpu.VMEM((D_MODEL // LANES, tm, LANES), F32)
    y, carry = pl.pallas_call(
        functools.partial(_ffn_kernel, tm=tm, nt=nt, whole_seqs=whole_seqs),
        grid=(n_tiles,),
        in_specs=[tok(D_MODEL), tok(SSD_WIDTH), tok(S5_WIDTH), per_seq,
                  _const_spec(wout.shape), _const_spec(n1.shape), _const_spec(n2.shape), _const_spec(wup.shape),
                  _const_spec(cw.shape), _const_spec(cb.shape), _const_spec(wdn.shape), _const_spec(n3.shape)],
        out_specs=[tok(D_MODEL), per_seq],
        out_shape=[jax.ShapeDtypeStruct((bsz, length, D_MODEL), F32), jax.ShapeDtypeStruct(hist.shape, F32)],
        scratch_shapes=[pltpu.VMEM((tm, D_MODEL), F32), pltpu.VMEM((tm, D_MODEL), BF16), slabs, slabs,
                        pltpu.VMEM((tm, D_FF), BF16)],
        compiler_params=_params(1),
        name="outffn",
    )(x, ys, y5, hist, wout, n1, n2, wup, cw, cb, wdn, n3)
    if whole_seqs:
        new_hist = jnp.transpose(carry.reshape(n_tiles, 2, SUBLANES, 2 * D_FF), (0, 2, 1, 3)).reshape(
            bsz, FFN_CONV - 1, 2 * D_FF)
    else:
        new_hist = carry[:, last::SUBLANES, :]
    return y, new_hist


def _block_diag(w):
    _, r, c = w.shape
    w4 = w.reshape(S5_BLOCKS, S5_GPB, r, c)
    eye = jnp.eye(S5_GPB, dtype=w.dtype)
    return jnp.einsum("kgrc,gh->kgrhc", w4, eye).reshape(S5_BLOCKS, S5_GPB * r, S5_GPB * c)


def _prep_weights(pre_mix_norm_w, w_in, ssd_conv_w, ssd_conv_b, ssd_dt_bias, ssd_a_log, ssd_d, ssd_norm_w,
                  s5_lambda_re, s5_lambda_im, s5_log_dt, s5_b_re, s5_b_im, s5_c_re, s5_c_im, s5_d,
                  s5_glu_w, s5_glu_b, w_out, post_mix_norm_w, pre_ffn_norm_w, w_up, ffn_conv_w, ffn_conv_b,
                  w_down, post_ffn_norm_w):
    o_dt = SSD_WIDTH + SSD_CONV_DIM
    o_u = o_dt + SSD_HEADS
    pad_heads = lambda v: jnp.pad(v, (0, LANES - SSD_HEADS)).reshape(1, LANES)
    w = dict(
        nw0=pre_mix_norm_w.reshape(1, D_MODEL),
        wa=w_in[:, :o_dt].astype(BF16),
        wdt=jnp.pad(w_in[:, o_dt:o_u], ((0, 0), (0, LANES - SSD_HEADS))).astype(BF16),
        wu=w_in[:, o_u:].astype(BF16),
        cw=ssd_conv_w, cb=ssd_conv_b.reshape(1, SSD_CONV_DIM),
        dtb=pad_heads(ssd_dt_bias), alog=pad_heads(ssd_a_log),
        dssd=jnp.repeat(ssd_d, SSD_HEAD_DIM).reshape(1, SSD_WIDTH),
        nssd=ssd_norm_w.reshape(1, SSD_WIDTH),
        wout=w_out.astype(BF16), n1=post_mix_norm_w.reshape(1, D_MODEL), n2=pre_ffn_norm_w.reshape(1, D_MODEL),
        wup=w_up.astype(BF16), fcw=ffn_conv_w, fcb=ffn_conv_b.reshape(1, 2 * D_FF),
        wdn=w_down.astype(BF16), n3=post_ffn_norm_w.reshape(1, D_MODEL),
    )
    dt = jnp.exp(s5_log_dt)[:, None]
    mag = jnp.exp(s5_lambda_re * dt)
    ang = s5_lambda_im * dt
    lb_re = mag * jnp.cos(ang)
    lb_im = mag * jnp.sin(ang)
    den = s5_lambda_re * s5_lambda_re + s5_lambda_im * s5_lambda_im
    q_re = ((lb_re - 1) * s5_lambda_re + lb_im * s5_lambda_im) / den
    q_im = (lb_im * s5_lambda_re - (lb_re - 1) * s5_lambda_im) / den
    bb_re = q_re[..., None] * s5_b_re - q_im[..., None] * s5_b_im
    bb_im = q_re[..., None] * s5_b_im + q_im[..., None] * s5_b_re
    to_in = lambda m: _block_diag(jnp.swapaxes(m, 1, 2))
    w["bb"] = jnp.concatenate([to_in(bb_re), to_in(bb_im)], axis=-1).astype(BF16)
    to_out = lambda m: _block_diag(jnp.swapaxes(m, 1, 2))
    w["cc"] = jnp.concatenate([to_out(s5_c_re), to_out(-s5_c_im)], axis=1).astype(BF16)
    w["lr"] = lb_re.reshape(S5_BLOCKS, 1, S5_BLOCK_STATE)
    w["li"] = lb_im.reshape(S5_BLOCKS, 1, S5_BLOCK_STATE)
    w["d5"] = s5_d.reshape(1, S5_WIDTH)
    w["gw"] = jnp.concatenate([_block_diag(s5_glu_w[..., :S5_GROUP_CH]), _block_diag(s5_glu_w[..., S5_GROUP_CH:])],
                              axis=-1).astype(BF16)
    gb = lambda v: v.reshape(S5_BLOCKS, 1, LANES)
    w["gb"] = jnp.concatenate([gb(s5_glu_b[:, :S5_GROUP_CH]), gb(s5_glu_b[:, S5_GROUP_CH:])], axis=-1)
    return w


def _hist_tile(hist):
    return jnp.pad(hist, ((0, 0), (HIST_ROWS - hist.shape[1], 0), (0, 0)))


def _layer(x, conv_hist, ssd_h0, s5_re, s5_im, ffn_hist, w, *, tm_in, tm, q, tt, nb, nc):
    bsz, length, _ = x.shape
    h0 = jnp.transpose(ssd_h0, (0, 3, 1, 2)).reshape(bsz, SSD_STATE, SSD_WIDTH)
    if length % tm == 0 and length // tm >= 2:
        u, y_ssd, tail, st = _mixer(x, _hist_tile(conv_hist), h0, w["nw0"], w["wa"], w["wdt"], w["wu"], w["cw"],
                                    w["cb"], w["dtb"], w["alog"], w["dssd"], w["nssd"], tm, q, nb)
        new_conv = tail[:, HIST_ROWS - (SSD_CONV - 1):, :]
    else:
        z, xbc, dt, u = _inproj(x, w["nw0"], w["wa"], w["wdt"], w["wu"], tm_in)
        new_conv = xbc[:, length - (SSD_CONV - 1):, :].astype(F32)
        lpad = -length % q
        if lpad:
            assert length < q, "a padded sequence must fit one SSD chunk"
            padt = lambda a: jnp.pad(a, ((0, 0), (0, lpad), (0, 0)))
            z, xbc, dt = padt(z), padt(xbc), padt(dt)
        y_ssd, st = _ssd(z, xbc, dt, _hist_tile(conv_hist), h0, w["cw"], w["cb"], w["dtb"], w["alog"], w["dssd"],
                         w["nssd"], q, length if lpad else q, nb, nc)
        y_ssd = y_ssd[:, :length]
    new_ssd = jnp.transpose(st.reshape(bsz, SSD_STATE, SSD_HEADS, SSD_HEAD_DIM), (0, 2, 3, 1))

    to_blocks = lambda s: jnp.transpose(s.reshape(bsz, S5_BLOCKS, S5_BLOCK_STATE), (1, 0, 2))
    from_blocks = lambda s: jnp.transpose(s, (1, 0, 2)).reshape(bsz, S5_GROUPS, S5_STATE)
    y5, hre, him = _s5(u, to_blocks(s5_re), to_blocks(s5_im), w["lr"], w["li"], w["bb"], w["cc"], w["d5"],
                       w["gw"], w["gb"], tt)

    y, new_ffn = _ffn(x, y_ssd, y5, ffn_hist, w["wout"], w["n1"], w["n2"], w["wup"], w["fcw"], w["fcb"],
                      w["wdn"], w["n3"], tm)
    return y, new_conv, new_ssd, from_blocks(hre), from_blocks(him), new_ffn


def _tiles(length):
    q = 128
    nc = max(1, min(SSD_CHUNKS_PER_STEP, length // q))
    tt = min(length, 128)
    assert length % tt == 0 and length >= SSD_CONV - 1
    return dict(tm_in=2 * ROW_TILE, tm=ROW_TILE, q=q, tt=tt, nb=SSD_SEQS_PER_STEP, nc=nc)


def kernel(x_prompt, x_sample, cache_ssd_conv, state_ssd, state_s5_re, state_s5_im, cache_ffn_conv, pre_mix_norm_w, w_in, ssd_conv_w, ssd_conv_b, ssd_dt_bias, ssd_a_log, ssd_d, ssd_norm_w, s5_lambda_re, s5_lambda_im, s5_log_dt, s5_b_re, s5_b_im, s5_c_re, s5_c_im, s5_d, s5_glu_w, s5_glu_b, w_out, post_mix_norm_w, pre_ffn_norm_w, w_up, ffn_conv_w, ffn_conv_b, w_down, post_ffn_norm_w):
    depth = w_in.shape[0]
    bsz = x_prompt.shape[0]
    dtp = x_prompt.dtype
    layer_params = (pre_mix_norm_w, w_in, ssd_conv_w, ssd_conv_b, ssd_dt_bias, ssd_a_log, ssd_d, ssd_norm_w,
                    s5_lambda_re, s5_lambda_im, s5_log_dt, s5_b_re, s5_b_im, s5_c_re, s5_c_im, s5_d,
                    s5_glu_w, s5_glu_b, w_out, post_mix_norm_w, pre_ffn_norm_w, w_up, ffn_conv_w, ffn_conv_b,
                    w_down, post_ffn_norm_w)
    y_prompt, y_sample = x_prompt, x_sample
    prompt_states, sample_states = [], []
    for l in range(depth):
        w = _prep_weights(*(p[l] for p in layer_params))
        y_prompt, *ps = _layer(
            y_prompt,
            jnp.zeros((bsz, SSD_CONV - 1, SSD_CONV_DIM), dtp),
            jnp.zeros((bsz, SSD_HEADS, SSD_HEAD_DIM, SSD_STATE), dtp),
            jnp.zeros((bsz, S5_GROUPS, S5_STATE), dtp),
            jnp.zeros((bsz, S5_GROUPS, S5_STATE), dtp),
            jnp.zeros((bsz, FFN_CONV - 1, 2 * D_FF), dtp),
            w, **_tiles(y_prompt.shape[1]))
        y_sample, *ss = _layer(
            y_sample, cache_ssd_conv[l], state_ssd[l], state_s5_re[l], state_s5_im[l], cache_ffn_conv[l],
            w, **_tiles(y_sample.shape[1]))
        prompt_states.append(ps)
        sample_states.append(ss)
    stack = lambda states: tuple(jnp.stack([s[k] for s in states]) for k in range(5))
    return (y_prompt, y_sample) + stack(prompt_states) + stack(sample_states)
```

```python
import functools
import math
from typing import Callable, NamedTuple

import jax
import jax.numpy as jnp
from jax import lax
from jax.experimental import pallas as pl
from jax.experimental.pallas import tpu as pltpu

F32 = jnp.float32
BF16 = jnp.bfloat16

D_MODEL = 1024
SSD_WIDTH = 512
SSD_HEAD_DIM = 64
SSD_HEADS = 8
SSD_GROUPS = 2
SSD_HPG = SSD_HEADS // SSD_GROUPS
SSD_GROUP_W = SSD_WIDTH // SSD_GROUPS
SSD_STATE = 128
SSD_CONV = 4
SSD_CONV_DIM = SSD_WIDTH + 2 * SSD_GROUPS * SSD_STATE
S5_WIDTH = 512
S5_GROUP_CH = 16
S5_GROUPS = 32
S5_STATE = 64
S5_BLOCKS = 4
S5_GPB = S5_GROUPS // S5_BLOCKS
S5_BLOCK_STATE = S5_GPB * S5_STATE
S5_PART_STEPS = 64
D_FF = 2816
FFN_CONV = 3
EPS = 1e-6

LANES = 128
SUBLANES = 8
HIST_ROWS = SUBLANES
VMEM_LIMIT = 56 * 1024 * 1024

ROW_TILE = 512
FFN_CHUNK = 256
SSD_SEQS_PER_STEP = 2
SSD_CHUNKS_PER_STEP = 4


def _rms(x, w):
    ms = jnp.mean(x * x, axis=-1, keepdims=True)
    return x * lax.rsqrt(ms + EPS) * w


def _gelu_tanh(x):
    c = math.sqrt(2.0 / math.pi)
    hx = 0.5 * x
    return hx + hx * jnp.tanh(x * (c + (c * 0.044715) * (x * x)))


def _softplus(x):
    return jnp.maximum(x, 0.0) + jnp.log1p(jnp.exp(-jnp.abs(x)))


def _causal_conv(ext, cur, w_ref, b_ref, c0, rows, taps):
    cols = slice(c0, c0 + LANES)
    ext[HIST_ROWS:HIST_ROWS + rows, :] = cur
    out = b_ref[:, cols] + cur * w_ref[taps - 1:taps, cols]
    for k in range(1, taps):
        out = out + ext[pl.ds(HIST_ROWS - k, rows), :] * w_ref[taps - 1 - k:taps - k, cols]
    ext[0:HIST_ROWS, :] = ext[rows:rows + HIST_ROWS, :]
    return out


def _drain(gen):
    for _ in gen:
        pass


def _interleave(main, sides, every):
    sides = list(sides)
    turn = 0
    for n, _ in enumerate(main, 1):
        while sides and turn < n / every:
            side = sides[turn % len(sides)]
            try:
                next(side)
                turn += 1
            except StopIteration:
                sides.remove(side)
    for side in sides:
        _drain(side)


def _const_spec(shape):
    nd = len(shape)
    return pl.BlockSpec(shape, lambda *_: (0,) * nd, pipeline_mode=pl.Buffered(1))


def _params(n_axes):
    return pltpu.CompilerParams(dimension_semantics=("arbitrary",) * n_axes, vmem_limit_bytes=VMEM_LIMIT)


def _inproj_kernel(x_ref, nw_ref, wa_ref, wdt_ref, wu_ref, z_ref, xbc_ref, dt_ref, u_ref):
    nb, tm, _ = x_ref.shape
    put = lambda ref, v: ref.__setitem__(..., v.reshape(ref.shape))
    xn = _rms(x_ref[...].reshape(nb * tm, D_MODEL), nw_ref[...]).astype(BF16)
    a = jnp.dot(xn, wa_ref[...], preferred_element_type=F32)
    put(z_ref, a[:, :SSD_WIDTH].astype(BF16))
    put(xbc_ref, a[:, SSD_WIDTH:].astype(BF16))
    put(dt_ref, jnp.dot(xn, wdt_ref[...], preferred_element_type=F32))
    put(u_ref, jnp.dot(xn, wu_ref[...], preferred_element_type=F32).astype(BF16))


def _inproj(x, nw, wa, wdt, wu, tm):
    bsz, length, _ = x.shape
    nb = max(1, min(bsz, tm // length))
    tm = min(tm, length)
    grid = (bsz // nb, length // tm)
    tok = lambda w, dt: (jax.ShapeDtypeStruct((bsz, length, w), dt), pl.BlockSpec((nb, tm, w), lambda b, i: (b, i, 0)))
    outs = [tok(SSD_WIDTH, BF16), tok(SSD_CONV_DIM, BF16), tok(LANES, F32), tok(S5_WIDTH, BF16)]
    return pl.pallas_call(
        _inproj_kernel,
        grid=grid,
        in_specs=[pl.BlockSpec((nb, tm, D_MODEL), lambda b, i: (b, i, 0)),
                  _const_spec(nw.shape), _const_spec(wa.shape), _const_spec(wdt.shape), _const_spec(wu.shape)],
        out_specs=[o[1] for o in outs],
        out_shape=[o[0] for o in outs],
        compiler_params=_params(2),
        name="inproj",
    )(x, nw, wa, wdt, wu)


def _split3(v):
    hi = v.astype(BF16)
    r1 = v - hi.astype(F32)
    mid = r1.astype(BF16)
    lo = (r1 - mid.astype(F32)).astype(BF16)
    return hi, mid, lo


def _head_tiles(a):
    q = a.shape[0]
    lane = lax.broadcasted_iota(jnp.int32, (q, LANES), 1)
    tiles = []
    for j in range(SSD_HEADS * SSD_HEAD_DIM // LANES):
        even = jnp.broadcast_to(a[:, 2 * j:2 * j + 1], (q, LANES))
        odd = jnp.broadcast_to(a[:, 2 * j + 1:2 * j + 2], (q, LANES))
        tiles.append(jnp.where(lane < SSD_HEAD_DIM, even, odd))
    return tiles


def _ssd_kernel(z_ref, xbc_ref, dt_ref, hist_ref, h0_ref, cw_ref, cb_ref, dtb_ref, alog_ref, d_ref, nw_ref,
                y_ref, st_out_ref, ext_scr, st_scr, *, q, valid, nb, nc):
    t = pl.program_id(1)
    n_slab = SSD_CONV_DIM // LANES

    @pl.when(t == 0)
    def _():
        for s in range(nb):
            for j in range(n_slab):
                ext_scr[s * n_slab + j, 0:HIST_ROWS, :] = hist_ref[s, :, j * LANES:(j + 1) * LANES]
        st_scr[...] = h0_ref[...]

    def io(s):
        def put_y(rows, v):
            y_ref[s, rows, :] = v
        return SsdIO(z=lambda rows: z_ref[s, rows, :], xbc=lambda rows: xbc_ref[s, rows, :],
                     dt=lambda rows: dt_ref[s, rows, :], put_y=put_y)

    for c in range(nc):
        chains = [_ssd_chunk(s, slice(c * q, (c + 1) * q), io(s), cw_ref, cb_ref, dtb_ref, alog_ref,
                             d_ref, nw_ref, ext_scr, st_scr, q=q, valid=valid) for s in range(nb)]
        for _ in zip(*chains):
            pass

    @pl.when(t == pl.num_programs(1) - 1)
    def _():
        st_out_ref[...] = st_scr[...]


class SsdIO(NamedTuple):
    z: Callable
    xbc: Callable
    dt: Callable
    put_y: Callable


def _ssd_chunk(s, rows, io, cw_ref, cb_ref, dtb_ref, alog_ref, d_ref, nw_ref, ext_scr, st_scr, *, q, valid):
    n_slab = SSD_CONV_DIM // LANES
    st_scr = st_scr.at[s]
    xraw = io.xbc(rows).astype(F32)
    slabs = []
    for j in range(n_slab):
        conv = _causal_conv(ext_scr.at[s * n_slab + j], xraw[:, j * LANES:(j + 1) * LANES], cw_ref, cb_ref,
                            j * LANES, q, SSD_CONV)
        slabs.append(conv * jax.nn.sigmoid(conv))
        if j % 2 == 1:
            yield
    xact = jnp.concatenate(slabs, axis=-1)
    xs = xact[:, :SSD_WIDTH]
    xs_bf = xs.astype(BF16)
    yield

    dt = _softplus(io.dt(rows) + dtb_ref[...])
    row = lax.broadcasted_iota(jnp.int32, (q, q), 0)
    col = lax.broadcasted_iota(jnp.int32, (q, q), 1)
    causal = row >= col
    if valid < q:
        dt = jnp.where(lax.broadcasted_iota(jnp.int32, dt.shape, 0) < valid, dt, 0.0)
    a = -jnp.exp(alog_ref[...])
    tril = jnp.where(causal, 1.0, 0.0).astype(BF16)
    cs = sum(jnp.dot(tril, part, preferred_element_type=F32) for part in _split3(dt * a))
    cs_t = cs.T
    dt_t = dt.T
    yield
    cs_tiles = _head_tiles(cs)
    dt_tiles = _head_tiles(dt)
    cat = lambda tiles: jnp.concatenate(tiles, axis=-1)
    decay_in = cat([jnp.exp(c) for c in cs_tiles])
    w_end = cat([jnp.exp(c[q - 1:q, :] - c) * d for c, d in zip(cs_tiles, dt_tiles)])
    chunk_decay = cat([jnp.exp(c[q - 1:q, :]) for c in cs_tiles])
    yield

    lane = lax.broadcasted_iota(jnp.int32, (1, SSD_GROUP_W), 1)
    ys = []
    for g in range(SSD_GROUPS):
        b_g = xact[:, SSD_WIDTH + g * SSD_STATE:SSD_WIDTH + (g + 1) * SSD_STATE].astype(BF16)
        c_off = SSD_WIDTH + SSD_GROUPS * SSD_STATE
        c_g = xact[:, c_off + g * SSD_STATE:c_off + (g + 1) * SSD_STATE].astype(BF16)
        gs = slice(g * SSD_GROUP_W, (g + 1) * SSD_GROUP_W)
        x_g = xs_bf[:, gs]
        scores = lax.dot_general(c_g, b_g, (((1,), (1,)), ((), ())), preferred_element_type=F32)
        st_g = st_scr[:, gs]
        y_g = jnp.dot(c_g, st_g.astype(BF16), preferred_element_type=F32) * decay_in[:, gs]
        yield
        for j in range(SSD_HPG):
            h = g * SSD_HPG + j
            seg = cs[:, h:h + 1] - cs_t[h:h + 1, :]
            lmat = jnp.where(causal, jnp.exp(seg), 0.0) * dt_t[h:h + 1, :]
            m = (scores * lmat).astype(BF16)
            in_head = (lane >= j * SSD_HEAD_DIM) & (lane < (j + 1) * SSD_HEAD_DIM)
            x_h = jnp.where(in_head, x_g, jnp.zeros_like(x_g))
            y_g = y_g + jnp.dot(m, x_h, preferred_element_type=F32)
            yield
        ys.append(y_g)
        wx = (xs[:, gs] * w_end[:, gs]).astype(BF16)
        new = lax.dot_general(b_g, wx, (((0,), (0,)), ((), ())), preferred_element_type=F32)
        st_scr[:, gs] = st_g * chunk_decay[:, gs] + new
        yield
    y = jnp.concatenate(ys, axis=-1) + d_ref[...] * xs

    zf = io.z(rows).astype(F32)
    gz = y * (zf * jax.nn.sigmoid(zf))
    yield
    nw = nw_ref[...]
    outs = []
    for g in range(SSD_GROUPS):
        gs = slice(g * SSD_GROUP_W, (g + 1) * SSD_GROUP_W)
        outs.append(_rms(gz[:, gs], nw[:, gs]))
    io.put_y(rows, jnp.concatenate(outs, axis=-1).astype(BF16))
    yield


def _ssd(z, xbc, dt, hist, h0, cw, cb, dtb, alog, dvec, nw, q, valid, nb, nc):
    bsz, length, _ = z.shape
    assert bsz % nb == 0 and length % (nc * q) == 0
    grid = (bsz // nb, length // (nc * q))
    tok = lambda w: pl.BlockSpec((nb, nc * q, w), lambda b, i: (b, i, 0))
    per_b = lambda r, w: pl.BlockSpec((nb, r, w), lambda b, i: (b, 0, 0))
    return pl.pallas_call(
        functools.partial(_ssd_kernel, q=q, valid=valid, nb=nb, nc=nc),
        grid=grid,
        in_specs=[tok(SSD_WIDTH), tok(SSD_CONV_DIM), tok(LANES), per_b(HIST_ROWS, SSD_CONV_DIM),
                  per_b(SSD_STATE, SSD_WIDTH), _const_spec(cw.shape), _const_spec(cb.shape), _const_spec(dtb.shape),
                  _const_spec(alog.shape), _const_spec(dvec.shape), _const_spec(nw.shape)],
        out_specs=[tok(SSD_WIDTH), per_b(SSD_STATE, SSD_WIDTH)],
        out_shape=[jax.ShapeDtypeStruct((bsz, length, SSD_WIDTH), BF16),
                   jax.ShapeDtypeStruct((bsz, SSD_STATE, SSD_WIDTH), F32)],
        scratch_shapes=[pltpu.VMEM((nb * SSD_CONV_DIM // LANES, q + HIST_ROWS, LANES), F32),
                        pltpu.VMEM((nb, SSD_STATE, SSD_WIDTH), F32)],
        compiler_params=_params(2),
        name="ssd",
    )(z, xbc, dt, hist, h0, cw, cb, dtb, alog, dvec, nw)


MIX_COLS = 256
MIX_PHASES_PER_PIECE = 5


def _inproj_phases(x_ref, nw_ref, wa_ref, wdt_ref, wu_ref, xn_scr, zx_scr, dtp_scr, u_ref, slot, *, nb, tm):
    for j in range(nb):
        xn_scr[j * tm:(j + 1) * tm, :] = _rms(x_ref[j], nw_ref[...]).astype(BF16)
        yield
    for c0 in range(0, SSD_WIDTH + SSD_CONV_DIM, MIX_COLS):
        cols = slice(c0, c0 + MIX_COLS)
        for j in range(nb):
            rows = slice(j * tm, (j + 1) * tm)
            zx_scr[slot, rows, cols] = jnp.dot(xn_scr[rows, :], wa_ref[:, cols], preferred_element_type=F32)
            yield
    for j in range(nb):
        rows = slice(j * tm, (j + 1) * tm)
        dtp_scr[slot, rows, :] = jnp.dot(xn_scr[rows, :], wdt_ref[...], preferred_element_type=F32)
        yield
    for c0 in range(0, S5_WIDTH, MIX_COLS):
        cols = slice(c0, c0 + MIX_COLS)
        for j in range(nb):
            rows = slice(j * tm, (j + 1) * tm)
            u_ref[j, :, cols] = jnp.dot(xn_scr[rows, :], wu_ref[:, cols], preferred_element_type=F32).astype(BF16)
            yield


def _mixer_kernel(x_ref, hist_ref, h0_ref, nw0_ref, wa_ref, wdt_ref, wu_ref, cw_ref, cb_ref, dtb_ref, alog_ref, d_ref,
                  nw_ref, u_ref, y_ref, tail_ref, st_out_ref, xn_scr, zx_scr, dtp_scr, ext_scr, st_scr,
                  *, tm, q, nb, nt):
    s = pl.program_id(0)
    prev = s - 1
    wslot = lax.rem(s, 2)
    rslot = 1 - wslot
    n_slab = SSD_CONV_DIM // LANES
    seq_pos = lax.rem(prev + nt, nt)

    @pl.when(s == 0)
    def _():
        zx_scr[1] = jnp.zeros(zx_scr.shape[1:], F32)
        dtp_scr[1] = jnp.zeros(dtp_scr.shape[1:], F32)
        ext_scr[...] = jnp.zeros(ext_scr.shape, F32)
        st_scr[...] = jnp.zeros(st_scr.shape, F32)

    @pl.when((seq_pos == 0) & (s > 0))
    def _():
        for j in range(nb):
            for k in range(n_slab):
                ext_scr[j * n_slab + k, 0:HIST_ROWS, :] = hist_ref[j, :, k * LANES:(k + 1) * LANES]
        st_scr[...] = h0_ref[...]

    def io(j):
        base = j * tm
        at = lambda rows: pl.ds(base + rows.start, rows.stop - rows.start)

        def put_y(rows, v):
            y_ref[j, rows, :] = v
        return SsdIO(z=lambda rows: zx_scr[rslot, at(rows), 0:SSD_WIDTH],
                     xbc=lambda rows: zx_scr[rslot, at(rows), SSD_WIDTH:SSD_WIDTH + SSD_CONV_DIM],
                     dt=lambda rows: dtp_scr[rslot, at(rows), :], put_y=put_y)

    inproj = _inproj_phases(x_ref, nw0_ref, wa_ref, wdt_ref, wu_ref, xn_scr, zx_scr, dtp_scr, u_ref, wslot,
                            nb=nb, tm=tm)
    next(inproj)
    phase = 0
    for c in range(tm // q):
        chains = [_ssd_chunk(j, slice(c * q, (c + 1) * q), io(j), cw_ref, cb_ref, dtb_ref, alog_ref, d_ref, nw_ref,
                             ext_scr, st_scr, q=q, valid=q) for j in range(nb)]
        for _ in zip(*chains):
            phase += 1
            if phase % MIX_PHASES_PER_PIECE == 0:
                next(inproj, None)
    for _ in inproj:
        pass

    @pl.when((seq_pos == nt - 1) & (s > 0))
    def _():
        st_out_ref[...] = st_scr[...]
        for j in range(nb):
            for k in range(n_slab):
                tail_ref[j, :, k * LANES:(k + 1) * LANES] = ext_scr[j * n_slab + k, 0:HIST_ROWS, :]


def _mixer(x, hist, h0, nw0, wa, wdt, wu, cw, cb, dtb, alog, dvec, nw, tm, q, nb):
    bsz, length, _ = x.shape
    assert bsz % nb == 0 and length % tm == 0 and tm % q == 0
    nt = length // tm
    n_tiles = (bsz // nb) * nt
    cur = lambda s: jnp.minimum(s, n_tiles - 1)
    prv = lambda s: jnp.maximum(s - 1, 0)
    tile = lambda w, which: pl.BlockSpec((nb, tm, w), lambda s: (which(s) // nt, which(s) % nt, 0))
    per_seq = lambda r, w: pl.BlockSpec((nb, r, w), lambda s: (prv(s) // nt, 0, 0))
    rows = nb * tm
    return pl.pallas_call(
        functools.partial(_mixer_kernel, tm=tm, q=q, nb=nb, nt=nt),
        grid=(n_tiles + 1,),
        in_specs=[tile(D_MODEL, cur), per_seq(HIST_ROWS, SSD_CONV_DIM), per_seq(SSD_STATE, SSD_WIDTH),
                  _const_spec(nw0.shape), _const_spec(wa.shape), _const_spec(wdt.shape), _const_spec(wu.shape),
                  _const_spec(cw.shape), _const_spec(cb.shape), _const_spec(dtb.shape), _const_spec(alog.shape),
                  _const_spec(dvec.shape), _const_spec(nw.shape)],
        out_specs=[tile(S5_WIDTH, cur), tile(SSD_WIDTH, prv), per_seq(HIST_ROWS, SSD_CONV_DIM),
                   per_seq(SSD_STATE, SSD_WIDTH)],
        out_shape=[jax.ShapeDtypeStruct((bsz, length, S5_WIDTH), BF16),
                   jax.ShapeDtypeStruct((bsz, length, SSD_WIDTH), BF16),
                   jax.ShapeDtypeStruct((bsz, HIST_ROWS, SSD_CONV_DIM), F32),
                   jax.ShapeDtypeStruct((bsz, SSD_STATE, SSD_WIDTH), F32)],
        scratch_shapes=[pltpu.VMEM((rows, D_MODEL), BF16),
                        pltpu.VMEM((2, rows, SSD_WIDTH + SSD_CONV_DIM), F32),
                        pltpu.VMEM((2, rows, LANES), F32),
                        pltpu.VMEM((nb * SSD_CONV_DIM // LANES, q + HIST_ROWS, LANES), F32),
                        pltpu.VMEM((nb, SSD_STATE, SSD_WIDTH), F32)],
        compiler_params=_params(1),
        name="mixer",
    )(x, hist, h0, nw0, wa, wdt, wu, cw, cb, dtb, alog, dvec, nw)


def _s5_kernel(u_ref, h0re_ref, h0im_ref, lr_ref, li_ref, bb_ref, cc_ref, d_ref, gw_ref, gb_ref,
               y_ref, hre_out_ref, him_out_ref, u_scr, y_scr, hs_scr, hre_scr, him_scr, *, tt, bsz):
    i = pl.program_id(0)

    @pl.when(i == 0)
    def _():
        hre_scr[...] = h0re_ref[...]
        him_scr[...] = h0im_ref[...]

    half = S5_BLOCK_STATE
    blocks = range(S5_BLOCKS)

    for b in range(bsz):
        ub = u_ref[b].astype(F32)
        for kb in blocks:
            u_scr[kb, pl.ds(b, tt, stride=bsz), :] = ub[:, kb * LANES:(kb + 1) * LANES]

    n_parts = max(1, tt // S5_PART_STEPS)
    part_steps = tt // n_parts
    part_rows = lambda p: slice(p * part_steps * bsz, (p + 1) * part_steps * bsz)
    state = [(hre_scr[kb], him_scr[kb]) for kb in blocks]
    lam = [(jnp.broadcast_to(lr_ref[kb], (bsz, half)), jnp.broadcast_to(li_ref[kb], (bsz, half))) for kb in blocks]

    def expand(p):
        rows = part_rows(p)
        for kb in blocks:
            hs_scr[kb, rows, :] = jnp.dot(u_scr[kb, rows, :].astype(BF16), bb_ref[kb], preferred_element_type=F32)
            yield

    def scan(p):
        for t in range(p * part_steps, (p + 1) * part_steps):
            rows = slice(t * bsz, (t + 1) * bsz)
            for kb in blocks:
                (lr, li), (hr, hi) = lam[kb], state[kb]
                nr = lr * hr - li * hi + hs_scr[kb, rows, 0:half]
                ni = lr * hi + li * hr + hs_scr[kb, rows, half:2 * half]
                hs_scr[kb, rows, 0:half] = nr
                hs_scr[kb, rows, half:2 * half] = ni
                state[kb] = (nr, ni)
            if t % 2 == 1:
                yield

    def project(p):
        rows = part_rows(p)
        ys = []
        for kb in blocks:
            ys.append(jnp.dot(hs_scr[kb, rows, :].astype(BF16), cc_ref[kb], preferred_element_type=F32)
                      + d_ref[:, kb * LANES:(kb + 1) * LANES] * u_scr[kb, rows, :])
            yield
        ges = [_gelu_tanh(y).astype(BF16) for y in ys]
        yield
        gls = []
        for kb in blocks:
            gls.append(jnp.dot(ges[kb], gw_ref[kb], preferred_element_type=F32) + gb_ref[kb])
            yield
        for kb in blocks:
            y_scr[kb, rows, :] = gls[kb][:, :LANES] * jax.nn.sigmoid(gls[kb][:, LANES:])
        yield

    _drain(expand(0))
    for p in range(n_parts):
        side = []
        if p + 1 < n_parts:
            side.append(expand(p + 1))
        if p >= 1:
            side.append(project(p - 1))
        _interleave(scan(p), side, every=2)
    _drain(project(n_parts - 1))
    for kb in blocks:
        hre_scr[kb], him_scr[kb] = state[kb]

    for b in range(bsz):
        y_ref[b] = jnp.concatenate([y_scr[kb, pl.ds(b, tt, stride=bsz), :] for kb in blocks], axis=-1).astype(BF16)

    @pl.when(i == pl.num_programs(0) - 1)
    def _():
        hre_out_ref[...] = hre_scr[...]
        him_out_ref[...] = him_scr[...]


def _s5(u, h0re, h0im, lr, li, bb, cc, dvec, gw, gb, tt):
    bsz, length, _ = u.shape
    grid = (length // tt,)
    st_shape = (S5_BLOCKS, bsz, S5_BLOCK_STATE)
    tok = pl.BlockSpec((bsz, tt, S5_WIDTH), lambda i: (0, i, 0))
    tb_scratch = pltpu.VMEM((S5_BLOCKS, tt * bsz, LANES), F32)
    return pl.pallas_call(
        functools.partial(_s5_kernel, tt=tt, bsz=bsz),
        grid=grid,
        in_specs=[tok, _const_spec(st_shape), _const_spec(st_shape), _const_spec(lr.shape), _const_spec(li.shape),
                  _const_spec(bb.shape), _const_spec(cc.shape), _const_spec(dvec.shape), _const_spec(gw.shape),
                  _const_spec(gb.shape)],
        out_specs=[tok, pl.BlockSpec(st_shape, lambda i: (0, 0, 0)), pl.BlockSpec(st_shape, lambda i: (0, 0, 0))],
        out_shape=[jax.ShapeDtypeStruct(u.shape, BF16),
                   jax.ShapeDtypeStruct(st_shape, F32), jax.ShapeDtypeStruct(st_shape, F32)],
        scratch_shapes=[tb_scratch, tb_scratch, pltpu.VMEM((S5_BLOCKS, tt * bsz, 2 * S5_BLOCK_STATE), F32),
                        pltpu.VMEM(st_shape, F32), pltpu.VMEM(st_shape, F32)],
        compiler_params=_params(1),
        name="s5",
    )(u, h0re, h0im, lr, li, bb, cc, dvec, gw, gb)


FFN_HEAD_ROWS = 32


def _ffn_kernel(x_ref, ys_ref, y5_ref, hist_ref, wout_ref, n1_ref, n2_ref, wup_ref, cw_ref, cb_ref, wdn_ref, n3_ref,
                y_ref, carry_ref, h_scr, hp_scr, permin_scr, out_scr, act_scr, *, tm, nt, whole_seqs):
    nj = tm // SUBLANES
    grp = SUBLANES
    n_slab = D_MODEL // LANES

    @pl.when(lax.rem(pl.program_id(0), nt) == 0)
    def _():
        carry_ref[0] = hist_ref[0]

    def tile_rows(ref, seg0, seg1):
        if whole_seqs:
            return ref[seg0:seg1].reshape((seg1 - seg0) * nj, ref.shape[-1])
        return ref[0, seg0 * nj:seg1 * nj, :]

    def head():
        mix = jnp.dot(tile_rows(ys_ref, 0, grp), wout_ref[0:SSD_WIDTH, :], preferred_element_type=F32)
        yield
        mix = mix + jnp.dot(tile_rows(y5_ref, 0, grp), wout_ref[SSD_WIDTH:, :], preferred_element_type=F32)
        yield
        rb = min(FFN_HEAD_ROWS, nj)
        for seg in range(grp):
            for j0 in range(0, nj, rb):
                rows = slice(seg * nj + j0, seg * nj + j0 + rb)
                if whole_seqs:
                    xr = x_ref[seg, j0:j0 + rb, :]
                else:
                    xr = x_ref[0, rows, :]
                h = xr + _rms(mix[rows], n1_ref[...])
                h_scr[rows, :] = h
                hn = _rms(h, n2_ref[...])
                for sl in range(n_slab):
                    permin_scr[sl, pl.ds(seg + grp * j0, rb, stride=grp), :] = hn[:, sl * LANES:(sl + 1) * LANES]
                yield
        for sl in range(n_slab):
            hp_scr[:, sl * LANES:(sl + 1) * LANES] = permin_scr[sl].astype(BF16)
            yield

    first_sublane = lax.broadcasted_iota(jnp.int32, (grp, FFN_CHUNK), 0) == 0

    def conv_chunk(hp, c0):
        cols = slice(c0, c0 + FFN_CHUNK)
        up = jnp.dot(hp, wup_ref[:, cols], preferred_element_type=F32)

        def wrapped(group, fill_row):
            fill = jnp.broadcast_to(carry_ref[0, fill_row:fill_row + 1, cols], (grp, FFN_CHUNK))
            return jnp.where(first_sublane, fill, pltpu.roll(group, 1, 0))

        if whole_seqs:
            s2 = carry_ref[0, 0:grp, cols]
            s1 = carry_ref[0, grp:2 * grp, cols]
        else:
            s2 = wrapped(up[tm - 2 * grp:tm - grp], grp - 1)
            s1 = wrapped(up[tm - grp:tm], 2 * grp - 1)
        carry_ref[0, :, cols] = up[tm - 2 * grp:tm]
        prev1 = jnp.concatenate([s1, up[:tm - grp]], axis=0)
        prev2 = jnp.concatenate([s2, s1, up[:tm - 2 * grp]], axis=0)
        return (cb_ref[:, cols] + up * cw_ref[2:3, cols] + prev1 * cw_ref[1:2, cols] + prev2 * cw_ref[0:1, cols])

    def ffn():
        hp = hp_scr[...]
        for c in range(D_FF // FFN_CHUNK):
            gate = conv_chunk(hp, c * FFN_CHUNK)
            yield
            val = conv_chunk(hp, D_FF + c * FFN_CHUNK)
            yield
            act_scr[:, c * FFN_CHUNK:(c + 1) * FFN_CHUNK] = (_gelu_tanh(gate) * val).astype(BF16)
            yield
        out = jnp.dot(act_scr[...], wdn_ref[...], preferred_element_type=F32)
        for sl in range(n_slab):
            out_scr[sl] = out[:, sl * LANES:(sl + 1) * LANES]
        yield

    def tail():
        for seg in range(grp):
            rows = slice(seg * nj, (seg + 1) * nj)
            nat = jnp.concatenate([out_scr[sl, pl.ds(seg, nj, stride=grp), :] for sl in range(n_slab)], axis=-1)
            y = h_scr[rows, :] + _rms(nat, n3_ref[...])
            if whole_seqs:
                y_ref[seg] = y
            else:
                y_ref[0, rows, :] = y
            yield

    _drain(head())
    _drain(ffn())
    _drain(tail())


def _ffn(x, ys, y5, ffn_hist, wout, n1, n2, wup, cw, cb, wdn, n3, tm):
    bsz, length, _ = x.shape
    assert FFN_CONV == 3
    whole_seqs = SUBLANES * length <= tm
    last = SUBLANES - 1
    if whole_seqs:
        assert bsz % SUBLANES == 0 and length % SUBLANES == 0 and length >= 2
        blk, nt, tm = (SUBLANES, length), 1, SUBLANES * length
        n_tiles = bsz // SUBLANES
        hist = jnp.transpose(ffn_hist.reshape(n_tiles, SUBLANES, 2, 2 * D_FF), (0, 2, 1, 3)).reshape(
            n_tiles, 2 * SUBLANES, 2 * D_FF)
    else:
        assert length % tm == 0 and tm % SUBLANES == 0 and tm // SUBLANES >= 2
        blk, nt = (1, tm), length // tm
        n_tiles = bsz * nt
        hist = jnp.zeros((bsz, 2 * SUBLANES, 2 * D_FF), F32).at[:, last::SUBLANES, :].set(ffn_hist)
    tok = lambda w: pl.BlockSpec(blk + (w,), lambda s: (s // nt, s % nt, 0))
    per_seq = pl.BlockSpec((1,) + hist.shape[1:], lambda s: (s // nt, 0, 0))
    slabs = pltpu.VMEM((D_MODEL // LANES, tm, LANES), F32)
    y, carry = pl.pallas_call(
        functools.partial(_ffn_kernel, tm=tm, nt=nt, whole_seqs=whole_seqs),
        grid=(n_tiles,),
        in_specs=[tok(D_MODEL), tok(SSD_WIDTH), tok(S5_WIDTH), per_seq,
                  _const_spec(wout.shape), _const_spec(n1.shape), _const_spec(n2.shape), _const_spec(wup.shape),
                  _const_spec(cw.shape), _const_spec(cb.shape), _const_spec(wdn.shape), _const_spec(n3.shape)],
        out_specs=[tok(D_MODEL), per_seq],
        out_shape=[jax.ShapeDtypeStruct((bsz, length, D_MODEL), F32), jax.ShapeDtypeStruct(hist.shape, F32)],
        scratch_shapes=[pltpu.VMEM((tm, D_MODEL), F32), pltpu.VMEM((tm, D_MODEL), BF16), slabs, slabs,
                        pltpu.VMEM((tm, D_FF), BF16)],
        compiler_params=_params(1),
        name="outffn",
    )(x, ys, y5, hist, wout, n1, n2, wup, cw, cb, wdn, n3)
    if whole_seqs:
        new_hist = jnp.transpose(carry.reshape(n_tiles, 2, SUBLANES, 2 * D_FF), (0, 2, 1, 3)).reshape(
            bsz, FFN_CONV - 1, 2 * D_FF)
    else:
        new_hist = carry[:, last::SUBLANES, :]
    return y, new_hist


def _block_diag(w):
    _, r, c = w.shape
    w4 = w.reshape(S5_BLOCKS, S5_GPB, r, c)
    eye = jnp.eye(S5_GPB, dtype=w.dtype)
    return jnp.einsum("kgrc,gh->kgrhc", w4, eye).reshape(S5_BLOCKS, S5_GPB * r, S5_GPB * c)


def _prep_weights(pre_mix_norm_w, w_in, ssd_conv_w, ssd_conv_b, ssd_dt_bias, ssd_a_log, ssd_d, ssd_norm_w,
                  s5_lambda_re, s5_lambda_im, s5_log_dt, s5_b_re, s5_b_im, s5_c_re, s5_c_im, s5_d,
                  s5_glu_w, s5_glu_b, w_out, post_mix_norm_w, pre_ffn_norm_w, w_up, ffn_conv_w, ffn_conv_b,
                  w_down, post_ffn_norm_w):
    o_dt = SSD_WIDTH + SSD_CONV_DIM
    o_u = o_dt + SSD_HEADS
    pad_heads = lambda v: jnp.pad(v, (0, LANES - SSD_HEADS)).reshape(1, LANES)
    w = dict(
        nw0=pre_mix_norm_w.reshape(1, D_MODEL),
        wa=w_in[:, :o_dt].astype(BF16),
        wdt=jnp.pad(w_in[:, o_dt:o_u], ((0, 0), (0, LANES - SSD_HEADS))).astype(BF16),
        wu=w_in[:, o_u:].astype(BF16),
        cw=ssd_conv_w, cb=ssd_conv_b.reshape(1, SSD_CONV_DIM),
        dtb=pad_heads(ssd_dt_bias), alog=pad_heads(ssd_a_log),
        dssd=jnp.repeat(ssd_d, SSD_HEAD_DIM).reshape(1, SSD_WIDTH),
        nssd=ssd_norm_w.reshape(1, SSD_WIDTH),
        wout=w_out.astype(BF16), n1=post_mix_norm_w.reshape(1, D_MODEL), n2=pre_ffn_norm_w.reshape(1, D_MODEL),
        wup=w_up.astype(BF16), fcw=ffn_conv_w, fcb=ffn_conv_b.reshape(1, 2 * D_FF),
        wdn=w_down.astype(BF16), n3=post_ffn_norm_w.reshape(1, D_MODEL),
    )
    dt = jnp.exp(s5_log_dt)[:, None]
    mag = jnp.exp(s5_lambda_re * dt)
    ang = s5_lambda_im * dt
    lb_re = mag * jnp.cos(ang)
    lb_im = mag * jnp.sin(ang)
    den = s5_lambda_re * s5_lambda_re + s5_lambda_im * s5_lambda_im
    q_re = ((lb_re - 1) * s5_lambda_re + lb_im * s5_lambda_im) / den
    q_im = (lb_im * s5_lambda_re - (lb_re - 1) * s5_lambda_im) / den
    bb_re = q_re[..., None] * s5_b_re - q_im[..., None] * s5_b_im
    bb_im = q_re[..., None] * s5_b_im + q_im[..., None] * s5_b_re
    to_in = lambda m: _block_diag(jnp.swapaxes(m, 1, 2))
    w["bb"] = jnp.concatenate([to_in(bb_re), to_in(bb_im)], axis=-1).astype(BF16)
    to_out = lambda m: _block_diag(jnp.swapaxes(m, 1, 2))
    w["cc"] = jnp.concatenate([to_out(s5_c_re), to_out(-s5_c_im)], axis=1).astype(BF16)
    w["lr"] = lb_re.reshape(S5_BLOCKS, 1, S5_BLOCK_STATE)
    w["li"] = lb_im.reshape(S5_BLOCKS, 1, S5_BLOCK_STATE)
    w["d5"] = s5_d.reshape(1, S5_WIDTH)
    w["gw"] = jnp.concatenate([_block_diag(s5_glu_w[..., :S5_GROUP_CH]), _block_diag(s5_glu_w[..., S5_GROUP_CH:])],
                              axis=-1).astype(BF16)
    gb = lambda v: v.reshape(S5_BLOCKS, 1, LANES)
    w["gb"] = jnp.concatenate([gb(s5_glu_b[:, :S5_GROUP_CH]), gb(s5_glu_b[:, S5_GROUP_CH:])], axis=-1)
    return w


def _hist_tile(hist):
    return jnp.pad(hist, ((0, 0), (HIST_ROWS - hist.shape[1], 0), (0, 0)))


def _layer(x, conv_hist, ssd_h0, s5_re, s5_im, ffn_hist, w, *, tm_in, tm, q, tt, nb, nc):
    bsz, length, _ = x.shape
    h0 = jnp.transpose(ssd_h0, (0, 3, 1, 2)).reshape(bsz, SSD_STATE, SSD_WIDTH)
    if length % tm == 0 and length // tm >= 2:
        u, y_ssd, tail, st = _mixer(x, _hist_tile(conv_hist), h0, w["nw0"], w["wa"], w["wdt"], w["wu"], w["cw"],
                                    w["cb"], w["dtb"], w["alog"], w["dssd"], w["nssd"], tm, q, nb)
        new_conv = tail[:, HIST_ROWS - (SSD_CONV - 1):, :]
    else:
        z, xbc, dt, u = _inproj(x, w["nw0"], w["wa"], w["wdt"], w["wu"], tm_in)
        new_conv = xbc[:, length - (SSD_CONV - 1):, :].astype(F32)
        lpad = -length % q
        if lpad:
            assert length < q, "a padded sequence must fit one SSD chunk"
            padt = lambda a: jnp.pad(a, ((0, 0), (0, lpad), (0, 0)))
            z, xbc, dt = padt(z), padt(xbc), padt(dt)
        y_ssd, st = _ssd(z, xbc, dt, _hist_tile(conv_hist), h0, w["cw"], w["cb"], w["dtb"], w["alog"], w["dssd"],
                         w["nssd"], q, length if lpad else q, nb, nc)
        y_ssd = y_ssd[:, :length]
    new_ssd = jnp.transpose(st.reshape(bsz, SSD_STATE, SSD_HEADS, SSD_HEAD_DIM), (0, 2, 3, 1))

    to_blocks = lambda s: jnp.transpose(s.reshape(bsz, S5_BLOCKS, S5_BLOCK_STATE), (1, 0, 2))
    from_blocks = lambda s: jnp.transpose(s, (1, 0, 2)).reshape(bsz, S5_GROUPS, S5_STATE)
    y5, hre, him = _s5(u, to_blocks(s5_re), to_blocks(s5_im), w["lr"], w["li"], w["bb"], w["cc"], w["d5"],
                       w["gw"], w["gb"], tt)

    y, new_ffn = _ffn(x, y_ssd, y5, ffn_hist, w["wout"], w["n1"], w["n2"], w["wup"], w["fcw"], w["fcb"],
                      w["wdn"], w["n3"], tm)
    return y, new_conv, new_ssd, from_blocks(hre), from_blocks(him), new_ffn


def _tiles(length):
    q = 128
    nc = max(1, min(SSD_CHUNKS_PER_STEP, length // q))
    tt = min(length, 128)
    assert length % tt == 0 and length >= SSD_CONV - 1
    return dict(tm_in=2 * ROW_TILE, tm=ROW_TILE, q=q, tt=tt, nb=SSD_SEQS_PER_STEP, nc=nc)


def kernel(x_prompt, x_sample, cache_ssd_conv, state_ssd, state_s5_re, state_s5_im, cache_ffn_conv, pre_mix_norm_w, w_in, ssd_conv_w, ssd_conv_b, ssd_dt_bias, ssd_a_log, ssd_d, ssd_norm_w, s5_lambda_re, s5_lambda_im, s5_log_dt, s5_b_re, s5_b_im, s5_c_re, s5_c_im, s5_d, s5_glu_w, s5_glu_b, w_out, post_mix_norm_w, pre_ffn_norm_w, w_up, ffn_conv_w, ffn_conv_b, w_down, post_ffn_norm_w):
    depth = w_in.shape[0]
    bsz = x_prompt.shape[0]
    dtp = x_prompt.dtype
    layer_params = (pre_mix_norm_w, w_in, ssd_conv_w, ssd_conv_b, ssd_dt_bias, ssd_a_log, ssd_d, ssd_norm_w,
                    s5_lambda_re, s5_lambda_im, s5_log_dt, s5_b_re, s5_b_im, s5_c_re, s5_c_im, s5_d,
                    s5_glu_w, s5_glu_b, w_out, post_mix_norm_w, pre_ffn_norm_w, w_up, ffn_conv_w, ffn_conv_b,
                    w_down, post_ffn_norm_w)
    y_prompt, y_sample = x_prompt, x_sample
    prompt_states, sample_states = [], []
    for l in range(depth):
        w = _prep_weights(*(p[l] for p in layer_params))
        y_prompt, *ps = _layer(
            y_prompt,
            jnp.zeros((bsz, SSD_CONV - 1, SSD_CONV_DIM), dtp),
            jnp.zeros((bsz, SSD_HEADS, SSD_HEAD_DIM, SSD_STATE), dtp),
            jnp.zeros((bsz, S5_GROUPS, S5_STATE), dtp),
            jnp.zeros((bsz, S5_GROUPS, S5_STATE), dtp),
            jnp.zeros((bsz, FFN_CONV - 1, 2 * D_FF), dtp),
            w, **_tiles(y_prompt.shape[1]))
        y_sample, *ss = _layer(
            y_sample, cache_ssd_conv[l], state_ssd[l], state_s5_re[l], state_s5_im[l], cache_ffn_conv[l],
            w, **_tiles(y_sample.shape[1]))
        prompt_states.append(ps)
        sample_states.append(ss)
    stack = lambda states: tuple(jnp.stack([s[k] for s in states]) for k in range(5))
    return (y_prompt, y_sample) + stack(prompt_states) + stack(sample_states)
```
